```python
import math
import jax
import jax.numpy as jnp
from jax import lax
import numpy as np

D_MODEL = 2048
BATCH = 1
SEQ = 8192
DEPTH = 4

GRID_W = 64
CTX_LEN = 256
ROPE_THETA = 10000.0
NORM_EPS = 1e-6
Q_BLOCK = 128

A_HEADS = 4
A_KV_HEADS = 2
A_HEAD_DIM = 128
A_WIDTH = A_HEADS * A_HEAD_DIM
A_KV_WIDTH = A_KV_HEADS * A_HEAD_DIM
LRU_WIDTH = 512
LRU_BLOCKS = 4
LRU_CONV = 4
LRU_C = 8.0
DN_HEADS = 4
DN_HEAD_DIM = 128
DN_WIDTH = DN_HEADS * DN_HEAD_DIM
DN_CONV = 4
DN_CHUNK = 64
MLA_HEADS = 4
MLA_Q_RANK = 384
MLA_KV_RANK = 256
MLA_NOPE = 128
MLA_ROPE = 64
MLA_V = 128
MLA_WIDTH = MLA_HEADS * MLA_V

MIX_WIDTH = A_WIDTH + LRU_WIDTH + DN_WIDTH + MLA_WIDTH
IN_SPLITS = (
    A_WIDTH, A_KV_WIDTH, A_KV_WIDTH, A_WIDTH,
    LRU_WIDTH, LRU_WIDTH,
    3 * DN_WIDTH, DN_WIDTH, 4 * DN_HEADS,
    MLA_Q_RANK, MLA_KV_RANK, MLA_ROPE, MLA_WIDTH,
)
IN_WIDTH = sum(IN_SPLITS)

kernel_name = 'hybrid_parallel_heads_dit_block'


def _rms_norm(x, w):
    xf = x.astype(jnp.float32)
    y = xf * lax.rsqrt(jnp.mean(xf * xf, axis=-1, keepdims=True) + NORM_EPS)
    return (y * w).astype(x.dtype)


def _l2_norm(x):
    return x * lax.rsqrt(jnp.sum(x * x, axis=-1, keepdims=True) + NORM_EPS)


def _split_heads(x, n):
    b, t, _ = x.shape
    return x.reshape(b, t, n, -1).transpose(0, 2, 1, 3)


def _merge_heads(x):
    b, n, t, d = x.shape
    return x.transpose(0, 2, 1, 3).reshape(b, t, n * d)


def _rope_1d(x, pos):
    half = x.shape[-1] // 2
    inv_freq = ROPE_THETA ** (-jnp.arange(half, dtype=jnp.float32) / half)
    ang = pos.astype(jnp.float32)[:, None] * inv_freq
    cos, sin = jnp.cos(ang), jnp.sin(ang)
    x1, x2 = x[..., :half], x[..., half:]
    return jnp.concatenate([x1 * cos - x2 * sin, x1 * sin + x2 * cos], axis=-1).astype(x.dtype)


def _rope_2d(x, rows, cols):
    half = x.shape[-1] // 2
    return jnp.concatenate([_rope_1d(x[..., :half], rows), _rope_1d(x[..., half:], cols)], axis=-1)


def _attend(q, k, v, scale):
    s = jnp.einsum('bkgqd,bktd->bkgqt', q, k).astype(jnp.float32) * scale
    p = jax.nn.softmax(s, axis=-1).astype(v.dtype)
    return jnp.einsum('bkgqt,bktd->bkgqd', p, v)


def _blocked_attention(q, k, v, scale):
    b, hk, g, s, d = q.shape
    nb = s // Q_BLOCK
    qb = jnp.moveaxis(q.reshape(b, hk, g, nb, Q_BLOCK, d), 3, 0)
    ob = lax.map(lambda qi: _attend(qi, k, v, scale), qb)
    return jnp.moveaxis(ob, 0, 3).reshape(b, hk, g, s, v.shape[-1])


def _dwconv_centred(x, w):
    k_w = w.shape[0]
    left = k_w // 2
    t = x.shape[1]
    xp = jnp.pad(x, ((0, 0), (left, k_w - 1 - left), (0, 0)))
    y = xp[:, 0:t] * w[0]
    for j in range(1, k_w):
        y = y + xp[:, j:j + t] * w[j]
    return y


def _block_diag(x, w):
    b, t, _ = x.shape
    nb, bw, _ = w.shape
    return jnp.einsum('btnj,njk->btnk', x.reshape(b, t, nb, bw), w).reshape(b, t, nb * bw)


def _lin_combine(left, right):
    a1, b1 = left
    a2, b2 = right
    return a1 * a2, a2 * b1 + b2


def _lru_scan(a, b, h0):
    a_cum, h = lax.associative_scan(_lin_combine, (a, b), axis=1)
    h = h + a_cum * h0[:, None]
    return h, h[:, -1]


def _gated_delta_chunked(q, k, v, g, beta, h0):
    b, h, t, dk = q.shape
    dv = v.shape[-1]
    c = DN_CHUNK
    n = t // c

    def chunks(a):
        return a.reshape(a.shape[:2] + (n, c) + a.shape[3:])

    q = chunks(q * dk ** -0.5)
    k = chunks(k)
    v = chunks(v)
    beta = chunks(beta)
    g = jnp.cumsum(chunks(g), axis=-1)
    idx = jnp.arange(c)
    incl = idx[:, None] >= idx[None, :]
    strict = idx[:, None] > idx[None, :]
    decay = jnp.exp(jnp.where(incl, g[..., :, None] - g[..., None, :], -jnp.inf))
    kb = k * beta[..., None]
    lower = jnp.where(strict, jnp.einsum('bhncd,bhnsd->bhncs', kb, k) * decay, 0.0)
    eye = jnp.eye(c, dtype=lower.dtype)
    rhs = jnp.concatenate([v * beta[..., None], kb * jnp.exp(g)[..., None]], axis=-1)
    sol = lax.linalg.triangular_solve(lower + eye, rhs, left_side=True, lower=True, unit_diagonal=True)
    u, w = sol[..., :dv], sol[..., dv:]
    qk = jnp.where(incl, jnp.einsum('bhncd,bhnsd->bhncs', q, k) * decay, 0.0)
    g_last = g[..., -1]
    k_dec = k * jnp.exp(g_last[..., None] - g)[..., None]
    q_dec = q * jnp.exp(g)[..., None]

    def step(s, inp):
        q_c, k_c, u_c, w_c, qk_c, gl_c = inp
        v_new = u_c - jnp.einsum('bhcd,bhde->bhce', w_c, s)
        o = jnp.einsum('bhcd,bhde->bhce', q_c, s) + jnp.einsum('bhcs,bhse->bhce', qk_c, v_new)
        s = s * jnp.exp(gl_c)[..., None, None] + jnp.einsum('bhcd,bhce->bhde', k_c, v_new)
        return s, o

    xs = tuple(jnp.moveaxis(a, 2, 0) for a in (q_dec, k_dec, u, w, qk, g_last))
    s_final, o = lax.scan(step, h0, xs)
    return jnp.moveaxis(o, 0, 2).reshape(b, h, t, dv), s_final


def _bidirectional(scan, ctx_fwd, lat_fwd, ctx_bwd, lat_bwd, h0, t_axis, need_ctx):
    def flip(args):
        return tuple(jnp.flip(a, t_axis) for a in args)

    yc_f, s_f = scan(*ctx_fwd, h0)
    yl_f, _ = scan(*lat_fwd, s_f)
    yc_b, s_b = scan(*flip(ctx_bwd), h0)
    yl_b, _ = scan(*flip(lat_bwd), s_b)
    y_lat = yl_f + jnp.flip(yl_b, t_axis)
    y_ctx = yc_f + jnp.flip(yc_b, t_axis) if need_ctx else None
    return y_lat, y_ctx


def _gqa_mixer(p, pc, qn, kn, rows, cols, need_ctx):
    uq, uk, uv, z = p
    ucq, uck, ucv, zc = pc
    b, s, _ = uq.shape
    grp = A_HEADS // A_KV_HEADS
    scale = A_HEAD_DIM ** -0.5
    q = _rope_2d(_rms_norm(_split_heads(uq, A_HEADS), qn), rows, cols)
    k = _rope_2d(_rms_norm(_split_heads(uk, A_KV_HEADS), kn), rows, cols)
    v = _split_heads(uv, A_KV_HEADS)
    kc = _rms_norm(_split_heads(uck, A_KV_HEADS), kn)
    vc = _split_heads(ucv, A_KV_HEADS)
    o = _blocked_attention(q.reshape(b, A_KV_HEADS, grp, s, A_HEAD_DIM),
                           jnp.concatenate([kc, k], axis=2), jnp.concatenate([vc, v], axis=2), scale)
    y = _merge_heads(o.reshape(b, A_HEADS, s, A_HEAD_DIM)) * jax.nn.silu(z)
    if not need_ctx:
        return y, None
    qc = _rms_norm(_split_heads(ucq, A_HEADS), qn)
    oc = _attend(qc.reshape(b, A_KV_HEADS, grp, -1, A_HEAD_DIM), kc, vc, scale)
    yc = _merge_heads(oc.reshape(b, A_HEADS, -1, A_HEAD_DIM)) * jax.nn.silu(zc)
    return y, yc


def _rglru_mixer(p, pc, conv_w, conv_b, w_a, b_a, w_x, b_x, lam, need_ctx):
    ux, z = p
    ucx, zc = pc
    xl = (_dwconv_centred(ux, conv_w) + conv_b).astype(jnp.float32)
    xc = (_dwconv_centred(ucx, conv_w) + conv_b).astype(jnp.float32)

    def gates(xs, d):
        r = jax.nn.sigmoid(_block_diag(xs, w_a[d]) + b_a[d])
        i = jax.nn.sigmoid(_block_diag(xs, w_x[d]) + b_x[d])
        log_a = -LRU_C * r * jax.nn.softplus(-lam[d])
        return jnp.exp(log_a), jnp.sqrt(-jnp.expm1(2.0 * log_a)) * (i * xs)

    h0 = jnp.zeros((ux.shape[0], LRU_WIDTH), jnp.float32)
    hl, hc = _bidirectional(_lru_scan, gates(xc, 0), gates(xl, 0), gates(xc, 1), gates(xl, 1),
                            h0, 1, need_ctx)
    y = hl.astype(ux.dtype) * jax.nn.silu(z)
    yc = hc.astype(ux.dtype) * jax.nn.silu(zc) if need_ctx else None
    return y, yc


def _deltanet_mixer(p, pc, conv_w, a_log, dt_bias, norm_w, need_ctx):
    def prep(uqkv, uab):
        b, t, _ = uqkv.shape
        qkv = jax.nn.silu(_dwconv_centred(uqkv, conv_w)).astype(jnp.float32)
        q, k, v = jnp.split(qkv, 3, axis=-1)
        q = _l2_norm(_split_heads(q, DN_HEADS))
        k = _l2_norm(_split_heads(k, DN_HEADS))
        v = _split_heads(v, DN_HEADS)
        ab = uab.astype(jnp.float32).reshape(b, t, 2, 2, DN_HEADS)
        beta = jax.nn.sigmoid(ab[..., 0, :]).transpose(2, 0, 3, 1)
        g = (-jnp.exp(a_log.astype(jnp.float32))[:, None, :, None]
             * jax.nn.softplus(ab[..., 1, :] + dt_bias).transpose(2, 0, 3, 1))
        return q, k, v, g, beta

    uqkv, z, uab = p
    ucqkv, zc, ucab = pc
    q, k, v, g, beta = prep(uqkv, uab)
    qc, kc, vc, gc, betac = prep(ucqkv, ucab)
    h0 = jnp.zeros((uqkv.shape[0], DN_HEADS, DN_HEAD_DIM, DN_HEAD_DIM), jnp.float32)
    ol, oc = _bidirectional(_gated_delta_chunked,
                            (qc, kc, vc, gc[0], betac[0]), (q, k, v, g[0], beta[0]),
                            (qc, kc, vc, gc[1], betac[1]), (q, k, v, g[1], beta[1]),
                            h0, 2, need_ctx)

    def out(o, zz):
        return _merge_heads(_rms_norm(o, norm_w).astype(zz.dtype) * jax.nn.silu(_split_heads(zz, DN_HEADS)))

    return out(ol, z), (out(oc, zc) if need_ctx else None)


def _mla_mixer(p, pc, q_norm_w, kv_norm_w, w_uq, w_ukv, qn, kn, rows, cols, need_ctx):
    def project_q(ucq):
        return _rms_norm(_split_heads(_rms_norm(ucq, q_norm_w) @ w_uq, MLA_HEADS), qn)

    def project_kv(uckv, ukr):
        kv = _split_heads(_rms_norm(uckv, kv_norm_w) @ w_ukv, MLA_HEADS)
        k_nope, v = kv[..., :MLA_NOPE], kv[..., MLA_NOPE:]
        k_rope = jnp.broadcast_to(ukr[:, None], k_nope.shape[:-1] + (MLA_ROPE,))
        k = _rms_norm(jnp.concatenate([k_nope, k_rope], axis=-1), kn)
        return k, v

    def rotate(t):
        return jnp.concatenate([t[..., :MLA_NOPE], _rope_2d(t[..., MLA_NOPE:], rows, cols)], axis=-1)

    ucq, uckv, ukr, z = p
    ccq, cckv, ckr, zc = pc
    scale = (MLA_NOPE + MLA_ROPE) ** -0.5
    q = rotate(project_q(ucq))
    k, v = project_kv(uckv, ukr)
    k = rotate(k)
    kc, vc = project_kv(cckv, ckr)
    o = _blocked_attention(q[:, :, None], jnp.concatenate([kc, k], axis=2),
                           jnp.concatenate([vc, v], axis=2), scale)[:, :, 0]
    y = _merge_heads(o) * jax.nn.silu(z)
    if not need_ctx:
        return y, None
    oc = _attend(project_q(ccq)[:, :, None], kc, vc, scale)[:, :, 0]
    return y, _merge_heads(oc) * jax.nn.silu(zc)


def _layer(x, ctx, c, c_ctx, rows, cols, need_ctx,
           norm_w, w_ada, b_ada, w_in, w_out, attn_q_norm, attn_k_norm,
           lru_conv_w, lru_conv_b, lru_w_a, lru_b_a, lru_w_x, lru_b_x, lru_lambda,
           dn_conv_w, dn_a_log, dn_dt_bias, dn_norm_w,
           mla_q_norm, mla_kv_norm, mla_w_uq, mla_w_ukv, mla_q_qk_norm, mla_k_qk_norm):
    shift, scale, gate = jnp.split(jax.nn.silu(c) @ w_ada + b_ada, 3, axis=-1)
    shift_c, scale_c, gate_c = jnp.split(jax.nn.silu(c_ctx) @ w_ada + b_ada, 3, axis=-1)
    h = _rms_norm(x, norm_w) * (1.0 + scale[:, None]) + shift[:, None]
    hc = _rms_norm(ctx, norm_w) * (1.0 + scale_c) + shift_c
    offsets = np.cumsum(IN_SPLITS)[:-1].tolist()
    p = jnp.split(h @ w_in, offsets, axis=-1)
    pc = jnp.split(hc @ w_in, offsets, axis=-1)
    y_a, yc_a = _gqa_mixer(p[0:4], pc[0:4], attn_q_norm, attn_k_norm, rows, cols, need_ctx)
    y_b, yc_b = _rglru_mixer(p[4:6], pc[4:6], lru_conv_w, lru_conv_b, lru_w_a, lru_b_a,
                             lru_w_x, lru_b_x, lru_lambda, need_ctx)
    y_c, yc_c = _deltanet_mixer(p[6:9], pc[6:9], dn_conv_w, dn_a_log, dn_dt_bias, dn_norm_w, need_ctx)
    y_d, yc_d = _mla_mixer(p[9:13], pc[9:13], mla_q_norm, mla_kv_norm, mla_w_uq, mla_w_ukv,
                           mla_q_qk_norm, mla_k_qk_norm, rows, cols, need_ctx)
    x = x + gate[:, None] * (jnp.concatenate([y_a, y_b, y_c, y_d], axis=-1) @ w_out)
    if need_ctx:
        ctx = ctx + gate_c * (jnp.concatenate([yc_a, yc_b, yc_c, yc_d], axis=-1) @ w_out)
    return x, ctx


def setup_inputs(seed: int = 0) -> dict:
    key = jax.random.key(seed)
    ks = iter(jax.random.split(key, 40))
    f32 = jnp.float32
    L = DEPTH

    def nrm(shape, scale):
        return scale * jax.random.normal(next(ks), shape, f32)

    def gain(shape):
        return 1.0 + nrm(shape, 0.01)

    u_lam = jax.random.uniform(next(ks), (L, 2, LRU_WIDTH), f32, 0.9, 0.999)
    dt = jnp.exp(jax.random.uniform(next(ks), (L, 2, DN_HEADS), f32, math.log(1e-3), math.log(1e-1)))
    bw = LRU_WIDTH // LRU_BLOCKS
    return {
        'x': nrm((BATCH, SEQ, D_MODEL), 1.0),
        'c': nrm((BATCH, D_MODEL), 1.0),
        'ctx': nrm((BATCH, CTX_LEN, D_MODEL), 1.0),
        'c_ctx': nrm((D_MODEL,), 1.0),
        'norm_w': gain((L, D_MODEL)),
        'w_ada': nrm((L, D_MODEL, 3 * D_MODEL), 0.5 * D_MODEL ** -0.5),
        'b_ada': nrm((L, 3 * D_MODEL), 0.01),
        'w_in': nrm((L, D_MODEL, IN_WIDTH), D_MODEL ** -0.5),
        'w_out': nrm((L, MIX_WIDTH, D_MODEL), MIX_WIDTH ** -0.5),
        'attn_q_norm': gain((L, A_HEAD_DIM)),
        'attn_k_norm': gain((L, A_HEAD_DIM)),
        'lru_conv_w': nrm((L, LRU_CONV, LRU_WIDTH), LRU_CONV ** -0.5),
        'lru_conv_b': nrm((L, LRU_WIDTH), 0.01),
        'lru_w_a': nrm((L, 2, LRU_BLOCKS, bw, bw), bw ** -0.5),
        'lru_b_a': nrm((L, 2, LRU_WIDTH), 0.01),
        'lru_w_x': nrm((L, 2, LRU_BLOCKS, bw, bw), bw ** -0.5),
        'lru_b_x': nrm((L, 2, LRU_WIDTH), 0.01),
        'lru_lambda': jnp.log(u_lam) - jnp.log1p(-u_lam),
        'dn_conv_w': nrm((L, DN_CONV, 3 * DN_WIDTH), DN_CONV ** -0.5),
        'dn_a_log': jnp.log(jax.random.uniform(next(ks), (L, 2, DN_HEADS), f32, 1.0, 16.0)),
        'dn_dt_bias': dt + jnp.log(-jnp.expm1(-dt)),
        'dn_norm_w': gain((L, DN_HEAD_DIM)),
        'mla_q_norm': gain((L, MLA_Q_RANK)),
        'mla_kv_norm': gain((L, MLA_KV_RANK)),
        'mla_w_uq': nrm((L, MLA_Q_RANK, MLA_HEADS * (MLA_NOPE + MLA_ROPE)), MLA_Q_RANK ** -0.5),
        'mla_w_ukv': nrm((L, MLA_KV_RANK, MLA_HEADS * (MLA_NOPE + MLA_V)), MLA_KV_RANK ** -0.5),
        'mla_q_qk_norm': gain((L, MLA_NOPE + MLA_ROPE)),
        'mla_k_qk_norm': gain((L, MLA_NOPE + MLA_ROPE)),
    }


def reference(x, c, ctx, c_ctx, norm_w, w_ada, b_ada, w_in, w_out, attn_q_norm, attn_k_norm,
              lru_conv_w, lru_conv_b, lru_w_a, lru_b_a, lru_w_x, lru_b_x, lru_lambda,
              dn_conv_w, dn_a_log, dn_dt_bias, dn_norm_w,
              mla_q_norm, mla_kv_norm, mla_w_uq, mla_w_ukv, mla_q_qk_norm, mla_k_qk_norm):
    n_tok = x.shape[1]
    ROWS = n_tok // GRID_W
    rows = jnp.repeat(jnp.arange(ROWS), GRID_W)
    cols = jnp.tile(jnp.arange(GRID_W), ROWS)
    for l in range(DEPTH):
        x, ctx = _layer(x, ctx, c, c_ctx, rows, cols, l < DEPTH - 1,
                        norm_w[l], w_ada[l], b_ada[l], w_in[l], w_out[l], attn_q_norm[l], attn_k_norm[l],
                        lru_conv_w[l], lru_conv_b[l], lru_w_a[l], lru_b_a[l], lru_w_x[l], lru_b_x[l],
                        lru_lambda[l], dn_conv_w[l], dn_a_log[l], dn_dt_bias[l], dn_norm_w[l],
                        mla_q_norm[l], mla_kv_norm[l], mla_w_uq[l], mla_w_ukv[l],
                        mla_q_qk_norm[l], mla_k_qk_norm[l])
    return x
```

```python
import functools
import math

import jax
import jax.numpy as jnp
from jax import lax
from jax.experimental import pallas as pl
from jax.experimental.pallas import tpu as pltpu

F32 = jnp.float32
BF16 = jnp.bfloat16

D_MODEL = 2048
DEPTH = 4
GRID_W = 64
ROPE_THETA = 10000.0
NORM_EPS = 1e-6
HEAD = 128
A_HEADS, A_KV_HEADS = 4, 2
LRU_WIDTH, LRU_BLOCKS, LRU_C = 512, 4, 8.0
DN_HEADS, DN_CHUNK = 4, 64
MLA_HEADS, MLA_Q_RANK, MLA_KV_RANK, MLA_NOPE, MLA_ROPE = 4, 384, 256, 128, 64
MLA_QK = MLA_NOPE + MLA_ROPE
MIX = 512

P_AQ, P_AK, P_AV, P_AZ = 0, 512, 768, 1024
P_BX, P_BZ = 1536, 2048
P_CQ, P_CK, P_CV, P_CZ = 2560, 3072, 3584, 4096
P_DZ, P_DCKV, P_DCQ = 4608, 5120, 5376
P_SM = 5760
P_W = 6144
SM_AB = MLA_ROPE

TOK = 256
VMEM_LIMIT = 56 * 1024 * 1024

NN = ((1,), (0,))
NT = ((1,), (1,))
TN = ((0,), (0,))


def _dg(a, b, dims=NN):
    return lax.dot_general(a, b, (dims, ((), ())), preferred_element_type=F32)


def _split2(a):
    hi = a.astype(BF16)
    lo = (a - hi.astype(F32)).astype(BF16)
    return hi, lo


def _dot3(a, b, dims=NN):
    ah, al = _split2(a)
    bh, bl = _split2(b)
    return _dg(ah, bh, dims) + (_dg(ah, bl, dims) + _dg(al, bh, dims))


def _dot_exact_lhs(m, b):
    b0 = b.astype(BF16)
    r1 = b - b0.astype(F32)
    b1 = r1.astype(BF16)
    b2 = (r1 - b1.astype(F32)).astype(BF16)
    return _dg(m, b0) + (_dg(m, b1) + _dg(m, b2))


def _sigmoid(x):
    return 1.0 / (1.0 + jnp.exp(-x))


def _silu(x):
    return x * _sigmoid(x)


def _softplus(x):
    return jnp.maximum(x, 0.0) + jnp.log(1.0 + jnp.exp(-jnp.abs(x)))


def _rms(x, w, n):
    return x * lax.rsqrt(jnp.sum(x * x, axis=-1, keepdims=True) * (1.0 / n) + NORM_EPS) * w


def _rope(x, cos, sin_signed, half):
    lane = lax.broadcasted_iota(jnp.int32, x.shape, 1)
    first = (lane % (2 * half)) < half
    rot = jnp.where(first, pltpu.roll(x, HEAD - half, 1), pltpu.roll(x, half, 1))
    return x * cos + rot * sin_signed


def _params(*sem):
    return pltpu.CompilerParams(dimension_semantics=sem, vmem_limit_bytes=VMEM_LIMIT)


def _row_tile(rows):
    for t in (768, 512, 256):
        if rows % t == 0:
            return t
    raise ValueError(f"row count {rows} is not a multiple of {TOK}")


def _ada_kernel(c_ref, w_ref, b_ref, o_ref):
    s = _silu(c_ref[...])
    w = w_ref[0]
    r0 = jnp.sum(s[:, 0:1] * w, axis=0, keepdims=True)
    r1 = jnp.sum(s[:, 1:2] * w, axis=0, keepdims=True)
    o_ref[0, 0] = jnp.concatenate([r0, r1], axis=0) + b_ref[0]


def _ada(c_cols, w_ada, b_ada):
    depth, d, _ = w_ada.shape
    tn = 512
    per = d // tn
    return pl.pallas_call(
        _ada_kernel,
        grid=(depth, 3 * per),
        in_specs=[
            pl.BlockSpec((d, 2), lambda l, j: (0, 0)),
            pl.BlockSpec((1, d, tn), lambda l, j: (l, 0, j)),
            pl.BlockSpec((1, 1, tn), lambda l, j: (l, 0, j)),
        ],
        out_specs=pl.BlockSpec((1, 1, 2, tn), lambda l, j: (l, j // per, 0, j % per)),
        out_shape=jax.ShapeDtypeStruct((depth, 3, 2, d), F32),
        compiler_params=_params("parallel", "parallel"),
        name="ada",
    )(c_cols, w_ada, b_ada.reshape(depth, 1, 3 * d))


def _inproj_kernel(x_ref, nw_ref, shift_ref, scale_ref, w_ref, o_ref, h_ref, *, lat):
    i = pl.program_id(0)
    tm = x_ref.shape[0]

    @pl.when(pl.program_id(1) == 0)
    def _():
        y = _rms(x_ref[...], nw_ref[...], D_MODEL)
        row = i * tm + lax.broadcasted_iota(jnp.int32, (tm, 1), 0)
        is_ctx = row >= lat
        scale = jnp.where(is_ctx, scale_ref[1:2, :], scale_ref[0:1, :])
        shift = jnp.where(is_ctx, shift_ref[1:2, :], shift_ref[0:1, :])
        h_ref[...] = (y * (1.0 + scale) + shift).astype(BF16)

    o_ref[...] = _dg(h_ref[...], w_ref[...])


def _inproj(xs, norm_w, shift, scale, w_in, lat):
    t, d = xs.shape
    tm, tn = _row_tile(t), 768
    return pl.pallas_call(
        functools.partial(_inproj_kernel, lat=lat),
        grid=(t // tm, P_W // tn),
        in_specs=[
            pl.BlockSpec((tm, d), lambda i, j: (i, 0)),
            pl.BlockSpec((1, d), lambda i, j: (0, 0)),
            pl.BlockSpec((2, d), lambda i, j: (0, 0)),
            pl.BlockSpec((2, d), lambda i, j: (0, 0)),
            pl.BlockSpec((d, tn), lambda i, j: (0, j)),
        ],
        out_specs=pl.BlockSpec((tm, tn), lambda i, j: (i, j)),
        out_shape=jax.ShapeDtypeStruct((t, P_W), F32),
        scratch_shapes=[pltpu.VMEM((tm, d), BF16)],
        compiler_params=_params("parallel", "arbitrary"),
        name="inproj",
    )(xs, norm_w.reshape(1, d), shift, scale, w_in)


def _outproj_kernel(ya_ref, yb_ref, yc_ref, yd_ref, w_ref, x_ref, g_ref, o_ref, *, lat):
    i = pl.program_id(0)
    tm = x_ref.shape[0]
    acc = _dg(ya_ref[...], w_ref[0])
    acc += _dg(yb_ref[...], w_ref[1])
    acc += _dg(yc_ref[...], w_ref[2])
    acc += _dg(yd_ref[...], w_ref[3])
    row = i * tm + lax.broadcasted_iota(jnp.int32, (tm, 1), 0)
    gate = jnp.where(row >= lat, g_ref[1:2, :], g_ref[0:1, :])
    o_ref[...] = x_ref[...] + gate * acc


def _outproj(ys, w_out, xs, gate, lat, out_rows):
    d = xs.shape[1]
    tm, tn = _row_tile(out_rows), 1024
    yspec = pl.BlockSpec((tm, MIX), lambda i, j: (i, 0))
    return pl.pallas_call(
        functools.partial(_outproj_kernel, lat=lat),
        grid=(out_rows // tm, d // tn),
        in_specs=[yspec, yspec, yspec, yspec,
                  pl.BlockSpec((4, MIX, tn), lambda i, j: (0, 0, j)),
                  pl.BlockSpec((tm, tn), lambda i, j: (i, j)),
                  pl.BlockSpec((2, tn), lambda i, j: (0, j))],
        out_specs=pl.BlockSpec((tm, tn), lambda i, j: (i, j)),
        out_shape=jax.ShapeDtypeStruct((out_rows, d), F32),
        compiler_params=_params("parallel", "parallel"),
        name="outproj",
    )(*ys, w_out, xs, gate)


def _flash_kernel(q_ref, k_ref, v_ref, z_ref, o_ref, *, lat, tk):
    tq = q_ref.shape[0]
    ctx = k_ref.shape[0] - lat
    q = q_ref[...]
    is_ctx_tile = pl.program_id(1) == pl.num_programs(1) - 1

    def attend(carry, k, v):
        m, l, acc = carry
        s = _dg(q, k, NT)
        m_new = jnp.maximum(m, jnp.max(s, axis=-1, keepdims=True))
        alpha = jnp.exp(m - m_new)
        p = jnp.exp(s - m_new)
        l = alpha * l + jnp.sum(p, axis=-1, keepdims=True)
        acc = alpha * acc + _dg(p.astype(BF16), v)
        return m_new, l, acc

    def lat_step(kb, carry):
        off = pl.multiple_of(kb * tk, tk)
        return attend(carry, k_ref[pl.ds(off, tk), :], v_ref[pl.ds(off, tk), :])

    init = (jnp.full((tq, 1), -1e30, F32), jnp.zeros((tq, 1), F32), jnp.zeros((tq, v_ref.shape[1]), F32))
    n_lat = jnp.where(is_ctx_tile, 0, lat // tk)
    carry = lax.fori_loop(0, n_lat, lat_step, init)
    _, l, acc = attend(carry, k_ref[pl.ds(lat, ctx), :], v_ref[pl.ds(lat, ctx), :])
    o_ref[...] = (acc / l * _silu(z_ref[...])).astype(o_ref.dtype)


def _flash(q, k, v, p, z_col, heads, kv_heads, dq, lat, name):
    t = q.shape[0]
    grp = heads // kv_heads
    tq = TOK
    tk = 512 if lat % 512 == 0 else TOK
    zb = z_col // HEAD
    return pl.pallas_call(
        functools.partial(_flash_kernel, lat=lat, tk=tk),
        grid=(heads, t // tq),
        in_specs=[
            pl.BlockSpec((tq, dq), lambda h, i: (i, h)),
            pl.BlockSpec((t, dq), lambda h, i: (0, h // grp)),
            pl.BlockSpec((t, HEAD), lambda h, i: (0, h // grp)),
            pl.BlockSpec((tq, HEAD), lambda h, i: (i, zb + h)),
        ],
        out_specs=pl.BlockSpec((tq, HEAD), lambda h, i: (i, h)),
        out_shape=jax.ShapeDtypeStruct((t, heads * HEAD), BF16),
        compiler_params=_params("parallel", "parallel"),
        name=name,
    )(q, k, v, p)


def _gqa_prep_kernel(q_ref, k_ref, v_ref, cos_ref, sin_ref, qn_ref, kn_ref, oq_ref, ok_ref, ov_ref):
    cos, sin = cos_ref[...], sin_ref[...]
    scale = HEAD ** -0.5
    for h in range(A_HEADS):
        c = slice(h * HEAD, (h + 1) * HEAD)
        oq_ref[:, c] = (_rope(_rms(q_ref[:, c], qn_ref[...], HEAD), cos, sin, 32) * scale).astype(BF16)
    for g in range(A_KV_HEADS):
        c = slice(g * HEAD, (g + 1) * HEAD)
        ok_ref[:, c] = _rope(_rms(k_ref[:, c], kn_ref[...], HEAD), cos, sin, 32).astype(BF16)
    ov_ref[...] = v_ref[...].astype(BF16)


def _gqa_prep(p, cos, sin, qn, kn):
    t = p.shape[0]
    tb = _row_tile(t)
    qw, kw = A_HEADS * HEAD, A_KV_HEADS * HEAD
    row = lambda w, c: pl.BlockSpec((tb, w), lambda i: (i, c))
    vec = pl.BlockSpec((1, HEAD), lambda i: (0, 0))
    return pl.pallas_call(
        _gqa_prep_kernel,
        grid=(t // tb,),
        in_specs=[row(qw, P_AQ // qw), row(kw, P_AK // kw), row(kw, P_AV // kw),
                  row(HEAD, 0), row(HEAD, 0), vec, vec],
        out_specs=[row(qw, 0), row(kw, 0), row(kw, 0)],
        out_shape=[jax.ShapeDtypeStruct((t, qw), BF16), jax.ShapeDtypeStruct((t, kw), BF16),
                   jax.ShapeDtypeStruct((t, kw), BF16)],
        compiler_params=_params("parallel"),
        name="gqa_prep",
    )(p, p, p, cos, sin, qn.reshape(1, HEAD), kn.reshape(1, HEAD))


def _mla_prep_kernel(cq_ref, ckv_ref, sm_ref, cos_ref, sin_ref, qnw_ref, kvnw_ref, wq_ref, wkv_ref,
                     qn_ref, kn_ref, oq_ref, ok_ref, ov_ref):
    cos, sin = cos_ref[...], sin_ref[...]
    scale = MLA_QK ** -0.5
    qn, kn = qn_ref[...], kn_ref[...]
    q = _dg(_rms(cq_ref[...], qnw_ref[...], MLA_Q_RANK).astype(BF16), wq_ref[...])
    kv = _dg(_rms(ckv_ref[...], kvnw_ref[...], MLA_KV_RANK).astype(BF16), wkv_ref[...])
    lane = lax.broadcasted_iota(jnp.int32, sm_ref.shape, 1)
    kr = jnp.where(lane < MLA_ROPE, sm_ref[...], 0.0)
    kr_ss = jnp.sum(kr * kr, axis=-1, keepdims=True)
    for h in range(MLA_HEADS):
        lo = slice(2 * h * HEAD, (2 * h + 1) * HEAD)
        hi = slice((2 * h + 1) * HEAD, (2 * h + 2) * HEAD)
        q_nope, q_rope = q[:, lo], q[:, hi]
        r = lax.rsqrt((jnp.sum(q_nope * q_nope, axis=-1, keepdims=True)
                       + jnp.sum(q_rope * q_rope, axis=-1, keepdims=True)) * (1.0 / MLA_QK) + NORM_EPS)
        oq_ref[:, lo] = (q_nope * r * qn[:, :HEAD] * scale).astype(BF16)
        oq_ref[:, hi] = (_rope(q_rope * r * qn[:, HEAD:], cos, sin, 16) * scale).astype(BF16)
        k_nope, v = kv[:, lo], kv[:, hi]
        r = lax.rsqrt((jnp.sum(k_nope * k_nope, axis=-1, keepdims=True) + kr_ss) * (1.0 / MLA_QK) + NORM_EPS)
        ok_ref[:, lo] = (k_nope * r * kn[:, :HEAD]).astype(BF16)
        ok_ref[:, hi] = _rope(kr * r * kn[:, HEAD:], cos, sin, 16).astype(BF16)
        ov_ref[:, h * HEAD:(h + 1) * HEAD] = v.astype(BF16)


def _mla_prep(p, cos, sin, q_norm_w, kv_norm_w, wq, wkv, qn, kn):
    t = p.shape[0]
    tb = _row_tile(t)
    w2 = MLA_HEADS * 2 * HEAD
    row = lambda w, c: pl.BlockSpec((tb, w), lambda i: (i, c))
    full = lambda a: pl.BlockSpec(a.shape, lambda i: (0, 0))
    args = (q_norm_w.reshape(1, -1), kv_norm_w.reshape(1, -1), wq, wkv, qn, kn)
    return pl.pallas_call(
        _mla_prep_kernel,
        grid=(t // tb,),
        in_specs=[row(MLA_Q_RANK, P_DCQ // MLA_Q_RANK), row(MLA_KV_RANK, P_DCKV // MLA_KV_RANK),
                  row(HEAD, P_SM // HEAD), row(HEAD, 0), row(HEAD, 0)] + [full(a) for a in args],
        out_specs=[row(w2, 0), row(w2, 0), row(MLA_HEADS * HEAD, 0)],
        out_shape=[jax.ShapeDtypeStruct((t, w2), BF16), jax.ShapeDtypeStruct((t, w2), BF16),
                   jax.ShapeDtypeStruct((t, MLA_HEADS * HEAD), BF16)],
        compiler_params=_params("parallel"),
        name="mla_prep",
    )(p, p, p, cos, sin, *args)


def _lru_kernel(ux_ref, z_ref, cw_ref, cb_ref, wa_ref, ba_ref, wx_ref, bx_ref, lam_ref, o_ref,
                xs_ref, a_ref, hf_ref, hb_ref, *, lat):
    t = ux_ref.shape[0]
    ctx = t - lat
    nb, nb_lat = t // TOK, lat // TOK
    cw, cb = cw_ref[...], cb_ref[...]

    def conv_block(b, _):
        t0 = pl.multiple_of(b * TOK, TOK)
        first = jnp.logical_or(b == 0, b == nb_lat)
        last = jnp.logical_or(b == nb_lat - 1, b == nb - 1)
        prev = ux_ref[pl.ds(pl.multiple_of(jnp.maximum(t0 - 8, 0), 8), 8), :]
        nxt = ux_ref[pl.ds(pl.multiple_of(jnp.minimum(t0 + TOK, t - 8), 8), 8), :]
        xe = jnp.concatenate([jnp.where(first, 0.0, prev), ux_ref[pl.ds(t0, TOK), :],
                              jnp.where(last, 0.0, nxt)], axis=0)
        y = cb + xe[6:6 + TOK] * cw[0:1]
        for j in range(1, 4):
            y = y + xe[6 + j:6 + j + TOK] * cw[j:j + 1]
        xs_ref[pl.ds(t0, TOK), :] = y
        return 0

    lax.fori_loop(0, nb, conv_block, 0)

    def gates(d, h_ref):
        wa, wx = wa_ref[d, 0], wx_ref[d, 0]
        ba, bx = ba_ref[d:d + 1, :], bx_ref[d:d + 1, :]
        sp = _softplus(-lam_ref[d:d + 1, :])

        def block(b, _):
            rows = pl.ds(pl.multiple_of(b * TOK, TOK), TOK)
            xs = xs_ref[rows, :]
            r = _sigmoid(_dot3(xs, wa) + ba)
            gi = _sigmoid(_dot3(xs, wx) + bx)
            log_a = -LRU_C * r * sp
            a_ref[rows, :] = jnp.exp(log_a)
            h_ref[rows, :] = jnp.sqrt(1.0 - jnp.exp(2.0 * log_a)) * (gi * xs)
            return 0

        lax.fori_loop(0, nb, block, 0)

    def scan_region(h_ref, r0, length, h0, reverse):
        ls = length // 8

        def step(s, carry):
            h, ac = carry
            rows = pl.ds(r0 + (ls - 1 - s if reverse else s), 8, stride=ls)
            a = a_ref[rows, :]
            h = a * h + h_ref[rows, :]
            ac = ac * a
            h_ref[rows, :] = h
            a_ref[rows, :] = ac
            return h, ac

        hk, ak = lax.fori_loop(0, ls, step, (jnp.zeros((8, HEAD), F32), jnp.ones((8, HEAD), F32)))
        carry_in = [None] * 8
        c = h0
        for k in (range(7, -1, -1) if reverse else range(8)):
            carry_in[k] = c
            c = ak[k:k + 1] * c + hk[k:k + 1]
        blk = min(ls, TOK)
        for k in range(8):
            def fix(b, _, k=k):
                rows = pl.ds(pl.multiple_of(r0 + k * ls + b * blk, 8), blk)
                h_ref[rows, :] = h_ref[rows, :] + a_ref[rows, :] * carry_in[k]
                return 0
            lax.fori_loop(0, ls // blk, fix, 0)
        return c

    zero = jnp.zeros((1, HEAD), F32)
    for d, h_ref in ((0, hf_ref), (1, hb_ref)):
        gates(d, h_ref)
        s_ctx = scan_region(h_ref, lat, ctx, zero, reverse=(d == 1))
        scan_region(h_ref, 0, lat, s_ctx, reverse=(d == 1))

    def out_block(b, _):
        rows = pl.ds(pl.multiple_of(b * TOK, TOK), TOK)
        o_ref[rows, :] = ((hf_ref[rows, :] + hb_ref[rows, :]) * _silu(z_ref[rows, :])).astype(BF16)
        return 0

    lax.fori_loop(0, nb, out_block, 0)


def _lru(p, conv_w, conv_b, w_a, b_a, w_x, b_x, lam, lat):
    t = p.shape[0]
    col = lambda c0: pl.BlockSpec((t, HEAD), lambda n: (0, c0 // HEAD + n))
    vec = lambda r: pl.BlockSpec((r, HEAD), lambda n: (0, n))
    mat = pl.BlockSpec((2, 1, HEAD, HEAD), lambda n: (0, n, 0, 0))
    return pl.pallas_call(
        functools.partial(_lru_kernel, lat=lat),
        grid=(LRU_BLOCKS,),
        in_specs=[col(P_BX), col(P_BZ), vec(4), vec(1), mat, vec(2), mat, vec(2), vec(2)],
        out_specs=pl.BlockSpec((t, HEAD), lambda n: (0, n)),
        out_shape=jax.ShapeDtypeStruct((t, LRU_WIDTH), BF16),
        scratch_shapes=[pltpu.VMEM((t, HEAD), F32)] * 4,
        compiler_params=_params("parallel"),
        name="lru",
    )(p, p, conv_w, conv_b.reshape(1, -1), w_a, b_a, w_x, b_x, lam)


def _tri_inv(lm, eye):
    x = eye - lm
    pw = _dot3(lm, lm)
    for _ in range(4):
        x = x + _dot3(x, pw)
        pw = _dot3(pw, pw)
    return x + _dot3(x, pw)


def _dn_prep_kernel(q_ref, k_ref, v_ref, qp_ref, kp_ref, vp_ref, qx_ref, kx_ref, vx_ref, cw_ref, sm_ref,
                    alog_ref, dtb_ref, u_ref, w_ref, qd_ref, kd_ref, qk_ref, gl_ref, *, nb_lat):
    i = pl.program_id(0)
    tb = q_ref.shape[0]
    cc = DN_CHUNK
    first = jnp.logical_or(i == 0, i == nb_lat)
    last = jnp.logical_or(i == nb_lat - 1, i == nb_lat)
    cw = cw_ref[...]

    def conv_silu(x_ref, prev_ref, next_ref, c0):
        xe = jnp.concatenate([jnp.where(first, 0.0, prev_ref[...]), x_ref[...],
                              jnp.where(last, 0.0, next_ref[...])], axis=0)
        y = xe[6:6 + tb] * cw[0:1, c0:c0 + MIX]
        for j in range(1, 4):
            y = y + xe[6 + j:6 + j + tb] * cw[j:j + 1, c0:c0 + MIX]
        return _silu(y)

    q = conv_silu(q_ref, qp_ref, qx_ref, 0)
    k = conv_silu(k_ref, kp_ref, kx_ref, MIX)
    v = conv_silu(v_ref, vp_ref, vx_ref, 2 * MIX)

    sm = sm_ref[...]
    beta_all = _sigmoid(sm)
    g_all = -jnp.exp(alog_ref[...]) * _softplus(sm + dtb_ref[...])

    r = lax.broadcasted_iota(jnp.int32, (tb, tb), 0)
    c = lax.broadcasted_iota(jnp.int32, (tb, tb), 1)
    same = (r // cc) == (c // cc)
    tri_f = jnp.where(same, jnp.where(c <= r, 1.0, 0.0), 0.0).astype(BF16)
    tri_b = jnp.where(same, jnp.where(c >= r, 1.0, 0.0), 0.0).astype(BF16)
    gcs = (_dot_exact_lhs(tri_f, g_all), _dot_exact_lhs(tri_b, g_all))

    ii = lax.broadcasted_iota(jnp.int32, (cc, cc), 0)
    jj = lax.broadcasted_iota(jnp.int32, (cc, cc), 1)
    eye = jnp.where(ii == jj, 1.0, 0.0)
    incl = (ii >= jj, ii <= jj)
    strict = (ii > jj, ii < jj)

    gl_ref[...] = jnp.zeros_like(gl_ref)
    qs, ks = [], []
    for h in range(DN_HEADS):
        hs = slice(h * HEAD, (h + 1) * HEAD)
        qh, kh = q[:, hs], k[:, hs]
        qs.append(qh * lax.rsqrt(jnp.sum(qh * qh, axis=-1, keepdims=True) + NORM_EPS) * (HEAD ** -0.5))
        ks.append(kh * lax.rsqrt(jnp.sum(kh * kh, axis=-1, keepdims=True) + NORM_EPS))

    for ch in range(tb // cc):
        rows = slice(ch * cc, (ch + 1) * cc)
        for d in range(2):
            gc = gcs[d][rows]
            gct = gc.T
            end = cc - 1 if d == 0 else 0
            for h in range(DN_HEADS):
                hs = slice(h * HEAD, (h + 1) * HEAD)
                lb = SM_AB + d * 2 * DN_HEADS + h
                lg = lb + DN_HEADS
                beta = beta_all[rows, lb:lb + 1]
                gcol, grow = gc[:, lg:lg + 1], gct[lg:lg + 1, :]
                glast = gc[end:end + 1, lg:lg + 1]
                qh, kh, vh = qs[h][rows], ks[h][rows], v[rows, hs]
                decay = jnp.where(incl[d], jnp.exp(jnp.where(incl[d], gcol - grow, 0.0)), 0.0)
                kb = kh * beta
                lm = jnp.where(strict[d], _dot3(kb, kh, NT) * decay, 0.0)
                qk = _dot3(qh, kh, NT) * decay
                eg = jnp.exp(gcol)
                sol = _dot3(_tri_inv(lm, eye), jnp.concatenate([vh * beta, kb * eg], axis=1))
                u_ref[d, rows, hs] = sol[:, :HEAD]
                w_ref[d, rows, hs] = sol[:, HEAD:]
                qd_ref[d, rows, hs] = qh * eg
                kd_ref[d, rows, hs] = kh * jnp.exp(glast - gcol)
                qk_ref[d, h, rows, :] = qk
                gl_ref[d, ch, h:h + 1, :] = jnp.broadcast_to(jnp.exp(glast), (1, HEAD))


def _dn_prep(p, conv_w, alog_row, dtb_row, lat):
    t = p.shape[0]
    tb = TOK
    nb, nb_lat = t // tb, lat // tb
    r8 = tb // 8
    blk = lambda c0: pl.BlockSpec((tb, MIX), lambda i: (i, c0 // MIX))
    prev = lambda c0: pl.BlockSpec((8, MIX), lambda i: (jnp.maximum(i * r8 - 1, 0), c0 // MIX))
    nxt = lambda c0: pl.BlockSpec((8, MIX), lambda i: (jnp.minimum((i + 1) * r8, t // 8 - 1), c0 // MIX))
    cols = (P_CQ, P_CK, P_CV)
    big = pl.BlockSpec((2, tb, MIX), lambda i: (0, i, 0))
    big_shape = jax.ShapeDtypeStruct((2, t, MIX), F32)
    return pl.pallas_call(
        functools.partial(_dn_prep_kernel, nb_lat=nb_lat),
        grid=(nb,),
        in_specs=[blk(c0) for c0 in cols] + [prev(c0) for c0 in cols] + [nxt(c0) for c0 in cols] + [
            pl.BlockSpec((4, 3 * MIX), lambda i: (0, 0)),
            pl.BlockSpec((tb, HEAD), lambda i: (i, P_SM // HEAD)),
            pl.BlockSpec((1, HEAD), lambda i: (0, 0)),
            pl.BlockSpec((1, HEAD), lambda i: (0, 0))],
        out_specs=[big, big, big, big,
                   pl.BlockSpec((2, DN_HEADS, tb, DN_CHUNK), lambda i: (0, 0, i, 0)),
                   pl.BlockSpec((2, tb // DN_CHUNK, 8, HEAD), lambda i: (0, i, 0, 0))],
        out_shape=[big_shape, big_shape, big_shape, big_shape,
                   jax.ShapeDtypeStruct((2, DN_HEADS, t, DN_CHUNK), F32),
                   jax.ShapeDtypeStruct((2, t // DN_CHUNK, 8, HEAD), F32)],
        compiler_params=_params("parallel"),
        name="dn_prep",
    )(*([p] * 9), conv_w, p, alog_row, dtb_row)


def _dn_scan_kernel(uf, wf, qdf, kdf, qkf, glf, ub, wb, qdb, kdb, qkb, glb, of_ref, ob_ref, s_ref):
    @pl.when(pl.program_id(0) == 0)
    def _():
        s_ref[...] = jnp.zeros_like(s_ref)

    dirs = ((uf, wf, qdf, kdf, qkf, glf, of_ref), (ub, wb, qdb, kdb, qkb, glb, ob_ref))
    for d, (u_ref, w_ref, qd_ref, kd_ref, qk_ref, gl_ref, o_ref) in enumerate(dirs):
        for h in range(DN_HEADS):
            hs = slice(h * HEAD, (h + 1) * HEAD)
            s = s_ref[d, h]
            v_new = u_ref[0, :, hs] - _dot3(w_ref[0, :, hs], s)
            o_ref[:, hs] = _dot3(qd_ref[0, :, hs], s) + _dot3(qk_ref[0, h], v_new)
            s_ref[d, h] = s * gl_ref[0, 0, h:h + 1, :] + _dot3(kd_ref[0, :, hs], v_new, TN)


def _dn_scan(u, w, qd, kd, qk, gl, lat):
    t = u.shape[1]
    cc = DN_CHUNK
    n, n_lat = t // cc, lat // cc
    n_ctx = n - n_lat
    cf = lambda i: jnp.where(i < n_ctx, n_lat + i, i - n_ctx)
    cb = lambda i: n - 1 - i
    specs = []
    for d, ch in ((0, cf), (1, cb)):
        big = pl.BlockSpec((1, cc, MIX), lambda i, d=d, ch=ch: (d, ch(i), 0))
        specs += [big, big, big, big,
                  pl.BlockSpec((1, DN_HEADS, cc, cc), lambda i, d=d, ch=ch: (d, 0, ch(i), 0)),
                  pl.BlockSpec((1, 1, 8, HEAD), lambda i, d=d, ch=ch: (d, ch(i), 0, 0))]
    return pl.pallas_call(
        _dn_scan_kernel,
        grid=(n,),
        in_specs=specs,
        out_specs=[pl.BlockSpec((cc, MIX), lambda i: (cf(i), 0)), pl.BlockSpec((cc, MIX), lambda i: (cb(i), 0))],
        out_shape=[jax.ShapeDtypeStruct((t, MIX), F32)] * 2,
        scratch_shapes=[pltpu.VMEM((2, DN_HEADS, HEAD, HEAD), F32)],
        compiler_params=_params("arbitrary"),
        name="dn_scan",
    )(u, w, qd, kd, qk, gl, u, w, qd, kd, qk, gl)


def _dn_out_kernel(of_ref, ob_ref, z_ref, nw_ref, o_ref):
    for h in range(DN_HEADS):
        hs = slice(h * HEAD, (h + 1) * HEAD)
        o = of_ref[:, hs] + ob_ref[:, hs]
        o_ref[:, hs] = (_rms(o, nw_ref[...], HEAD) * _silu(z_ref[:, hs])).astype(BF16)


def _dn_out(o_f, o_b, p, norm_w):
    t = p.shape[0]
    tb = _row_tile(t)
    row = lambda c: pl.BlockSpec((tb, MIX), lambda i: (i, c))
    return pl.pallas_call(
        _dn_out_kernel,
        grid=(t // tb,),
        in_specs=[row(0), row(0), row(P_CZ // MIX), pl.BlockSpec((1, HEAD), lambda i: (0, 0))],
        out_specs=row(0),
        out_shape=jax.ShapeDtypeStruct((t, MIX), BF16),
        compiler_params=_params("parallel"),
        name="dn_out",
    )(o_f, o_b, p, norm_w.reshape(1, HEAD))


def _arrange_w_in(w_in):
    pad = jnp.zeros(w_in.shape[:2] + (P_W - w_in.shape[2],), w_in.dtype)
    parts = [w_in[..., 0:4608],
             w_in[..., 5328:5840],
             w_in[..., 5008:5264],
             w_in[..., 4624:5008],
             w_in[..., 5264:5328],
             w_in[..., 4608:4624],
             pad]
    return jnp.concatenate(parts, axis=-1).astype(BF16)


def _arrange_mla_wq(w_uq):
    l, r, _ = w_uq.shape
    w = w_uq.reshape(l, r, MLA_HEADS, MLA_QK)
    w = jnp.pad(w, ((0, 0), (0, 0), (0, 0), (0, 2 * HEAD - MLA_QK)))
    return w.reshape(l, r, MLA_HEADS * 2 * HEAD).astype(BF16)


def _pad_qk_norm(w):
    return jnp.pad(w, ((0, 0), (0, 2 * HEAD - MLA_QK)))[:, None, :]


def _small_lane_row(vals):
    l = vals.shape[0]
    row = jnp.zeros((l, HEAD), F32)
    for d in range(2):
        lo = SM_AB + d * 2 * DN_HEADS + DN_HEADS
        row = row.at[:, lo:lo + DN_HEADS].set(vals[:, d, :])
    return row[:, None, :]


def _rope_tables(lat, ctx):
    t = jnp.arange(lat)
    rows, cols = (t // GRID_W).astype(F32), (t % GRID_W).astype(F32)
    lane = jnp.arange(HEAD)

    def table(half, width):
        inv_freq = ROPE_THETA ** (-jnp.arange(half, dtype=F32) / half)
        pos = jnp.where((lane // (2 * half))[None, :] == 0, rows[:, None], cols[:, None])
        ang = pos * inv_freq[lane % half][None, :]
        live = (lane < width)[None, :]
        cos = jnp.where(live, jnp.cos(ang), 0.0)
        sin = jnp.where(live, jnp.sin(ang), 0.0) * jnp.where((lane % (2 * half)) < half, -1.0, 1.0)[None, :]
        cos_c = jnp.broadcast_to(jnp.where(live, 1.0, 0.0), (ctx, HEAD))
        return (jnp.concatenate([cos, cos_c], axis=0).astype(F32),
                jnp.concatenate([sin, jnp.zeros((ctx, HEAD), F32)], axis=0).astype(F32))

    return table(32, HEAD), table(16, MLA_ROPE)


def kernel(x, c, ctx, c_ctx, norm_w, w_ada, b_ada, w_in, w_out, attn_q_norm, attn_k_norm, lru_conv_w, lru_conv_b, lru_w_a, lru_b_a, lru_w_x, lru_b_x, lru_lambda, dn_conv_w, dn_a_log, dn_dt_bias, dn_norm_w, mla_q_norm, mla_kv_norm, mla_w_uq, mla_w_ukv, mla_q_qk_norm, mla_k_qk_norm):
    assert x.shape[0] == 1 and ctx.shape[1] == TOK and x.shape[1] % (8 * TOK) == 0
    lat, n_ctx = x.shape[1], ctx.shape[1]
    depth = w_in.shape[0]

    xs = jnp.concatenate([x[0], ctx[0]], axis=0)
    mod = _ada(jnp.stack([c[0], c_ctx], axis=1), w_ada, b_ada)
    (cos_a, sin_a), (cos_m, sin_m) = _rope_tables(lat, n_ctx)
    w_in_r = _arrange_w_in(w_in)
    w_out_r = w_out.reshape(depth, 4, MIX, D_MODEL).astype(BF16)
    wq_r = _arrange_mla_wq(mla_w_uq)
    wkv_r = mla_w_ukv.astype(BF16)
    qn_r, kn_r = _pad_qk_norm(mla_q_qk_norm), _pad_qk_norm(mla_k_qk_norm)
    alog_r, dtb_r = _small_lane_row(dn_a_log), _small_lane_row(dn_dt_bias)

    for l in range(depth):
        p = _inproj(xs, norm_w[l], mod[l, 0], mod[l, 1], w_in_r[l], lat)
        qa, ka, va = _gqa_prep(p, cos_a, sin_a, attn_q_norm[l], attn_k_norm[l])
        y_a = _flash(qa, ka, va, p, P_AZ, A_HEADS, A_KV_HEADS, HEAD, lat, "gqa_attn")
        y_b = _lru(p, lru_conv_w[l], lru_conv_b[l], lru_w_a[l], lru_b_a[l], lru_w_x[l], lru_b_x[l],
                   lru_lambda[l], lat)
        u, w, qd, kd, qk, gl = _dn_prep(p, dn_conv_w[l], alog_r[l], dtb_r[l], lat)
        o_f, o_b = _dn_scan(u, w, qd, kd, qk, gl, lat)
        y_c = _dn_out(o_f, o_b, p, dn_norm_w[l])
        qm, km, vm = _mla_prep(p, cos_m, sin_m, mla_q_norm[l], mla_kv_norm[l], wq_r[l], wkv_r[l],
                               qn_r[l], kn_r[l])
        y_d = _flash(qm, km, vm, p, P_DZ, MLA_HEADS, MLA_HEADS, 2 * HEAD, lat, "mla_attn")
        last = l == depth - 1
        xs = _outproj((y_a, y_b, y_c, y_d), w_out_r[l], xs, mod[l, 2], lat, lat if last else lat + n_ctx)
    return xs[None]
```

```python
import functools
import math

import jax
import jax.numpy as jnp
from jax import lax
from jax.experimental import pallas as pl
from jax.experimental.pallas import tpu as pltpu

F32 = jnp.float32
BF16 = jnp.bfloat16

D_MODEL = 2048
DEPTH = 4
GRID_W = 64
ROPE_THETA = 10000.0
NORM_EPS = 1e-6
HEAD = 128
A_HEADS, A_KV_HEADS = 4, 2
LRU_WIDTH, LRU_BLOCKS, LRU_C = 512, 4, 8.0
DN_HEADS, DN_CHUNK = 4, 64
MLA_HEADS, MLA_Q_RANK, MLA_KV_RANK, MLA_NOPE, MLA_ROPE = 4, 384, 256, 128, 64
MLA_QK = MLA_NOPE + MLA_ROPE
MIX = 512

P_AQ, P_AK, P_AV, P_AZ = 0, 512, 768, 1024
P_BX, P_BZ = 1536, 2048
P_CQ, P_CK, P_CV, P_CZ = 2560, 3072, 3584, 4096
P_DZ, P_DCKV, P_DCQ = 4608, 5120, 5376
P_SM = 5760
P_W = 6144
SM_AB = MLA_ROPE

TOK = 256
VMEM_LIMIT = 56 * 1024 * 1024
LOG2E = math.log2(math.e)
MAX_EXP2_LOGIT = 60.0

NN = ((1,), (0,))
NT = ((1,), (1,))
TN = ((0,), (0,))


def _dg(a, b, dims=NN):
    return lax.dot_general(a, b, (dims, ((), ())), preferred_element_type=F32)


def _split2(a):
    hi = a.astype(BF16)
    lo = (a - hi.astype(F32)).astype(BF16)
    return hi, lo


def _dot3(a, b, dims=NN):
    ah, al = _split2(a)
    bh, bl = _split2(b)
    return _dg(ah, bh, dims) + (_dg(ah, bl, dims) + _dg(al, bh, dims))


def _dot_exact_lhs(m, b):
    b0 = b.astype(BF16)
    r1 = b - b0.astype(F32)
    b1 = r1.astype(BF16)
    b2 = (r1 - b1.astype(F32)).astype(BF16)
    return _dg(m, b0) + (_dg(m, b1) + _dg(m, b2))


def _sigmoid(x):
    return 1.0 / (1.0 + jnp.exp(-x))


def _silu(x):
    return x * _sigmoid(x)


def _softplus(x):
    return jnp.maximum(x, 0.0) + jnp.log(1.0 + jnp.exp(-jnp.abs(x)))


def _rms(x, w, n):
    return x * lax.rsqrt(jnp.sum(x * x, axis=-1, keepdims=True) * (1.0 / n) + NORM_EPS) * w


def _rope(x, cos, sin_signed, half):
    lane = lax.broadcasted_iota(jnp.int32, x.shape, 1)
    first = (lane % (2 * half)) < half
    rot = jnp.where(first, pltpu.roll(x, HEAD - half, 1), pltpu.roll(x, half, 1))
    return x * cos + rot * sin_signed


def _params(*sem):
    return pltpu.CompilerParams(dimension_semantics=sem, vmem_limit_bytes=VMEM_LIMIT)


def _row_tile(rows):
    for t in (768, 512, 256):
        if rows % t == 0:
            return t
    raise ValueError(f"row count {rows} is not a multiple of {TOK}")


def _ada_kernel(c_ref, w_ref, b_ref, o_ref):
    s = _silu(c_ref[...])
    w = w_ref[0]
    r0 = jnp.sum(s[:, 0:1] * w, axis=0, keepdims=True)
    r1 = jnp.sum(s[:, 1:2] * w, axis=0, keepdims=True)
    o_ref[0, 0] = jnp.concatenate([r0, r1], axis=0) + b_ref[0]


def _ada(c_cols, w_ada, b_ada):
    depth, d, _ = w_ada.shape
    tn = 512
    per = d // tn
    return pl.pallas_call(
        _ada_kernel,
        grid=(depth, 3 * per),
        in_specs=[
            pl.BlockSpec((d, 2), lambda l, j: (0, 0)),
            pl.BlockSpec((1, d, tn), lambda l, j: (l, 0, j)),
            pl.BlockSpec((1, 1, tn), lambda l, j: (l, 0, j)),
        ],
        out_specs=pl.BlockSpec((1, 1, 2, tn), lambda l, j: (l, j // per, 0, j % per)),
        out_shape=jax.ShapeDtypeStruct((depth, 3, 2, d), F32),
        compiler_params=_params("parallel", "parallel"),
        name="ada",
    )(c_cols, w_ada, b_ada.reshape(depth, 1, 3 * d))


def _inproj_kernel(x_ref, nw_ref, shift_ref, scale_ref, w_ref, o_ref, h_ref, *, lat):
    i = pl.program_id(0)
    tm = x_ref.shape[0]

    @pl.when(pl.program_id(1) == 0)
    def _():
        y = _rms(x_ref[...], nw_ref[...], D_MODEL)
        row = i * tm + lax.broadcasted_iota(jnp.int32, (tm, 1), 0)
        is_ctx = row >= lat
        scale = jnp.where(is_ctx, scale_ref[1:2, :], scale_ref[0:1, :])
        shift = jnp.where(is_ctx, shift_ref[1:2, :], shift_ref[0:1, :])
        h_ref[...] = (y * (1.0 + scale) + shift).astype(BF16)

    o_ref[...] = _dg(h_ref[...], w_ref[...])


def _inproj(xs, norm_w, shift, scale, w_in, lat):
    t, d = xs.shape
    tm, tn = _row_tile(t), 768
    return pl.pallas_call(
        functools.partial(_inproj_kernel, lat=lat),
        grid=(t // tm, P_W // tn),
        in_specs=[
            pl.BlockSpec((tm, d), lambda i, j: (i, 0)),
            pl.BlockSpec((1, d), lambda i, j: (0, 0)),
            pl.BlockSpec((2, d), lambda i, j: (0, 0)),
            pl.BlockSpec((2, d), lambda i, j: (0, 0)),
            pl.BlockSpec((d, tn), lambda i, j: (0, j)),
        ],
        out_specs=pl.BlockSpec((tm, tn), lambda i, j: (i, j)),
        out_shape=jax.ShapeDtypeStruct((t, P_W), F32),
        scratch_shapes=[pltpu.VMEM((tm, d), BF16)],
        compiler_params=_params("parallel", "arbitrary"),
        name="inproj",
    )(xs, norm_w.reshape(1, d), shift, scale, w_in)


def _outproj_kernel(ya_ref, yb_ref, yc_ref, yd_ref, w_ref, x_ref, g_ref, o_ref, *, lat):
    i = pl.program_id(0)
    tm = x_ref.shape[0]
    acc = _dg(ya_ref[...], w_ref[0])
    acc += _dg(yb_ref[...], w_ref[1])
    acc += _dg(yc_ref[...], w_ref[2])
    acc += _dg(yd_ref[...], w_ref[3])
    row = i * tm + lax.broadcasted_iota(jnp.int32, (tm, 1), 0)
    gate = jnp.where(row >= lat, g_ref[1:2, :], g_ref[0:1, :])
    o_ref[...] = x_ref[...] + gate * acc


def _outproj(ys, w_out, xs, gate, lat, out_rows):
    d = xs.shape[1]
    tm, tn = _row_tile(out_rows), 1024
    yspec = pl.BlockSpec((tm, MIX), lambda i, j: (i, 0))
    return pl.pallas_call(
        functools.partial(_outproj_kernel, lat=lat),
        grid=(out_rows // tm, d // tn),
        in_specs=[yspec, yspec, yspec, yspec,
                  pl.BlockSpec((4, MIX, tn), lambda i, j: (0, 0, j)),
                  pl.BlockSpec((tm, tn), lambda i, j: (i, j)),
                  pl.BlockSpec((2, tn), lambda i, j: (0, j))],
        out_specs=pl.BlockSpec((tm, tn), lambda i, j: (i, j)),
        out_shape=jax.ShapeDtypeStruct((out_rows, d), F32),
        compiler_params=_params("parallel", "parallel"),
        name="outproj",
    )(*ys, w_out, xs, gate)


def _flash_kernel(q_ref, k_ref, v_ref, z_ref, o_ref, *, lat, tk, dq, shared_kv, bounded):
    tq = q_ref.shape[0]
    ctx = k_ref.shape[0] - lat
    nh = q_ref.shape[1] // dq
    is_ctx_tile = pl.program_id(1) == pl.num_programs(1) - 1
    ctx_rows = pl.ds(lat, ctx)

    def scores(c, rows):
        kc = 0 if shared_kv else c
        s = _dg(q_ref[:, c * dq:(c + 1) * dq], k_ref[rows, kc * dq:(kc + 1) * dq], NT)
        return s, v_ref[rows, kc * HEAD:(kc + 1) * HEAD]

    def attend_bounded(c, carry, rows):
        l, acc = carry
        s, v = scores(c, rows)
        p = jnp.exp2(s)
        for j in range(s.shape[1] // HEAD):
            l = l + p[:, j * HEAD:(j + 1) * HEAD]
        return l, acc + _dg(p.astype(BF16), v)

    def attend_online(c, carry, rows):
        m, l, acc = carry
        s, v = scores(c, rows)
        m_new = jnp.maximum(m, jnp.max(s, axis=-1, keepdims=True))
        alpha = jnp.exp2(m - m_new)
        p = jnp.exp2(s - m_new)
        l = alpha * l + jnp.sum(p, axis=-1, keepdims=True)
        return m_new, l, alpha * acc + _dg(p.astype(BF16), v)

    def finish(c, carry):
        l, acc = carry[-2], carry[-1]
        cs = slice(c * HEAD, (c + 1) * HEAD)
        o_ref[:, cs] = (acc / jnp.sum(l, axis=-1, keepdims=True) * _silu(z_ref[:, cs])).astype(o_ref.dtype)

    zeros = jnp.zeros((tq, HEAD), F32)
    if bounded:
        attend, init = attend_bounded, (zeros, zeros)
    else:
        attend, init = attend_online, (jnp.full((tq, 1), -1e30, F32), jnp.zeros((tq, 1), F32), zeros)

    @pl.when(jnp.logical_not(is_ctx_tile))
    def _():
        def step(kb, carries):
            rows = pl.ds(pl.multiple_of(kb * tk, tk), tk)
            return tuple(attend(c, carries[c], rows) for c in range(nh))

        carries = lax.fori_loop(0, lat // tk, step, (init,) * nh, unroll=2)
        for c in range(nh):
            finish(c, attend(c, carries[c], ctx_rows))

    @pl.when(is_ctx_tile)
    def _():
        for c in range(nh):
            finish(c, attend(c, init, ctx_rows))


def _flash(q, k, v, p, z_col, heads, kv_heads, dq, lat, logit_bound, name):
    t = q.shape[0]
    nh = 2
    shared_kv = heads // kv_heads == nh
    nk = 1 if shared_kv else nh
    tq = TOK
    tk = 512 if lat % 1024 == 0 else TOK
    zb = z_col // (nh * HEAD)

    def call(bounded):
        return pl.pallas_call(
            functools.partial(_flash_kernel, lat=lat, tk=tk, dq=dq, shared_kv=shared_kv, bounded=bounded),
            grid=(heads // nh, t // tq),
            in_specs=[
                pl.BlockSpec((tq, nh * dq), lambda g, i: (i, g)),
                pl.BlockSpec((t, nk * dq), lambda g, i: (0, g)),
                pl.BlockSpec((t, nk * HEAD), lambda g, i: (0, g)),
                pl.BlockSpec((tq, nh * HEAD), lambda g, i: (i, zb + g)),
            ],
            out_specs=pl.BlockSpec((tq, nh * HEAD), lambda g, i: (i, g)),
            out_shape=jax.ShapeDtypeStruct((t, heads * HEAD), BF16),
            compiler_params=_params("parallel", "parallel"),
            name=name + ("_bounded" if bounded else "_online"),
        )(q, k, v, p)

    return lax.cond(logit_bound <= MAX_EXP2_LOGIT, lambda: call(True), lambda: call(False))


def _gqa_prep_kernel(q_ref, k_ref, v_ref, cos_ref, sin_ref, qn_ref, kn_ref, oq_ref, ok_ref, ov_ref):
    cos, sin = cos_ref[...], sin_ref[...]
    scale = HEAD ** -0.5 * LOG2E
    for h in range(A_HEADS):
        c = slice(h * HEAD, (h + 1) * HEAD)
        oq_ref[:, c] = (_rope(_rms(q_ref[:, c], qn_ref[...], HEAD), cos, sin, 32) * scale).astype(BF16)
    for g in range(A_KV_HEADS):
        c = slice(g * HEAD, (g + 1) * HEAD)
        ok_ref[:, c] = _rope(_rms(k_ref[:, c], kn_ref[...], HEAD), cos, sin, 32).astype(BF16)
    ov_ref[...] = v_ref[...].astype(BF16)


def _gqa_prep(p, cos, sin, qn, kn):
    t = p.shape[0]
    tb = _row_tile(t)
    qw, kw = A_HEADS * HEAD, A_KV_HEADS * HEAD
    row = lambda w, c: pl.BlockSpec((tb, w), lambda i: (i, c))
    vec = pl.BlockSpec((1, HEAD), lambda i: (0, 0))
    return pl.pallas_call(
        _gqa_prep_kernel,
        grid=(t // tb,),
        in_specs=[row(qw, P_AQ // qw), row(kw, P_AK // kw), row(kw, P_AV // kw),
                  row(HEAD, 0), row(HEAD, 0), vec, vec],
        out_specs=[row(qw, 0), row(kw, 0), row(kw, 0)],
        out_shape=[jax.ShapeDtypeStruct((t, qw), BF16), jax.ShapeDtypeStruct((t, kw), BF16),
                   jax.ShapeDtypeStruct((t, kw), BF16)],
        compiler_params=_params("parallel"),
        name="gqa_prep",
    )(p, p, p, cos, sin, qn.reshape(1, HEAD), kn.reshape(1, HEAD))


def _mla_prep_kernel(cq_ref, ckv_ref, sm_ref, cos_ref, sin_ref, qnw_ref, kvnw_ref, wq_ref, wkv_ref,
                     qn_ref, kn_ref, oq_ref, ok_ref, ov_ref):
    cos, sin = cos_ref[...], sin_ref[...]
    scale = MLA_QK ** -0.5 * LOG2E
    qn, kn = qn_ref[...], kn_ref[...]
    q = _dg(_rms(cq_ref[...], qnw_ref[...], MLA_Q_RANK).astype(BF16), wq_ref[...])
    kv = _dg(_rms(ckv_ref[...], kvnw_ref[...], MLA_KV_RANK).astype(BF16), wkv_ref[...])
    lane = lax.broadcasted_iota(jnp.int32, sm_ref.shape, 1)
    kr = jnp.where(lane < MLA_ROPE, sm_ref[...], 0.0)
    kr_ss = jnp.sum(kr * kr, axis=-1, keepdims=True)
    for h in range(MLA_HEADS):
        lo = slice(2 * h * HEAD, (2 * h + 1) * HEAD)
        hi = slice((2 * h + 1) * HEAD, (2 * h + 2) * HEAD)
        q_nope, q_rope = q[:, lo], q[:, hi]
        r = lax.rsqrt((jnp.sum(q_nope * q_nope, axis=-1, keepdims=True)
                       + jnp.sum(q_rope * q_rope, axis=-1, keepdims=True)) * (1.0 / MLA_QK) + NORM_EPS)
        oq_ref[:, lo] = (q_nope * r * qn[:, :HEAD] * scale).astype(BF16)
        oq_ref[:, hi] = (_rope(q_rope * r * qn[:, HEAD:], cos, sin, 16) * scale).astype(BF16)
        k_nope, v = kv[:, lo], kv[:, hi]
        r = lax.rsqrt((jnp.sum(k_nope * k_nope, axis=-1, keepdims=True) + kr_ss) * (1.0 / MLA_QK) + NORM_EPS)
        ok_ref[:, lo] = (k_nope * r * kn[:, :HEAD]).astype(BF16)
        ok_ref[:, hi] = _rope(kr * r * kn[:, HEAD:], cos, sin, 16).astype(BF16)
        ov_ref[:, h * HEAD:(h + 1) * HEAD] = v.astype(BF16)


def _mla_prep(p, cos, sin, q_norm_w, kv_norm_w, wq, wkv, qn, kn):
    t = p.shape[0]
    tb = _row_tile(t)
    w2 = MLA_HEADS * 2 * HEAD
    row = lambda w, c: pl.BlockSpec((tb, w), lambda i: (i, c))
    full = lambda a: pl.BlockSpec(a.shape, lambda i: (0, 0))
    args = (q_norm_w.reshape(1, -1), kv_norm_w.reshape(1, -1), wq, wkv, qn, kn)
    return pl.pallas_call(
        _mla_prep_kernel,
        grid=(t // tb,),
        in_specs=[row(MLA_Q_RANK, P_DCQ // MLA_Q_RANK), row(MLA_KV_RANK, P_DCKV // MLA_KV_RANK),
                  row(HEAD, P_SM // HEAD), row(HEAD, 0), row(HEAD, 0)] + [full(a) for a in args],
        out_specs=[row(w2, 0), row(w2, 0), row(MLA_HEADS * HEAD, 0)],
        out_shape=[jax.ShapeDtypeStruct((t, w2), BF16), jax.ShapeDtypeStruct((t, w2), BF16),
                   jax.ShapeDtypeStruct((t, MLA_HEADS * HEAD), BF16)],
        compiler_params=_params("parallel"),
        name="mla_prep",
    )(p, p, p, cos, sin, *args)


def _lru_kernel(ux_ref, z_ref, cw_ref, cb_ref, wa_ref, ba_ref, wx_ref, bx_ref, lam_ref, o_ref,
                xs_ref, a_ref, hf_ref, hb_ref, *, lat):
    t = ux_ref.shape[0]
    ctx = t - lat
    nb, nb_lat = t // TOK, lat // TOK
    cw, cb = cw_ref[...], cb_ref[...]

    def conv_block(b, _):
        t0 = pl.multiple_of(b * TOK, TOK)
        first = jnp.logical_or(b == 0, b == nb_lat)
        last = jnp.logical_or(b == nb_lat - 1, b == nb - 1)
        prev = ux_ref[pl.ds(pl.multiple_of(jnp.maximum(t0 - 8, 0), 8), 8), :]
        nxt = ux_ref[pl.ds(pl.multiple_of(jnp.minimum(t0 + TOK, t - 8), 8), 8), :]
        xe = jnp.concatenate([jnp.where(first, 0.0, prev), ux_ref[pl.ds(t0, TOK), :],
                              jnp.where(last, 0.0, nxt)], axis=0)
        y = cb + xe[6:6 + TOK] * cw[0:1]
        for j in range(1, 4):
            y = y + xe[6 + j:6 + j + TOK] * cw[j:j + 1]
        xs_ref[pl.ds(t0, TOK), :] = y
        return 0

    lax.fori_loop(0, nb, conv_block, 0)

    def gates(d, h_ref):
        wa, wx = wa_ref[d, 0], wx_ref[d, 0]
        ba, bx = ba_ref[d:d + 1, :], bx_ref[d:d + 1, :]
        sp = _softplus(-lam_ref[d:d + 1, :])

        def block(b, _):
            rows = pl.ds(pl.multiple_of(b * TOK, TOK), TOK)
            xs = xs_ref[rows, :]
            r = _sigmoid(_dot3(xs, wa) + ba)
            gi = _sigmoid(_dot3(xs, wx) + bx)
            log_a = -LRU_C * r * sp
            a_ref[rows, :] = jnp.exp(log_a)
            h_ref[rows, :] = jnp.sqrt(1.0 - jnp.exp(2.0 * log_a)) * (gi * xs)
            return 0

        lax.fori_loop(0, nb, block, 0)

    def scan_region(h_ref, r0, length, h0, reverse):
        ls = length // 8

        def step(s, carry):
            h, ac = carry
            rows = pl.ds(r0 + (ls - 1 - s if reverse else s), 8, stride=ls)
            a = a_ref[rows, :]
            h = a * h + h_ref[rows, :]
            ac = ac * a
            h_ref[rows, :] = h
            a_ref[rows, :] = ac
            return h, ac

        hk, ak = lax.fori_loop(0, ls, step, (jnp.zeros((8, HEAD), F32), jnp.ones((8, HEAD), F32)))
        carry_in = [None] * 8
        c = h0
        for k in (range(7, -1, -1) if reverse else range(8)):
            carry_in[k] = c
            c = ak[k:k + 1] * c + hk[k:k + 1]
        blk = min(ls, TOK)
        for k in range(8):
            def fix(b, _, k=k):
                rows = pl.ds(pl.multiple_of(r0 + k * ls + b * blk, 8), blk)
                h_ref[rows, :] = h_ref[rows, :] + a_ref[rows, :] * carry_in[k]
                return 0
            lax.fori_loop(0, ls // blk, fix, 0)
        return c

    zero = jnp.zeros((1, HEAD), F32)
    for d, h_ref in ((0, hf_ref), (1, hb_ref)):
        gates(d, h_ref)
        s_ctx = scan_region(h_ref, lat, ctx, zero, reverse=(d == 1))
        scan_region(h_ref, 0, lat, s_ctx, reverse=(d == 1))

    def out_block(b, _):
        rows = pl.ds(pl.multiple_of(b * TOK, TOK), TOK)
        o_ref[rows, :] = ((hf_ref[rows, :] + hb_ref[rows, :]) * _silu(z_ref[rows, :])).astype(BF16)
        return 0

    lax.fori_loop(0, nb, out_block, 0)


def _lru(p, conv_w, conv_b, w_a, b_a, w_x, b_x, lam, lat):
    t = p.shape[0]
    col = lambda c0: pl.BlockSpec((t, HEAD), lambda n: (0, c0 // HEAD + n))
    vec = lambda r: pl.BlockSpec((r, HEAD), lambda n: (0, n))
    mat = pl.BlockSpec((2, 1, HEAD, HEAD), lambda n: (0, n, 0, 0))
    return pl.pallas_call(
        functools.partial(_lru_kernel, lat=lat),
        grid=(LRU_BLOCKS,),
        in_specs=[col(P_BX), col(P_BZ), vec(4), vec(1), mat, vec(2), mat, vec(2), vec(2)],
        out_specs=pl.BlockSpec((t, HEAD), lambda n: (0, n)),
        out_shape=jax.ShapeDtypeStruct((t, LRU_WIDTH), BF16),
        scratch_shapes=[pltpu.VMEM((t, HEAD), F32)] * 4,
        compiler_params=_params("parallel"),
        name="lru",
    )(p, p, conv_w, conv_b.reshape(1, -1), w_a, b_a, w_x, b_x, lam)


def _tri_inv(lm, eye):
    x = eye - lm
    pw = _dot3(lm, lm)
    for _ in range(4):
        x = x + _dot3(x, pw)
        pw = _dot3(pw, pw)
    return x + _dot3(x, pw)


def _dn_prep_kernel(q_ref, k_ref, v_ref, qp_ref, kp_ref, vp_ref, qx_ref, kx_ref, vx_ref, cw_ref, sm_ref,
                    alog_ref, dtb_ref, u_ref, w_ref, qd_ref, kd_ref, qk_ref, gl_ref, *, nb_lat):
    i = pl.program_id(0)
    tb = q_ref.shape[0]
    cc = DN_CHUNK
    first = jnp.logical_or(i == 0, i == nb_lat)
    last = jnp.logical_or(i == nb_lat - 1, i == nb_lat)
    cw = cw_ref[...]

    def conv_silu(x_ref, prev_ref, next_ref, c0):
        xe = jnp.concatenate([jnp.where(first, 0.0, prev_ref[...]), x_ref[...],
                              jnp.where(last, 0.0, next_ref[...])], axis=0)
        y = xe[6:6 + tb] * cw[0:1, c0:c0 + MIX]
        for j in range(1, 4):
            y = y + xe[6 + j:6 + j + tb] * cw[j:j + 1, c0:c0 + MIX]
        return _silu(y)

    q = conv_silu(q_ref, qp_ref, qx_ref, 0)
    k = conv_silu(k_ref, kp_ref, kx_ref, MIX)
    v = conv_silu(v_ref, vp_ref, vx_ref, 2 * MIX)

    sm = sm_ref[...]
    beta_all = _sigmoid(sm)
    g_all = -jnp.exp(alog_ref[...]) * _softplus(sm + dtb_ref[...])

    r = lax.broadcasted_iota(jnp.int32, (tb, tb), 0)
    c = lax.broadcasted_iota(jnp.int32, (tb, tb), 1)
    same = (r // cc) == (c // cc)
    tri_f = jnp.where(same, jnp.where(c <= r, 1.0, 0.0), 0.0).astype(BF16)
    tri_b = jnp.where(same, jnp.where(c >= r, 1.0, 0.0), 0.0).astype(BF16)
    gcs = (_dot_exact_lhs(tri_f, g_all), _dot_exact_lhs(tri_b, g_all))

    ii = lax.broadcasted_iota(jnp.int32, (cc, cc), 0)
    jj = lax.broadcasted_iota(jnp.int32, (cc, cc), 1)
    eye = jnp.where(ii == jj, 1.0, 0.0)
    incl = (ii >= jj, ii <= jj)
    strict = (ii > jj, ii < jj)

    gl_ref[...] = jnp.zeros_like(gl_ref)
    qs, ks = [], []
    for h in range(DN_HEADS):
        hs = slice(h * HEAD, (h + 1) * HEAD)
        qh, kh = q[:, hs], k[:, hs]
        qs.append(qh * lax.rsqrt(jnp.sum(qh * qh, axis=-1, keepdims=True) + NORM_EPS) * (HEAD ** -0.5))
        ks.append(kh * lax.rsqrt(jnp.sum(kh * kh, axis=-1, keepdims=True) + NORM_EPS))

    for ch in range(tb // cc):
        rows = slice(ch * cc, (ch + 1) * cc)
        for d in range(2):
            gc = gcs[d][rows]
            gct = gc.T
            end = cc - 1 if d == 0 else 0
            for h in range(DN_HEADS):
                hs = slice(h * HEAD, (h + 1) * HEAD)
                lb = SM_AB + d * 2 * DN_HEADS + h
                lg = lb + DN_HEADS
                beta = beta_all[rows, lb:lb + 1]
                gcol, grow = gc[:, lg:lg + 1], gct[lg:lg + 1, :]
                glast = gc[end:end + 1, lg:lg + 1]
                qh, kh, vh = qs[h][rows], ks[h][rows], v[rows, hs]
                decay = jnp.where(incl[d], jnp.exp(jnp.where(incl[d], gcol - grow, 0.0)), 0.0)
                kb = kh * beta
                lm = jnp.where(strict[d], _dot3(kb, kh, NT) * decay, 0.0)
                qk = _dot3(qh, kh, NT) * decay
                eg = jnp.exp(gcol)
                sol = _dot3(_tri_inv(lm, eye), jnp.concatenate([vh * beta, kb * eg], axis=1))
                u_ref[d, rows, hs] = sol[:, :HEAD]
                w_ref[d, rows, hs] = sol[:, HEAD:]
                qd_ref[d, rows, hs] = qh * eg
                kd_ref[d, rows, hs] = kh * jnp.exp(glast - gcol)
                qk_ref[d, h, rows, :] = qk
                gl_ref[d, ch, h:h + 1, :] = jnp.broadcast_to(jnp.exp(glast), (1, HEAD))


def _dn_prep(p, conv_w, alog_row, dtb_row, lat):
    t = p.shape[0]
    tb = TOK
    nb, nb_lat = t // tb, lat // tb
    r8 = tb // 8
    blk = lambda c0: pl.BlockSpec((tb, MIX), lambda i: (i, c0 // MIX))
    prev = lambda c0: pl.BlockSpec((8, MIX), lambda i: (jnp.maximum(i * r8 - 1, 0), c0 // MIX))
    nxt = lambda c0: pl.BlockSpec((8, MIX), lambda i: (jnp.minimum((i + 1) * r8, t // 8 - 1), c0 // MIX))
    cols = (P_CQ, P_CK, P_CV)
    big = pl.BlockSpec((2, tb, MIX), lambda i: (0, i, 0))
    big_shape = jax.ShapeDtypeStruct((2, t, MIX), F32)
    return pl.pallas_call(
        functools.partial(_dn_prep_kernel, nb_lat=nb_lat),
        grid=(nb,),
        in_specs=[blk(c0) for c0 in cols] + [prev(c0) for c0 in cols] + [nxt(c0) for c0 in cols] + [
            pl.BlockSpec((4, 3 * MIX), lambda i: (0, 0)),
            pl.BlockSpec((tb, HEAD), lambda i: (i, P_SM // HEAD)),
            pl.BlockSpec((1, HEAD), lambda i: (0, 0)),
            pl.BlockSpec((1, HEAD), lambda i: (0, 0))],
        out_specs=[big, big, big, big,
                   pl.BlockSpec((2, DN_HEADS, tb, DN_CHUNK), lambda i: (0, 0, i, 0)),
                   pl.BlockSpec((2, tb // DN_CHUNK, 8, HEAD), lambda i: (0, i, 0, 0))],
        out_shape=[big_shape, big_shape, big_shape, big_shape,
                   jax.ShapeDtypeStruct((2, DN_HEADS, t, DN_CHUNK), F32),
                   jax.ShapeDtypeStruct((2, t // DN_CHUNK, 8, HEAD), F32)],
        compiler_params=_params("parallel"),
        name="dn_prep",
    )(*([p] * 9), conv_w, p, alog_row, dtb_row)


def _dn_scan_kernel(uf, wf, qdf, kdf, qkf, glf, ub, wb, qdb, kdb, qkb, glb, of_ref, ob_ref, s_ref):
    @pl.when(pl.program_id(0) == 0)
    def _():
        s_ref[...] = jnp.zeros_like(s_ref)

    dirs = ((uf, wf, qdf, kdf, qkf, glf, of_ref), (ub, wb, qdb, kdb, qkb, glb, ob_ref))
    for d, (u_ref, w_ref, qd_ref, kd_ref, qk_ref, gl_ref, o_ref) in enumerate(dirs):
        for h in range(DN_HEADS):
            hs = slice(h * HEAD, (h + 1) * HEAD)
            s = s_ref[d, h]
            v_new = u_ref[0, :, hs] - _dot3(w_ref[0, :, hs], s)
            o_ref[:, hs] = _dot3(qd_ref[0, :, hs], s) + _dot3(qk_ref[0, h], v_new)
            s_ref[d, h] = s * gl_ref[0, 0, h:h + 1, :] + _dot3(kd_ref[0, :, hs], v_new, TN)


def _dn_scan(u, w, qd, kd, qk, gl, lat):
    t = u.shape[1]
    cc = DN_CHUNK
    n, n_lat = t // cc, lat // cc
    n_ctx = n - n_lat
    cf = lambda i: jnp.where(i < n_ctx, n_lat + i, i - n_ctx)
    cb = lambda i: n - 1 - i
    specs = []
    for d, ch in ((0, cf), (1, cb)):
        big = pl.BlockSpec((1, cc, MIX), lambda i, d=d, ch=ch: (d, ch(i), 0))
        specs += [big, big, big, big,
                  pl.BlockSpec((1, DN_HEADS, cc, cc), lambda i, d=d, ch=ch: (d, 0, ch(i), 0)),
                  pl.BlockSpec((1, 1, 8, HEAD), lambda i, d=d, ch=ch: (d, ch(i), 0, 0))]
    return pl.pallas_call(
        _dn_scan_kernel,
        grid=(n,),
        in_specs=specs,
        out_specs=[pl.BlockSpec((cc, MIX), lambda i: (cf(i), 0)), pl.BlockSpec((cc, MIX), lambda i: (cb(i), 0))],
        out_shape=[jax.ShapeDtypeStruct((t, MIX), F32)] * 2,
        scratch_shapes=[pltpu.VMEM((2, DN_HEADS, HEAD, HEAD), F32)],
        compiler_params=_params("arbitrary"),
        name="dn_scan",
    )(u, w, qd, kd, qk, gl, u, w, qd, kd, qk, gl)


def _dn_out_kernel(of_ref, ob_ref, z_ref, nw_ref, o_ref):
    for h in range(DN_HEADS):
        hs = slice(h * HEAD, (h + 1) * HEAD)
        o = of_ref[:, hs] + ob_ref[:, hs]
        o_ref[:, hs] = (_rms(o, nw_ref[...], HEAD) * _silu(z_ref[:, hs])).astype(BF16)


def _dn_out(o_f, o_b, p, norm_w):
    t = p.shape[0]
    tb = _row_tile(t)
    row = lambda c: pl.BlockSpec((tb, MIX), lambda i: (i, c))
    return pl.pallas_call(
        _dn_out_kernel,
        grid=(t // tb,),
        in_specs=[row(0), row(0), row(P_CZ // MIX), pl.BlockSpec((1, HEAD), lambda i: (0, 0))],
        out_specs=row(0),
        out_shape=jax.ShapeDtypeStruct((t, MIX), BF16),
        compiler_params=_params("parallel"),
        name="dn_out",
    )(o_f, o_b, p, norm_w.reshape(1, HEAD))


def _arrange_w_in(w_in):
    pad = jnp.zeros(w_in.shape[:2] + (P_W - w_in.shape[2],), w_in.dtype)
    parts = [w_in[..., 0:4608],
             w_in[..., 5328:5840],
             w_in[..., 5008:5264],
             w_in[..., 4624:5008],
             w_in[..., 5264:5328],
             w_in[..., 4608:4624],
             pad]
    return jnp.concatenate(parts, axis=-1).astype(BF16)


def _arrange_mla_wq(w_uq):
    l, r, _ = w_uq.shape
    w = w_uq.reshape(l, r, MLA_HEADS, MLA_QK)
    w = jnp.pad(w, ((0, 0), (0, 0), (0, 0), (0, 2 * HEAD - MLA_QK)))
    return w.reshape(l, r, MLA_HEADS * 2 * HEAD).astype(BF16)


def _pad_qk_norm(w):
    return jnp.pad(w, ((0, 0), (0, 2 * HEAD - MLA_QK)))[:, None, :]


def _small_lane_row(vals):
    l = vals.shape[0]
    row = jnp.zeros((l, HEAD), F32)
    for d in range(2):
        lo = SM_AB + d * 2 * DN_HEADS + DN_HEADS
        row = row.at[:, lo:lo + DN_HEADS].set(vals[:, d, :])
    return row[:, None, :]


def _rope_tables(lat, ctx):
    t = jnp.arange(lat)
    rows, cols = (t // GRID_W).astype(F32), (t % GRID_W).astype(F32)
    lane = jnp.arange(HEAD)

    def table(half, width):
        inv_freq = ROPE_THETA ** (-jnp.arange(half, dtype=F32) / half)
        pos = jnp.where((lane // (2 * half))[None, :] == 0, rows[:, None], cols[:, None])
        ang = pos * inv_freq[lane % half][None, :]
        live = (lane < width)[None, :]
        cos = jnp.where(live, jnp.cos(ang), 0.0)
        sin = jnp.where(live, jnp.sin(ang), 0.0) * jnp.where((lane % (2 * half)) < half, -1.0, 1.0)[None, :]
        cos_c = jnp.broadcast_to(jnp.where(live, 1.0, 0.0), (ctx, HEAD))
        return (jnp.concatenate([cos, cos_c], axis=0).astype(F32),
                jnp.concatenate([sin, jnp.zeros((ctx, HEAD), F32)], axis=0).astype(F32))

    return table(32, HEAD), table(16, MLA_ROPE)


def kernel(x, c, ctx, c_ctx, norm_w, w_ada, b_ada, w_in, w_out, attn_q_norm, attn_k_norm, lru_conv_w, lru_conv_b, lru_w_a, lru_b_a, lru_w_x, lru_b_x, lru_lambda, dn_conv_w, dn_a_log, dn_dt_bias, dn_norm_w, mla_q_norm, mla_kv_norm, mla_w_uq, mla_w_ukv, mla_q_qk_norm, mla_k_qk_norm):
    assert x.shape[0] == 1 and ctx.shape[1] == TOK and x.shape[1] % (8 * TOK) == 0
    lat, n_ctx = x.shape[1], ctx.shape[1]
    depth = w_in.shape[0]

    xs = jnp.concatenate([x[0], ctx[0]], axis=0)
    mod = _ada(jnp.stack([c[0], c_ctx], axis=1), w_ada, b_ada)
    (cos_a, sin_a), (cos_m, sin_m) = _rope_tables(lat, n_ctx)
    w_in_r = _arrange_w_in(w_in)
    w_out_r = w_out.reshape(depth, 4, MIX, D_MODEL).astype(BF16)
    wq_r = _arrange_mla_wq(mla_w_uq)
    wkv_r = mla_w_ukv.astype(BF16)
    qn_r, kn_r = _pad_qk_norm(mla_q_qk_norm), _pad_qk_norm(mla_k_qk_norm)
    alog_r, dtb_r = _small_lane_row(dn_a_log), _small_lane_row(dn_dt_bias)

    for l in range(depth):
        p = _inproj(xs, norm_w[l], mod[l, 0], mod[l, 1], w_in_r[l], lat)
        qa, ka, va = _gqa_prep(p, cos_a, sin_a, attn_q_norm[l], attn_k_norm[l])
        bound_a = jnp.max(jnp.abs(attn_q_norm[l])) * jnp.max(jnp.abs(attn_k_norm[l])) * (HEAD ** 0.5 * LOG2E)
        y_a = _flash(qa, ka, va, p, P_AZ, A_HEADS, A_KV_HEADS, HEAD, lat, bound_a, "gqa_attn")
        y_b = _lru(p, lru_conv_w[l], lru_conv_b[l], lru_w_a[l], lru_b_a[l], lru_w_x[l], lru_b_x[l],
                   lru_lambda[l], lat)
        u, w, qd, kd, qk, gl = _dn_prep(p, dn_conv_w[l], alog_r[l], dtb_r[l], lat)
        o_f, o_b = _dn_scan(u, w, qd, kd, qk, gl, lat)
        y_c = _dn_out(o_f, o_b, p, dn_norm_w[l])
        qm, km, vm = _mla_prep(p, cos_m, sin_m, mla_q_norm[l], mla_kv_norm[l], wq_r[l], wkv_r[l],
                               qn_r[l], kn_r[l])
        bound_d = (jnp.max(jnp.abs(mla_q_qk_norm[l])) * jnp.max(jnp.abs(mla_k_qk_norm[l]))
                   * (MLA_QK ** 0.5 * LOG2E))
        y_d = _flash(qm, km, vm, p, P_DZ, MLA_HEADS, MLA_HEADS, 2 * HEAD, lat, bound_d, "mla_attn")
        last = l == depth - 1
        xs = _outproj((y_a, y_b, y_c, y_d), w_out_r[l], xs, mod[l, 2], lat, lat if last else lat + n_ctx)
    return xs[None]
```

```python
import functools
import math

import jax
import jax.numpy as jnp
from jax import lax
from jax.experimental import pallas as pl
from jax.experimental.pallas import tpu as pltpu

F32 = jnp.float32
BF16 = jnp.bfloat16

D_MODEL = 2048
DEPTH = 4
GRID_W = 64
ROPE_THETA = 10000.0
NORM_EPS = 1e-6
HEAD = 128
A_HEADS, A_KV_HEADS = 4, 2
LRU_WIDTH, LRU_BLOCKS, LRU_C = 512, 4, 8.0
DN_HEADS, DN_CHUNK = 4, 64
MLA_HEADS, MLA_Q_RANK, MLA_KV_RANK, MLA_NOPE, MLA_ROPE = 4, 384, 256, 128, 64
MLA_QK = MLA_NOPE + MLA_ROPE
MIX = 512

P_AQ, P_AK, P_AV, P_AZ = 0, 512, 768, 1024
P_BX, P_BZ = 1536, 2048
P_CQ, P_CK, P_CV, P_CZ = 2560, 3072, 3584, 4096
P_DZ, P_DCKV, P_DCQ = 4608, 5120, 5376
P_SM = 5760
P_W = 6144
SM_AB = MLA_ROPE

TOK = 256
VMEM_LIMIT = 56 * 1024 * 1024
LOG2E = math.log2(math.e)
MAX_EXP2_LOGIT = 60.0

NN = ((1,), (0,))
NT = ((1,), (1,))
TN = ((0,), (0,))


def _dg(a, b, dims=NN):
    return lax.dot_general(a, b, (dims, ((), ())), preferred_element_type=F32)


def _split2(a):
    hi = a.astype(BF16)
    lo = (a - hi.astype(F32)).astype(BF16)
    return hi, lo


def _dot3s(a, b, dims=NN):
    return _dg(a[0], b[0], dims) + (_dg(a[0], b[1], dims) + _dg(a[1], b[0], dims))


def _dot3(a, b, dims=NN):
    return _dot3s(_split2(a), _split2(b), dims)


def _dot_exact_lhs(m, b):
    b0 = b.astype(BF16)
    r1 = b - b0.astype(F32)
    b1 = r1.astype(BF16)
    b2 = (r1 - b1.astype(F32)).astype(BF16)
    return _dg(m, b0) + (_dg(m, b1) + _dg(m, b2))


def _sigmoid(x):
    return 1.0 / (1.0 + jnp.exp(-x))


def _silu(x):
    return x * _sigmoid(x)


def _softplus(x):
    return jnp.maximum(x, 0.0) + jnp.log(1.0 + jnp.exp(-jnp.abs(x)))


def _rms(x, w, n):
    return x * lax.rsqrt(jnp.sum(x * x, axis=-1, keepdims=True) * (1.0 / n) + NORM_EPS) * w


def _rope(x, cos, sin_signed, half):
    lane = lax.broadcasted_iota(jnp.int32, x.shape, 1)
    first = (lane % (2 * half)) < half
    rot = jnp.where(first, pltpu.roll(x, HEAD - half, 1), pltpu.roll(x, half, 1))
    return x * cos + rot * sin_signed


def _params(*sem):
    return pltpu.CompilerParams(dimension_semantics=sem, vmem_limit_bytes=VMEM_LIMIT)


def _row_tile(rows):
    for t in (768, 512, 256):
        if rows % t == 0:
            return t
    raise ValueError(f"row count {rows} is not a multiple of {TOK}")


def _ada_kernel(c_ref, w_ref, b_ref, o_ref):
    s = _silu(c_ref[...])
    w = w_ref[0]
    r0 = jnp.sum(s[:, 0:1] * w, axis=0, keepdims=True)
    r1 = jnp.sum(s[:, 1:2] * w, axis=0, keepdims=True)
    o_ref[0, 0] = jnp.concatenate([r0, r1], axis=0) + b_ref[0]


def _ada(c_cols, w_ada, b_ada):
    depth, d, _ = w_ada.shape
    tn = 512
    per = d // tn
    return pl.pallas_call(
        _ada_kernel,
        grid=(depth, 3 * per),
        in_specs=[
            pl.BlockSpec((d, 2), lambda l, j: (0, 0)),
            pl.BlockSpec((1, d, tn), lambda l, j: (l, 0, j)),
            pl.BlockSpec((1, 1, tn), lambda l, j: (l, 0, j)),
        ],
        out_specs=pl.BlockSpec((1, 1, 2, tn), lambda l, j: (l, j // per, 0, j % per)),
        out_shape=jax.ShapeDtypeStruct((depth, 3, 2, d), F32),
        compiler_params=_params("parallel", "parallel"),
        name="ada",
    )(c_cols, w_ada, b_ada.reshape(depth, 1, 3 * d))


def _inproj_kernel(x_ref, nw_ref, shift_ref, scale_ref, w_ref, o_ref, h_ref, *, lat):
    i = pl.program_id(0)
    tm = x_ref.shape[0]

    @pl.when(pl.program_id(1) == 0)
    def _():
        rc = 32

        def chunk(ci, _):
            rows = pl.ds(pl.multiple_of(ci * rc, rc), rc)
            y = _rms(x_ref[rows, :], nw_ref[...], D_MODEL)
            is_ctx = i * tm + ci * rc + lax.broadcasted_iota(jnp.int32, (rc, 1), 0) >= lat
            scale = jnp.where(is_ctx, scale_ref[1:2, :], scale_ref[0:1, :])
            shift = jnp.where(is_ctx, shift_ref[1:2, :], shift_ref[0:1, :])
            h_ref[rows, :] = (y * (1.0 + scale) + shift).astype(BF16)
            return 0

        lax.fori_loop(0, tm // rc, chunk, 0)

    o_ref[...] = _dg(h_ref[...], w_ref[...])


def _inproj(xs, norm_w, shift, scale, w_in, lat):
    t, d = xs.shape
    tm, tn = (1408 if t % 1408 == 0 else _row_tile(t)), 768
    return pl.pallas_call(
        functools.partial(_inproj_kernel, lat=lat),
        grid=(t // tm, P_W // tn),
        in_specs=[
            pl.BlockSpec((tm, d), lambda i, j: (i, 0)),
            pl.BlockSpec((1, d), lambda i, j: (0, 0)),
            pl.BlockSpec((2, d), lambda i, j: (0, 0)),
            pl.BlockSpec((2, d), lambda i, j: (0, 0)),
            pl.BlockSpec((d, tn), lambda i, j: (0, j)),
        ],
        out_specs=pl.BlockSpec((tm, tn), lambda i, j: (i, j)),
        out_shape=jax.ShapeDtypeStruct((t, P_W), F32),
        scratch_shapes=[pltpu.VMEM((tm, d), BF16)],
        compiler_params=_params("parallel", "arbitrary"),
        name="inproj",
    )(xs, norm_w.reshape(1, d), shift, scale, w_in)


def _outproj_kernel(ya_ref, yb_ref, yc_ref, yd_ref, w_ref, x_ref, g_ref, o_ref, *, lat):
    i = pl.program_id(0)
    tm = x_ref.shape[0]
    acc = _dg(ya_ref[...], w_ref[0])
    acc += _dg(yb_ref[...], w_ref[1])
    acc += _dg(yc_ref[...], w_ref[2])
    acc += _dg(yd_ref[...], w_ref[3])
    row = i * tm + lax.broadcasted_iota(jnp.int32, (tm, 1), 0)
    gate = jnp.where(row >= lat, g_ref[1:2, :], g_ref[0:1, :])
    o_ref[...] = x_ref[...] + gate * acc


def _outproj(ys, w_out, xs, gate, lat, out_rows):
    d = xs.shape[1]
    tm, tn = _row_tile(out_rows), 1024
    yspec = pl.BlockSpec((tm, MIX), lambda i, j: (i, 0))
    return pl.pallas_call(
        functools.partial(_outproj_kernel, lat=lat),
        grid=(out_rows // tm, d // tn),
        in_specs=[yspec, yspec, yspec, yspec,
                  pl.BlockSpec((4, MIX, tn), lambda i, j: (0, 0, j)),
                  pl.BlockSpec((tm, tn), lambda i, j: (i, j)),
                  pl.BlockSpec((2, tn), lambda i, j: (0, j))],
        out_specs=pl.BlockSpec((tm, tn), lambda i, j: (i, j)),
        out_shape=jax.ShapeDtypeStruct((out_rows, d), F32),
        compiler_params=_params("parallel", "parallel"),
        name="outproj",
    )(*ys, w_out, xs, gate)


def _flash_kernel(q_ref, k_ref, v_ref, z_ref, o_ref, *, lat, tk, dq, shared_kv, bounded):
    tq = q_ref.shape[0]
    ctx = k_ref.shape[0] - lat
    nh = q_ref.shape[1] // dq
    is_ctx_tile = pl.program_id(1) == pl.num_programs(1) - 1
    ctx_rows = pl.ds(lat, ctx)

    def scores(c, rows):
        kc = 0 if shared_kv else c
        s = _dg(q_ref[:, c * dq:(c + 1) * dq], k_ref[rows, kc * dq:(kc + 1) * dq], NT)
        return s, v_ref[rows, kc * HEAD:(kc + 1) * HEAD]

    def attend_bounded(c, carry, rows):
        l, acc = carry
        s, v = scores(c, rows)
        p = jnp.exp2(s)
        for j in range(s.shape[1] // HEAD):
            l = l + p[:, j * HEAD:(j + 1) * HEAD]
        return l, acc + _dg(p.astype(BF16), v)

    def attend_online(c, carry, rows):
        m, l, acc = carry
        s, v = scores(c, rows)
        m_new = jnp.maximum(m, jnp.max(s, axis=-1, keepdims=True))
        alpha = jnp.exp2(m - m_new)
        p = jnp.exp2(s - m_new)
        l = alpha * l + jnp.sum(p, axis=-1, keepdims=True)
        return m_new, l, alpha * acc + _dg(p.astype(BF16), v)

    def finish(c, carry):
        l, acc = carry[-2], carry[-1]
        cs = slice(c * HEAD, (c + 1) * HEAD)
        o_ref[:, cs] = (acc / jnp.sum(l, axis=-1, keepdims=True) * _silu(z_ref[:, cs])).astype(o_ref.dtype)

    zeros = jnp.zeros((tq, HEAD), F32)
    if bounded:
        attend, init = attend_bounded, (zeros, zeros)
    else:
        attend, init = attend_online, (jnp.full((tq, 1), -1e30, F32), jnp.zeros((tq, 1), F32), zeros)

    @pl.when(jnp.logical_not(is_ctx_tile))
    def _():
        def step(kb, carries):
            rows = pl.ds(pl.multiple_of(kb * tk, tk), tk)
            return tuple(attend(c, carries[c], rows) for c in range(nh))

        carries = lax.fori_loop(0, lat // tk, step, (init,) * nh, unroll=2)
        for c in range(nh):
            finish(c, attend(c, carries[c], ctx_rows))

    @pl.when(is_ctx_tile)
    def _():
        for c in range(nh):
            finish(c, attend(c, init, ctx_rows))


def _flash(q, k, v, p, z_col, heads, kv_heads, dq, lat, logit_bound, name):
    t = q.shape[0]
    nh = 2
    shared_kv = heads // kv_heads == nh
    nk = 1 if shared_kv else nh
    tq = TOK
    tk = 1024 if lat % 2048 == 0 else TOK
    zb = z_col // (nh * HEAD)

    def call(bounded):
        return pl.pallas_call(
            functools.partial(_flash_kernel, lat=lat, tk=tk, dq=dq, shared_kv=shared_kv, bounded=bounded),
            grid=(heads // nh, t // tq),
            in_specs=[
                pl.BlockSpec((tq, nh * dq), lambda g, i: (i, g)),
                pl.BlockSpec((t, nk * dq), lambda g, i: (0, g)),
                pl.BlockSpec((t, nk * HEAD), lambda g, i: (0, g)),
                pl.BlockSpec((tq, nh * HEAD), lambda g, i: (i, zb + g)),
            ],
            out_specs=pl.BlockSpec((tq, nh * HEAD), lambda g, i: (i, g)),
            out_shape=jax.ShapeDtypeStruct((t, heads * HEAD), BF16),
            compiler_params=_params("parallel", "parallel"),
            name=name + ("_bounded" if bounded else "_online"),
        )(q, k, v, p)

    return lax.cond(logit_bound <= MAX_EXP2_LOGIT, lambda: call(True), lambda: call(False))


def _gqa_prep_kernel(q_ref, k_ref, v_ref, cos_ref, sin_ref, qn_ref, kn_ref, oq_ref, ok_ref, ov_ref):
    cos, sin = cos_ref[...], sin_ref[...]
    scale = HEAD ** -0.5 * LOG2E
    for h in range(A_HEADS):
        c = slice(h * HEAD, (h + 1) * HEAD)
        oq_ref[:, c] = (_rope(_rms(q_ref[:, c], qn_ref[...], HEAD), cos, sin, 32) * scale).astype(BF16)
    for g in range(A_KV_HEADS):
        c = slice(g * HEAD, (g + 1) * HEAD)
        ok_ref[:, c] = _rope(_rms(k_ref[:, c], kn_ref[...], HEAD), cos, sin, 32).astype(BF16)
    ov_ref[...] = v_ref[...].astype(BF16)


def _gqa_prep(p, cos, sin, qn, kn):
    t = p.shape[0]
    tb = _row_tile(t)
    qw, kw = A_HEADS * HEAD, A_KV_HEADS * HEAD
    row = lambda w, c: pl.BlockSpec((tb, w), lambda i: (i, c))
    vec = pl.BlockSpec((1, HEAD), lambda i: (0, 0))
    return pl.pallas_call(
        _gqa_prep_kernel,
        grid=(t // tb,),
        in_specs=[row(qw, P_AQ // qw), row(kw, P_AK // kw), row(kw, P_AV // kw),
                  row(HEAD, 0), row(HEAD, 0), vec, vec],
        out_specs=[row(qw, 0), row(kw, 0), row(kw, 0)],
        out_shape=[jax.ShapeDtypeStruct((t, qw), BF16), jax.ShapeDtypeStruct((t, kw), BF16),
                   jax.ShapeDtypeStruct((t, kw), BF16)],
        compiler_params=_params("parallel"),
        name="gqa_prep",
    )(p, p, p, cos, sin, qn.reshape(1, HEAD), kn.reshape(1, HEAD))


def _mla_prep_kernel(cq_ref, ckv_ref, sm_ref, cos_ref, sin_ref, qnw_ref, kvnw_ref, wq_ref, wkv_ref,
                     qn_ref, kn_ref, oq_ref, ok_ref, ov_ref):
    cos, sin = cos_ref[...], sin_ref[...]
    scale = MLA_QK ** -0.5 * LOG2E
    qn, kn = qn_ref[...], kn_ref[...]
    q = _dg(_rms(cq_ref[...], qnw_ref[...], MLA_Q_RANK).astype(BF16), wq_ref[...])
    kv = _dg(_rms(ckv_ref[...], kvnw_ref[...], MLA_KV_RANK).astype(BF16), wkv_ref[...])
    lane = lax.broadcasted_iota(jnp.int32, sm_ref.shape, 1)
    kr = jnp.where(lane < MLA_ROPE, sm_ref[...], 0.0)
    kr_ss = jnp.sum(kr * kr, axis=-1, keepdims=True)
    for h in range(MLA_HEADS):
        lo = slice(2 * h * HEAD, (2 * h + 1) * HEAD)
        hi = slice((2 * h + 1) * HEAD, (2 * h + 2) * HEAD)
        q_nope, q_rope = q[:, lo], q[:, hi]
        r = lax.rsqrt((jnp.sum(q_nope * q_nope, axis=-1, keepdims=True)
                       + jnp.sum(q_rope * q_rope, axis=-1, keepdims=True)) * (1.0 / MLA_QK) + NORM_EPS)
        oq_ref[:, lo] = (q_nope * r * qn[:, :HEAD] * scale).astype(BF16)
        oq_ref[:, hi] = (_rope(q_rope * r * qn[:, HEAD:], cos, sin, 16) * scale).astype(BF16)
        k_nope, v = kv[:, lo], kv[:, hi]
        r = lax.rsqrt((jnp.sum(k_nope * k_nope, axis=-1, keepdims=True) + kr_ss) * (1.0 / MLA_QK) + NORM_EPS)
        ok_ref[:, lo] = (k_nope * r * kn[:, :HEAD]).astype(BF16)
        ok_ref[:, hi] = _rope(kr * r * kn[:, HEAD:], cos, sin, 16).astype(BF16)
        ov_ref[:, h * HEAD:(h + 1) * HEAD] = v.astype(BF16)


def _mla_prep(p, cos, sin, q_norm_w, kv_norm_w, wq, wkv, qn, kn):
    t = p.shape[0]
    tb = _row_tile(t)
    w2 = MLA_HEADS * 2 * HEAD
    row = lambda w, c: pl.BlockSpec((tb, w), lambda i: (i, c))
    full = lambda a: pl.BlockSpec(a.shape, lambda i: (0, 0))
    args = (q_norm_w.reshape(1, -1), kv_norm_w.reshape(1, -1), wq, wkv, qn, kn)
    return pl.pallas_call(
        _mla_prep_kernel,
        grid=(t // tb,),
        in_specs=[row(MLA_Q_RANK, P_DCQ // MLA_Q_RANK), row(MLA_KV_RANK, P_DCKV // MLA_KV_RANK),
                  row(HEAD, P_SM // HEAD), row(HEAD, 0), row(HEAD, 0)] + [full(a) for a in args],
        out_specs=[row(w2, 0), row(w2, 0), row(MLA_HEADS * HEAD, 0)],
        out_shape=[jax.ShapeDtypeStruct((t, w2), BF16), jax.ShapeDtypeStruct((t, w2), BF16),
                   jax.ShapeDtypeStruct((t, MLA_HEADS * HEAD), BF16)],
        compiler_params=_params("parallel"),
        name="mla_prep",
    )(p, p, p, cos, sin, *args)


def _lru_kernel(ux_ref, z_ref, cw_ref, cb_ref, w_ref, b_ref, lam_ref, o_ref,
                af_ref, ab_ref, hf_ref, hb_ref, cin_ref, *, lat):
    t = ux_ref.shape[0]
    ctx = t - lat
    nb, nb_lat = t // TOK, lat // TOK
    cw, cb = cw_ref[...], cb_ref[...]
    w_split = _split2(w_ref[0])
    bias = b_ref[0]
    sp = (_softplus(-lam_ref[0:1, :]), _softplus(-lam_ref[1:2, :]))
    dirs = ((af_ref, hf_ref), (ab_ref, hb_ref))

    def gates_block(b, _):
        t0 = pl.multiple_of(b * TOK, TOK)
        rows = pl.ds(t0, TOK)
        first = jnp.logical_or(b == 0, b == nb_lat)
        last = jnp.logical_or(b == nb_lat - 1, b == nb - 1)
        prev = ux_ref[pl.ds(pl.multiple_of(jnp.maximum(t0 - 8, 0), 8), 8), :]
        nxt = ux_ref[pl.ds(pl.multiple_of(jnp.minimum(t0 + TOK, t - 8), 8), 8), :]
        xe = jnp.concatenate([jnp.where(first, 0.0, prev), ux_ref[rows, :], jnp.where(last, 0.0, nxt)], axis=0)
        xs = cb + xe[6:6 + TOK] * cw[0:1]
        for j in range(1, 4):
            xs = xs + xe[6 + j:6 + j + TOK] * cw[j:j + 1]
        g = _dot3s(_split2(xs), w_split) + bias
        for d, (a_ref, h_ref) in enumerate(dirs):
            r = _sigmoid(g[:, 2 * d * HEAD:(2 * d + 1) * HEAD])
            gi = _sigmoid(g[:, (2 * d + 1) * HEAD:(2 * d + 2) * HEAD])
            a = jnp.exp(-LRU_C * r * sp[d])
            a_ref[rows, :] = a
            h_ref[rows, :] = jnp.sqrt(1.0 - a * a) * (gi * xs)
        return 0

    lax.fori_loop(0, nb, gates_block, 0)

    def scan_region(r0, length, h0):
        ls = length // 8

        def step(s, carry):
            out = []
            for (a_ref, h_ref), idx, (h, ac) in zip(dirs, (s, ls - 1 - s), carry):
                rows = pl.ds(r0 + idx, 8, stride=ls)
                a = a_ref[rows, :]
                h = a * h + h_ref[rows, :]
                ac = ac * a
                h_ref[rows, :] = h
                a_ref[rows, :] = ac
                out.append((h, ac))
            return tuple(out)

        init = (jnp.zeros((8, HEAD), F32), jnp.ones((8, HEAD), F32))
        local = lax.fori_loop(0, ls, step, (init, init), unroll=4)
        carry_in, final = [], []
        for d, order in enumerate((range(8), range(7, -1, -1))):
            hk, ak = local[d]
            cin = [None] * 8
            c = h0[d]
            for k in order:
                cin[k] = c
                c = ak[k:k + 1] * c + hk[k:k + 1]
            carry_in.append(jnp.concatenate(cin, axis=0))
            final.append(c)
        return carry_in, final

    def emit(rows, cf, cb_):
        h = (hf_ref[rows, :] + af_ref[rows, :] * cf) + (hb_ref[rows, :] + ab_ref[rows, :] * cb_)
        o_ref[rows, :] = (h * _silu(z_ref[rows, :])).astype(BF16)

    zero = jnp.zeros((1, HEAD), F32)
    cin_ctx, s_ctx = scan_region(lat, ctx, (zero, zero))
    for k in range(8):
        emit(pl.ds(lat + k * (ctx // 8), ctx // 8), cin_ctx[0][k:k + 1], cin_ctx[1][k:k + 1])
    cin_lat, _ = scan_region(0, lat, s_ctx)
    cin_ref[0:8, :] = cin_lat[0]
    cin_ref[8:16, :] = cin_lat[1]
    blocks_per_segment = lat // 8 // TOK

    def out_block(b, _):
        k = b // blocks_per_segment
        emit(pl.ds(pl.multiple_of(b * TOK, TOK), TOK), cin_ref[pl.ds(k, 1), :], cin_ref[pl.ds(8 + k, 1), :])
        return 0

    lax.fori_loop(0, nb_lat, out_block, 0)


def _lru(p, conv_w, conv_b, w_gates, b_gates, lam, lat):
    t = p.shape[0]
    col = lambda c0: pl.BlockSpec((t, HEAD), lambda n: (0, c0 // HEAD + n))
    vec = lambda r: pl.BlockSpec((r, HEAD), lambda n: (0, n))
    return pl.pallas_call(
        functools.partial(_lru_kernel, lat=lat),
        grid=(LRU_BLOCKS,),
        in_specs=[col(P_BX), col(P_BZ), vec(4), vec(1),
                  pl.BlockSpec((1, HEAD, 4 * HEAD), lambda n: (n, 0, 0)),
                  pl.BlockSpec((1, 1, 4 * HEAD), lambda n: (n, 0, 0)), vec(2)],
        out_specs=pl.BlockSpec((t, HEAD), lambda n: (0, n)),
        out_shape=jax.ShapeDtypeStruct((t, LRU_WIDTH), BF16),
        scratch_shapes=[pltpu.VMEM((t, HEAD), F32)] * 4 + [pltpu.VMEM((16, HEAD), F32)],
        compiler_params=_params("parallel"),
        name="lru",
    )(p, p, conv_w, conv_b.reshape(1, -1), w_gates, b_gates, lam)


def _dn_prep_kernel(q_ref, k_ref, v_ref, qp_ref, kp_ref, vp_ref, qx_ref, kx_ref, vx_ref, cw_ref, sm_ref,
                    alog_ref, dtb_ref, u_ref, w_ref, qd_ref, kd_ref, qk_ref, gl_ref, *, nb_lat):
    i = pl.program_id(0)
    tb = q_ref.shape[0]
    cc = DN_CHUNK
    first = jnp.logical_or(i == 0, i == nb_lat)
    last = jnp.logical_or(i == nb_lat - 1, i == nb_lat)
    cw = cw_ref[...]

    def conv_silu(x_ref, prev_ref, next_ref, c0):
        xe = jnp.concatenate([jnp.where(first, 0.0, prev_ref[...]), x_ref[...],
                              jnp.where(last, 0.0, next_ref[...])], axis=0)
        y = xe[6:6 + tb] * cw[0:1, c0:c0 + MIX]
        for j in range(1, 4):
            y = y + xe[6 + j:6 + j + tb] * cw[j:j + 1, c0:c0 + MIX]
        return _silu(y)

    q = conv_silu(q_ref, qp_ref, qx_ref, 0)
    k = conv_silu(k_ref, kp_ref, kx_ref, MIX)
    v = conv_silu(v_ref, vp_ref, vx_ref, 2 * MIX)

    sm = sm_ref[...]
    beta_all = _sigmoid(sm)
    g_all = -jnp.exp(alog_ref[...]) * _softplus(sm + dtb_ref[...])

    r = lax.broadcasted_iota(jnp.int32, (tb, tb), 0)
    c = lax.broadcasted_iota(jnp.int32, (tb, tb), 1)
    same = (r // cc) == (c // cc)
    tri_f = jnp.where(same, jnp.where(c <= r, 1.0, 0.0), 0.0).astype(BF16)
    tri_b = jnp.where(same, jnp.where(c >= r, 1.0, 0.0), 0.0).astype(BF16)
    gcs = (_dot_exact_lhs(tri_f, g_all), _dot_exact_lhs(tri_b, g_all))

    ii = lax.broadcasted_iota(jnp.int32, (cc, cc), 0)
    jj = lax.broadcasted_iota(jnp.int32, (cc, cc), 1)
    eye = jnp.where(ii == jj, 1.0, 0.0)
    incl = (ii >= jj, ii <= jj)
    strict = (ii > jj, ii < jj)

    gl_ref[...] = jnp.zeros_like(gl_ref)
    qs, ks = [], []
    for h in range(DN_HEADS):
        hs = slice(h * HEAD, (h + 1) * HEAD)
        qh, kh = q[:, hs], k[:, hs]
        qs.append(qh * lax.rsqrt(jnp.sum(qh * qh, axis=-1, keepdims=True) + NORM_EPS) * (HEAD ** -0.5))
        ks.append(kh * lax.rsqrt(jnp.sum(kh * kh, axis=-1, keepdims=True) + NORM_EPS))

    for ch in range(tb // cc):
        rows = slice(ch * cc, (ch + 1) * cc)
        gc = [gcs[d][rows] for d in range(2)]
        gct = [g.T for g in gc]
        kk, qk0 = [], []
        for h in range(DN_HEADS):
            k_split = _split2(ks[h][rows])
            kk.append(_dot3s(k_split, k_split, NT))
            qk0.append(_dot3s(_split2(qs[h][rows]), k_split, NT))
        probs = []
        for d in range(2):
            end = cc - 1 if d == 0 else 0
            for h in range(DN_HEADS):
                hs = slice(h * HEAD, (h + 1) * HEAD)
                lb = SM_AB + d * 2 * DN_HEADS + h
                lg = lb + DN_HEADS
                beta = beta_all[rows, lb:lb + 1]
                gcol, grow = gc[d][:, lg:lg + 1], gct[d][lg:lg + 1, :]
                glast = gc[d][end:end + 1, lg:lg + 1]
                qh, kh, vh = qs[h][rows], ks[h][rows], v[rows, hs]
                decay = jnp.where(incl[d], jnp.exp(jnp.where(incl[d], gcol - grow, 0.0)), 0.0)
                eg = jnp.exp(gcol)
                kb = kh * beta
                lm = jnp.where(strict[d], beta * kk[h] * decay, 0.0)
                qk_ref[d, h, rows, :] = qk0[h] * decay
                qd_ref[d, rows, hs] = qh * eg
                kd_ref[d, rows, hs] = kh * jnp.exp(glast - gcol)
                gl_ref[d, ch, h:h + 1, :] = jnp.broadcast_to(jnp.exp(glast), (1, HEAD))
                probs.append((d, hs, lm, jnp.concatenate([vh * beta, kb * eg], axis=1)))
        xs = [eye - lm for _, _, lm, _ in probs]
        pw = [_split2(lm) for _, _, lm, _ in probs]
        pw = [_dot3s(s, s) for s in pw]
        for _ in range(4):
            pw = [_split2(m) for m in pw]
            xs = [x + _dot3s(_split2(x), s) for x, s in zip(xs, pw)]
            pw = [_dot3s(s, s) for s in pw]
        pw = [_split2(m) for m in pw]
        xs = [x + _dot3s(_split2(x), s) for x, s in zip(xs, pw)]
        sols = [_dot3(x, rhs) for x, (_, _, _, rhs) in zip(xs, probs)]
        for sol, (d, hs, _, _) in zip(sols, probs):
            u_ref[d, rows, hs] = sol[:, :HEAD]
            w_ref[d, rows, hs] = sol[:, HEAD:]


def _dn_prep(p, conv_w, alog_row, dtb_row, lat):
    t = p.shape[0]
    tb = TOK
    nb, nb_lat = t // tb, lat // tb
    r8 = tb // 8
    blk = lambda c0: pl.BlockSpec((tb, MIX), lambda i: (i, c0 // MIX))
    prev = lambda c0: pl.BlockSpec((8, MIX), lambda i: (jnp.maximum(i * r8 - 1, 0), c0 // MIX))
    nxt = lambda c0: pl.BlockSpec((8, MIX), lambda i: (jnp.minimum((i + 1) * r8, t // 8 - 1), c0 // MIX))
    cols = (P_CQ, P_CK, P_CV)
    big = pl.BlockSpec((2, tb, MIX), lambda i: (0, i, 0))
    big_shape = jax.ShapeDtypeStruct((2, t, MIX), F32)
    return pl.pallas_call(
        functools.partial(_dn_prep_kernel, nb_lat=nb_lat),
        grid=(nb,),
        in_specs=[blk(c0) for c0 in cols] + [prev(c0) for c0 in cols] + [nxt(c0) for c0 in cols] + [
            pl.BlockSpec((4, 3 * MIX), lambda i: (0, 0)),
            pl.BlockSpec((tb, HEAD), lambda i: (i, P_SM // HEAD)),
            pl.BlockSpec((1, HEAD), lambda i: (0, 0)),
            pl.BlockSpec((1, HEAD), lambda i: (0, 0))],
        out_specs=[big, big, big, big,
                   pl.BlockSpec((2, DN_HEADS, tb, DN_CHUNK), lambda i: (0, 0, i, 0)),
                   pl.BlockSpec((2, tb // DN_CHUNK, 8, HEAD), lambda i: (0, i, 0, 0))],
        out_shape=[big_shape, big_shape, big_shape, big_shape,
                   jax.ShapeDtypeStruct((2, DN_HEADS, t, DN_CHUNK), F32),
                   jax.ShapeDtypeStruct((2, t // DN_CHUNK, 8, HEAD), F32)],
        compiler_params=_params("parallel"),
        name="dn_prep",
    )(*([p] * 9), conv_w, p, alog_row, dtb_row)


def _dn_scan_kernel(uf, wf, qdf, kdf, qkf, glf, ub, wb, qdb, kdb, qkb, glb, of_ref, ob_ref, s_ref):
    @pl.when(pl.program_id(0) == 0)
    def _():
        s_ref[...] = jnp.zeros_like(s_ref)

    cc = DN_CHUNK
    dirs = ((uf, wf, qdf, kdf, qkf, glf, of_ref), (ub, wb, qdb, kdb, qkb, glb, ob_ref))
    probs = [(d, h, slice(h * HEAD, (h + 1) * HEAD)) for d in range(2) for h in range(DN_HEADS)]
    ws = [_dot3(jnp.concatenate([dirs[d][1][0, :, hs], dirs[d][2][0, :, hs]], axis=0), s_ref[d, h])
          for d, h, hs in probs]
    v_new = [_split2(dirs[d][0][0, :, hs] - t[:cc]) for t, (d, h, hs) in zip(ws, probs)]
    outs = [t[cc:] + _dot3s(_split2(dirs[d][4][0, h]), vn) for t, vn, (d, h, hs) in zip(ws, v_new, probs)]
    upd = [_dot3s(_split2(dirs[d][3][0, :, hs]), vn, TN) for vn, (d, h, hs) in zip(v_new, probs)]
    for o, ds, (d, h, hs) in zip(outs, upd, probs):
        dirs[d][6][:, hs] = o
        s_ref[d, h] = s_ref[d, h] * dirs[d][5][0, 0, h:h + 1, :] + ds


def _dn_scan(u, w, qd, kd, qk, gl, lat):
    t = u.shape[1]
    cc = DN_CHUNK
    n, n_lat = t // cc, lat // cc
    n_ctx = n - n_lat
    cf = lambda i: jnp.where(i < n_ctx, n_lat + i, i - n_ctx)
    cb = lambda i: n - 1 - i
    specs = []
    for d, ch in ((0, cf), (1, cb)):
        big = pl.BlockSpec((1, cc, MIX), lambda i, d=d, ch=ch: (d, ch(i), 0))
        specs += [big, big, big, big,
                  pl.BlockSpec((1, DN_HEADS, cc, cc), lambda i, d=d, ch=ch: (d, 0, ch(i), 0)),
                  pl.BlockSpec((1, 1, 8, HEAD), lambda i, d=d, ch=ch: (d, ch(i), 0, 0))]
    return pl.pallas_call(
        _dn_scan_kernel,
        grid=(n,),
        in_specs=specs,
        out_specs=[pl.BlockSpec((cc, MIX), lambda i: (cf(i), 0)), pl.BlockSpec((cc, MIX), lambda i: (cb(i), 0))],
        out_shape=[jax.ShapeDtypeStruct((t, MIX), F32)] * 2,
        scratch_shapes=[pltpu.VMEM((2, DN_HEADS, HEAD, HEAD), F32)],
        compiler_params=_params("arbitrary"),
        name="dn_scan",
    )(u, w, qd, kd, qk, gl, u, w, qd, kd, qk, gl)


def _dn_out_kernel(of_ref, ob_ref, z_ref, nw_ref, o_ref):
    for h in range(DN_HEADS):
        hs = slice(h * HEAD, (h + 1) * HEAD)
        o = of_ref[:, hs] + ob_ref[:, hs]
        o_ref[:, hs] = (_rms(o, nw_ref[...], HEAD) * _silu(z_ref[:, hs])).astype(BF16)


def _dn_out(o_f, o_b, p, norm_w):
    t = p.shape[0]
    tb = _row_tile(t)
    row = lambda c: pl.BlockSpec((tb, MIX), lambda i: (i, c))
    return pl.pallas_call(
        _dn_out_kernel,
        grid=(t // tb,),
        in_specs=[row(0), row(0), row(P_CZ // MIX), pl.BlockSpec((1, HEAD), lambda i: (0, 0))],
        out_specs=row(0),
        out_shape=jax.ShapeDtypeStruct((t, MIX), BF16),
        compiler_params=_params("parallel"),
        name="dn_out",
    )(o_f, o_b, p, norm_w.reshape(1, HEAD))


def _arrange_w_in(w_in):
    pad = jnp.zeros(w_in.shape[:2] + (P_W - w_in.shape[2],), w_in.dtype)
    parts = [w_in[..., 0:4608],
             w_in[..., 5328:5840],
             w_in[..., 5008:5264],
             w_in[..., 4624:5008],
             w_in[..., 5264:5328],
             w_in[..., 4608:4624],
             pad]
    return jnp.concatenate(parts, axis=-1).astype(BF16)


def _arrange_mla_wq(w_uq):
    l, r, _ = w_uq.shape
    w = w_uq.reshape(l, r, MLA_HEADS, MLA_QK)
    w = jnp.pad(w, ((0, 0), (0, 0), (0, 0), (0, 2 * HEAD - MLA_QK)))
    return w.reshape(l, r, MLA_HEADS * 2 * HEAD).astype(BF16)


def _pad_qk_norm(w):
    return jnp.pad(w, ((0, 0), (0, 2 * HEAD - MLA_QK)))[:, None, :]


def _small_lane_row(vals):
    l = vals.shape[0]
    row = jnp.zeros((l, HEAD), F32)
    for d in range(2):
        lo = SM_AB + d * 2 * DN_HEADS + DN_HEADS
        row = row.at[:, lo:lo + DN_HEADS].set(vals[:, d, :])
    return row[:, None, :]


def _arrange_lru_gates(w_a, b_a, w_x, b_x):
    w = jnp.concatenate([w_a[:, 0], w_x[:, 0], w_a[:, 1], w_x[:, 1]], axis=-1)
    l = b_a.shape[0]
    blk = lambda b, d: b[:, d].reshape(l, LRU_BLOCKS, 1, HEAD)
    b = jnp.concatenate([blk(b_a, 0), blk(b_x, 0), blk(b_a, 1), blk(b_x, 1)], axis=-1)
    return w, b


def _rope_tables(lat, ctx):
    t = jnp.arange(lat)
    rows, cols = (t // GRID_W).astype(F32), (t % GRID_W).astype(F32)
    lane = jnp.arange(HEAD)

    def table(half, width):
        inv_freq = ROPE_THETA ** (-jnp.arange(half, dtype=F32) / half)
        pos = jnp.where((lane // (2 * half))[None, :] == 0, rows[:, None], cols[:, None])
        ang = pos * inv_freq[lane % half][None, :]
        live = (lane < width)[None, :]
        cos = jnp.where(live, jnp.cos(ang), 0.0)
        sin = jnp.where(live, jnp.sin(ang), 0.0) * jnp.where((lane % (2 * half)) < half, -1.0, 1.0)[None, :]
        cos_c = jnp.broadcast_to(jnp.where(live, 1.0, 0.0), (ctx, HEAD))
        return (jnp.concatenate([cos, cos_c], axis=0).astype(F32),
                jnp.concatenate([sin, jnp.zeros((ctx, HEAD), F32)], axis=0).astype(F32))

    return table(32, HEAD), table(16, MLA_ROPE)


def kernel(x, c, ctx, c_ctx, norm_w, w_ada, b_ada, w_in, w_out, attn_q_norm, attn_k_norm, lru_conv_w, lru_conv_b, lru_w_a, lru_b_a, lru_w_x, lru_b_x, lru_lambda, dn_conv_w, dn_a_log, dn_dt_bias, dn_norm_w, mla_q_norm, mla_kv_norm, mla_w_uq, mla_w_ukv, mla_q_qk_norm, mla_k_qk_norm):
    assert x.shape[0] == 1 and ctx.shape[1] == TOK and x.shape[1] % (8 * TOK) == 0
    lat, n_ctx = x.shape[1], ctx.shape[1]
    depth = w_in.shape[0]

    xs = jnp.concatenate([x[0], ctx[0]], axis=0)
    mod = _ada(jnp.stack([c[0], c_ctx], axis=1), w_ada, b_ada)
    (cos_a, sin_a), (cos_m, sin_m) = _rope_tables(lat, n_ctx)
    w_in_r = _arrange_w_in(w_in)
    w_out_r = w_out.reshape(depth, 4, MIX, D_MODEL).astype(BF16)
    wq_r = _arrange_mla_wq(mla_w_uq)
    wkv_r = mla_w_ukv.astype(BF16)
    qn_r, kn_r = _pad_qk_norm(mla_q_qk_norm), _pad_qk_norm(mla_k_qk_norm)
    alog_r, dtb_r = _small_lane_row(dn_a_log), _small_lane_row(dn_dt_bias)
    lru_w, lru_b = _arrange_lru_gates(lru_w_a, lru_b_a, lru_w_x, lru_b_x)

    for l in range(depth):
        p = _inproj(xs, norm_w[l], mod[l, 0], mod[l, 1], w_in_r[l], lat)
        qa, ka, va = _gqa_prep(p, cos_a, sin_a, attn_q_norm[l], attn_k_norm[l])
        bound_a = jnp.max(jnp.abs(attn_q_norm[l])) * jnp.max(jnp.abs(attn_k_norm[l])) * (HEAD ** 0.5 * LOG2E)
        y_a = _flash(qa, ka, va, p, P_AZ, A_HEADS, A_KV_HEADS, HEAD, lat, bound_a, "gqa_attn")
        y_b = _lru(p, lru_conv_w[l], lru_conv_b[l], lru_w[l], lru_b[l], lru_lambda[l], lat)
        u, w, qd, kd, qk, gl = _dn_prep(p, dn_conv_w[l], alog_r[l], dtb_r[l], lat)
        o_f, o_b = _dn_scan(u, w, qd, kd, qk, gl, lat)
        y_c = _dn_out(o_f, o_b, p, dn_norm_w[l])
        qm, km, vm = _mla_prep(p, cos_m, sin_m, mla_q_norm[l], mla_kv_norm[l], wq_r[l], wkv_r[l],
                               qn_r[l], kn_r[l])
        bound_d = (jnp.max(jnp.abs(mla_q_qk_norm[l])) * jnp.max(jnp.abs(mla_k_qk_norm[l]))
                   * (MLA_QK ** 0.5 * LOG2E))
        y_d = _flash(qm, km, vm, p, P_DZ, MLA_HEADS, MLA_HEADS, 2 * HEAD, lat, bound_d, "mla_attn")
        last = l == depth - 1
        xs = _outproj((y_a, y_b, y_c, y_d), w_out_r[l], xs, mod[l, 2], lat, lat if last else lat + n_ctx)
    return xs[None]
```

```python
import functools
import math

import jax
import jax.numpy as jnp
from jax import lax
from jax.experimental import pallas as pl
from jax.experimental.pallas import tpu as pltpu

F32 = jnp.float32
BF16 = jnp.bfloat16

D_MODEL = 2048
DEPTH = 4
GRID_W = 64
ROPE_THETA = 10000.0
NORM_EPS = 1e-6
HEAD = 128
A_HEADS, A_KV_HEADS = 4, 2
LRU_WIDTH, LRU_BLOCKS, LRU_C = 512, 4, 8.0
DN_HEADS, DN_CHUNK = 4, 64
MLA_HEADS, MLA_Q_RANK, MLA_KV_RANK, MLA_NOPE, MLA_ROPE = 4, 384, 256, 128, 64
MLA_QK = MLA_NOPE + MLA_ROPE
MIX = 512

P_AQ, P_AK, P_AV, P_AZ = 0, 512, 768, 1024
P_BX, P_BZ = 1536, 2048
P_CQ, P_CK, P_CV, P_CZ = 2560, 3072, 3584, 4096
P_DZ, P_DCKV, P_DCQ = 4608, 5120, 5376
P_SM = 5760
P_W = 6144
SM_AB = MLA_ROPE

TOK = 256
HALO = 16
VMEM_LIMIT = 56 * 1024 * 1024
LOG2E = math.log2(math.e)
MAX_EXP2_LOGIT = 60.0

NN = ((1,), (0,))
NT = ((1,), (1,))
TN = ((0,), (0,))


def _dg(a, b, dims=NN):
    return lax.dot_general(a, b, (dims, ((), ())), preferred_element_type=F32)


def _split2(a):
    hi = a.astype(BF16)
    lo = (a - hi.astype(F32)).astype(BF16)
    return hi, lo


def _dot3s(a, b, dims=NN):
    return _dg(a[0], b[0], dims) + (_dg(a[0], b[1], dims) + _dg(a[1], b[0], dims))


def _dot3(a, b, dims=NN):
    return _dot3s(_split2(a), _split2(b), dims)


def _dot_exact_lhs(m, b):
    b0 = b.astype(BF16)
    r1 = b - b0.astype(F32)
    b1 = r1.astype(BF16)
    b2 = (r1 - b1.astype(F32)).astype(BF16)
    return _dg(m, b0) + (_dg(m, b1) + _dg(m, b2))


def _sigmoid(x):
    return 1.0 / (1.0 + jnp.exp(-x))


def _silu(x):
    return x * _sigmoid(x)


def _softplus(x):
    return jnp.maximum(x, 0.0) + jnp.log(1.0 + jnp.exp(-jnp.abs(x)))


def _rms(x, w, n):
    return x * lax.rsqrt(jnp.sum(x * x, axis=-1, keepdims=True) * (1.0 / n) + NORM_EPS) * w


def _rope(x, cos, sin_signed, half):
    lane = lax.broadcasted_iota(jnp.int32, x.shape, 1)
    first = (lane % (2 * half)) < half
    rot = jnp.where(first, pltpu.roll(x, HEAD - half, 1), pltpu.roll(x, half, 1))
    return x * cos + rot * sin_signed


def _params(*sem):
    return pltpu.CompilerParams(dimension_semantics=sem, vmem_limit_bytes=VMEM_LIMIT)


def _row_tile(rows):
    for t in (768, 512, 256):
        if rows % t == 0:
            return t
    raise ValueError(f"row count {rows} is not a multiple of {TOK}")


def _ada_kernel(c_ref, w_ref, b_ref, o_ref):
    s = _silu(c_ref[...])
    w = w_ref[0]
    r0 = jnp.sum(s[:, 0:1] * w, axis=0, keepdims=True)
    r1 = jnp.sum(s[:, 1:2] * w, axis=0, keepdims=True)
    o_ref[0, 0] = jnp.concatenate([r0, r1], axis=0) + b_ref[0]


def _ada(c_cols, w_ada, b_ada):
    depth, d, _ = w_ada.shape
    tn = 512
    per = d // tn
    return pl.pallas_call(
        _ada_kernel,
        grid=(depth, 3 * per),
        in_specs=[
            pl.BlockSpec((d, 2), lambda l, j: (0, 0)),
            pl.BlockSpec((1, d, tn), lambda l, j: (l, 0, j)),
            pl.BlockSpec((1, 1, tn), lambda l, j: (l, 0, j)),
        ],
        out_specs=pl.BlockSpec((1, 1, 2, tn), lambda l, j: (l, j // per, 0, j % per)),
        out_shape=jax.ShapeDtypeStruct((depth, 3, 2, d), F32),
        compiler_params=_params("parallel", "parallel"),
        name="ada",
    )(c_cols, w_ada, b_ada.reshape(depth, 1, 3 * d))


def _inproj_kernel(x_ref, nw_ref, shift_ref, scale_ref, w_ref, o_ref, h_ref, *, lat):
    i = pl.program_id(0)
    tm = x_ref.shape[0]

    @pl.when(pl.program_id(1) == 0)
    def _():
        rc = 32

        def chunk(ci, _):
            rows = pl.ds(pl.multiple_of(ci * rc, rc), rc)
            y = _rms(x_ref[rows, :], nw_ref[...], D_MODEL)
            is_ctx = i * tm + ci * rc + lax.broadcasted_iota(jnp.int32, (rc, 1), 0) >= lat
            scale = jnp.where(is_ctx, scale_ref[1:2, :], scale_ref[0:1, :])
            shift = jnp.where(is_ctx, shift_ref[1:2, :], shift_ref[0:1, :])
            h_ref[rows, :] = (y * (1.0 + scale) + shift).astype(BF16)
            return 0

        lax.fori_loop(0, tm // rc, chunk, 0)

    o_ref[...] = _dg(h_ref[...], w_ref[...]).astype(o_ref.dtype)


def _inproj(xs, norm_w, shift, scale, w_in, layer, lat):
    t, d = xs.shape
    tm, tn = (1408 if t % 1408 == 0 else _row_tile(t)), 768
    return pl.pallas_call(
        functools.partial(_inproj_kernel, lat=lat),
        grid=(t // tm, P_W // tn),
        in_specs=[
            pl.BlockSpec((tm, d), lambda i, j: (i, 0)),
            pl.BlockSpec((1, d), lambda i, j: (0, 0)),
            pl.BlockSpec((2, d), lambda i, j: (0, 0)),
            pl.BlockSpec((2, d), lambda i, j: (0, 0)),
            pl.BlockSpec((None, d, tn), lambda i, j: (layer, 0, j)),
        ],
        out_specs=pl.BlockSpec((tm, tn), lambda i, j: (i, j)),
        out_shape=jax.ShapeDtypeStruct((t, P_W), BF16),
        scratch_shapes=[pltpu.VMEM((tm, d), BF16)],
        compiler_params=_params("parallel", "arbitrary"),
        name="inproj",
    )(xs, norm_w.reshape(1, d), shift, scale, w_in)


def _outproj_kernel(ya_ref, yb_ref, yc_ref, yd_ref, w_ref, x_ref, g_ref, o_ref, *, lat):
    i = pl.program_id(0)
    tm = x_ref.shape[0]
    acc = _dg(ya_ref[...], w_ref[0])
    acc += _dg(yb_ref[...], w_ref[1])
    acc += _dg(yc_ref[...], w_ref[2])
    acc += _dg(yd_ref[...], w_ref[3])
    row = i * tm + lax.broadcasted_iota(jnp.int32, (tm, 1), 0)
    gate = jnp.where(row >= lat, g_ref[1:2, :], g_ref[0:1, :])
    o_ref[...] = x_ref[...] + gate * acc


def _outproj(ys, w_out, layer, xs, gate, lat, out_rows):
    d = xs.shape[1]
    tm, tn = _row_tile(out_rows), 1024
    yspec = pl.BlockSpec((tm, MIX), lambda i, j: (i, 0))
    return pl.pallas_call(
        functools.partial(_outproj_kernel, lat=lat),
        grid=(out_rows // tm, d // tn),
        in_specs=[yspec, yspec, yspec, yspec,
                  pl.BlockSpec((None, 4, MIX, tn), lambda i, j: (layer, 0, 0, j)),
                  pl.BlockSpec((tm, tn), lambda i, j: (i, j)),
                  pl.BlockSpec((2, tn), lambda i, j: (0, j))],
        out_specs=pl.BlockSpec((tm, tn), lambda i, j: (i, j)),
        out_shape=jax.ShapeDtypeStruct((out_rows, d), F32),
        compiler_params=_params("parallel", "parallel"),
        name="outproj",
    )(*ys, w_out, xs, gate)


def _flash_kernel(q_ref, k_ref, v_ref, z_ref, o_ref, *, lat, tk, dq, shared_kv, bounded):
    tq = q_ref.shape[0]
    ctx = k_ref.shape[0] - lat
    nh = q_ref.shape[1] // dq
    is_ctx_tile = pl.program_id(1) == pl.num_programs(1) - 1
    ctx_rows = pl.ds(lat, ctx)

    def scores(c, rows):
        kc = 0 if shared_kv else c
        s = _dg(q_ref[:, c * dq:(c + 1) * dq], k_ref[rows, kc * dq:(kc + 1) * dq], NT)
        return s, v_ref[rows, kc * HEAD:(kc + 1) * HEAD]

    def attend_bounded(c, carry, rows):
        l, acc = carry
        s, v = scores(c, rows)
        p = jnp.exp2(s)
        for j in range(s.shape[1] // HEAD):
            l = l + p[:, j * HEAD:(j + 1) * HEAD]
        return l, acc + _dg(p.astype(BF16), v)

    def attend_online(c, carry, rows):
        m, l, acc = carry
        s, v = scores(c, rows)
        m_new = jnp.maximum(m, jnp.max(s, axis=-1, keepdims=True))
        alpha = jnp.exp2(m - m_new)
        p = jnp.exp2(s - m_new)
        l = alpha * l + jnp.sum(p, axis=-1, keepdims=True)
        return m_new, l, alpha * acc + _dg(p.astype(BF16), v)

    def finish(c, carry):
        l, acc = carry[-2], carry[-1]
        cs = slice(c * HEAD, (c + 1) * HEAD)
        gate = _silu(z_ref[:, cs].astype(F32))
        o_ref[:, cs] = (acc / jnp.sum(l, axis=-1, keepdims=True) * gate).astype(o_ref.dtype)

    zeros = jnp.zeros((tq, HEAD), F32)
    if bounded:
        attend, init = attend_bounded, (zeros, zeros)
    else:
        attend, init = attend_online, (jnp.full((tq, 1), -1e30, F32), jnp.zeros((tq, 1), F32), zeros)

    @pl.when(jnp.logical_not(is_ctx_tile))
    def _():
        def step(kb, carries):
            rows = pl.ds(pl.multiple_of(kb * tk, tk), tk)
            return tuple(attend(c, carries[c], rows) for c in range(nh))

        carries = lax.fori_loop(0, lat // tk, step, (init,) * nh, unroll=2)
        for c in range(nh):
            finish(c, attend(c, carries[c], ctx_rows))

    @pl.when(is_ctx_tile)
    def _():
        for c in range(nh):
            finish(c, attend(c, init, ctx_rows))


def _flash(q, k, v, v_col, p, z_col, heads, kv_heads, dq, lat, logit_bound, name):
    t = q.shape[0]
    nh = 2
    shared_kv = heads // kv_heads == nh
    nk = 1 if shared_kv else nh
    tq = TOK
    tk = 1024 if lat % 2048 == 0 else TOK
    zb = z_col // (nh * HEAD)
    vb = v_col // (nk * HEAD)

    def call(bounded):
        return pl.pallas_call(
            functools.partial(_flash_kernel, lat=lat, tk=tk, dq=dq, shared_kv=shared_kv, bounded=bounded),
            grid=(heads // nh, t // tq),
            in_specs=[
                pl.BlockSpec((tq, nh * dq), lambda g, i: (i, g)),
                pl.BlockSpec((t, nk * dq), lambda g, i: (0, g)),
                pl.BlockSpec((t, nk * HEAD), lambda g, i: (0, vb + g)),
                pl.BlockSpec((tq, nh * HEAD), lambda g, i: (i, zb + g)),
            ],
            out_specs=pl.BlockSpec((tq, nh * HEAD), lambda g, i: (i, g)),
            out_shape=jax.ShapeDtypeStruct((t, heads * HEAD), BF16),
            compiler_params=_params("parallel", "parallel"),
            name=name + ("_bounded" if bounded else "_online"),
        )(q, k, v, p)

    return lax.cond(logit_bound <= MAX_EXP2_LOGIT, lambda: call(True), lambda: call(False))


def _gqa_prep_kernel(q_ref, k_ref, cos_ref, sin_ref, qn_ref, kn_ref, oq_ref, ok_ref):
    cos, sin = cos_ref[...], sin_ref[...]
    scale = HEAD ** -0.5 * LOG2E
    for h in range(A_HEADS):
        c = slice(h * HEAD, (h + 1) * HEAD)
        oq_ref[:, c] = (_rope(_rms(q_ref[:, c].astype(F32), qn_ref[...], HEAD), cos, sin, 32) * scale).astype(BF16)
    for g in range(A_KV_HEADS):
        c = slice(g * HEAD, (g + 1) * HEAD)
        ok_ref[:, c] = _rope(_rms(k_ref[:, c].astype(F32), kn_ref[...], HEAD), cos, sin, 32).astype(BF16)


def _gqa_prep(p, cos, sin, qn, kn):
    t = p.shape[0]
    tb = _row_tile(t)
    qw, kw = A_HEADS * HEAD, A_KV_HEADS * HEAD
    row = lambda w, c: pl.BlockSpec((tb, w), lambda i: (i, c))
    vec = pl.BlockSpec((1, HEAD), lambda i: (0, 0))
    return pl.pallas_call(
        _gqa_prep_kernel,
        grid=(t // tb,),
        in_specs=[row(qw, P_AQ // qw), row(kw, P_AK // kw), row(HEAD, 0), row(HEAD, 0), vec, vec],
        out_specs=[row(qw, 0), row(kw, 0)],
        out_shape=[jax.ShapeDtypeStruct((t, qw), BF16), jax.ShapeDtypeStruct((t, kw), BF16)],
        compiler_params=_params("parallel"),
        name="gqa_prep",
    )(p, p, cos, sin, qn.reshape(1, HEAD), kn.reshape(1, HEAD))


def _mla_prep_kernel(cq_ref, ckv_ref, sm_ref, cos_ref, sin_ref, qnw_ref, kvnw_ref, wq_ref, wkv_ref,
                     qn_ref, kn_ref, oq_ref, ok_ref, ov_ref):
    cos, sin = cos_ref[...], sin_ref[...]
    scale = MLA_QK ** -0.5 * LOG2E
    qn, kn = qn_ref[...], kn_ref[...]
    q = _dg(_rms(cq_ref[...].astype(F32), qnw_ref[...], MLA_Q_RANK).astype(BF16), wq_ref[...])
    kv = _dg(_rms(ckv_ref[...].astype(F32), kvnw_ref[...], MLA_KV_RANK).astype(BF16), wkv_ref[...])
    lane = lax.broadcasted_iota(jnp.int32, sm_ref.shape, 1)
    kr = jnp.where(lane < MLA_ROPE, sm_ref[...].astype(F32), 0.0)
    kr_ss = jnp.sum(kr * kr, axis=-1, keepdims=True)
    for h in range(MLA_HEADS):
        lo = slice(2 * h * HEAD, (2 * h + 1) * HEAD)
        hi = slice((2 * h + 1) * HEAD, (2 * h + 2) * HEAD)
        q_nope, q_rope = q[:, lo], q[:, hi]
        r = lax.rsqrt((jnp.sum(q_nope * q_nope, axis=-1, keepdims=True)
                       + jnp.sum(q_rope * q_rope, axis=-1, keepdims=True)) * (1.0 / MLA_QK) + NORM_EPS)
        oq_ref[:, lo] = (q_nope * r * qn[:, :HEAD] * scale).astype(BF16)
        oq_ref[:, hi] = (_rope(q_rope * r * qn[:, HEAD:], cos, sin, 16) * scale).astype(BF16)
        k_nope, v = kv[:, lo], kv[:, hi]
        r = lax.rsqrt((jnp.sum(k_nope * k_nope, axis=-1, keepdims=True) + kr_ss) * (1.0 / MLA_QK) + NORM_EPS)
        ok_ref[:, lo] = (k_nope * r * kn[:, :HEAD]).astype(BF16)
        ok_ref[:, hi] = _rope(kr * r * kn[:, HEAD:], cos, sin, 16).astype(BF16)
        ov_ref[:, h * HEAD:(h + 1) * HEAD] = v.astype(BF16)


def _mla_prep(p, cos, sin, q_norm_w, kv_norm_w, wq, wkv, qn, kn):
    t = p.shape[0]
    tb = _row_tile(t)
    w2 = MLA_HEADS * 2 * HEAD
    row = lambda w, c: pl.BlockSpec((tb, w), lambda i: (i, c))
    full = lambda a: pl.BlockSpec(a.shape, lambda i: (0, 0))
    args = (q_norm_w.reshape(1, -1), kv_norm_w.reshape(1, -1), wq, wkv, qn, kn)
    return pl.pallas_call(
        _mla_prep_kernel,
        grid=(t // tb,),
        in_specs=[row(MLA_Q_RANK, P_DCQ // MLA_Q_RANK), row(MLA_KV_RANK, P_DCKV // MLA_KV_RANK),
                  row(HEAD, P_SM // HEAD), row(HEAD, 0), row(HEAD, 0)] + [full(a) for a in args],
        out_specs=[row(w2, 0), row(w2, 0), row(MLA_HEADS * HEAD, 0)],
        out_shape=[jax.ShapeDtypeStruct((t, w2), BF16), jax.ShapeDtypeStruct((t, w2), BF16),
                   jax.ShapeDtypeStruct((t, MLA_HEADS * HEAD), BF16)],
        compiler_params=_params("parallel"),
        name="mla_prep",
    )(p, p, p, cos, sin, *args)


def _lru_kernel(ux_ref, z_ref, cw_ref, cb_ref, w_ref, b_ref, lam_ref, o_ref,
                af_ref, ab_ref, hf_ref, hb_ref, cin_ref, *, lat):
    t = ux_ref.shape[0]
    ctx = t - lat
    nb, nb_lat = t // TOK, lat // TOK
    cw, cb = cw_ref[...], cb_ref[...]
    bias = b_ref[0]
    sp = (_softplus(-lam_ref[0:1, :]), _softplus(-lam_ref[1:2, :]))
    dirs = ((af_ref, hf_ref), (ab_ref, hb_ref))

    def gates_block(b, _):
        t0 = pl.multiple_of(b * TOK, TOK)
        rows = pl.ds(t0, TOK)
        first = jnp.logical_or(b == 0, b == nb_lat)
        last = jnp.logical_or(b == nb_lat - 1, b == nb - 1)
        prev = ux_ref[pl.ds(pl.multiple_of(jnp.maximum(t0 - HALO, 0), HALO), HALO), :].astype(F32)
        nxt = ux_ref[pl.ds(pl.multiple_of(jnp.minimum(t0 + TOK, t - HALO), HALO), HALO), :].astype(F32)
        xe = jnp.concatenate([jnp.where(first, 0.0, prev), ux_ref[rows, :].astype(F32),
                              jnp.where(last, 0.0, nxt)], axis=0)
        xs = cb + xe[HALO - 2:HALO - 2 + TOK] * cw[0:1]
        for j in range(1, 4):
            xs = xs + xe[HALO - 2 + j:HALO - 2 + j + TOK] * cw[j:j + 1]
        g = _dg(xs.astype(BF16), w_ref[0]) + bias
        for d, (a_ref, h_ref) in enumerate(dirs):
            r = _sigmoid(g[:, 2 * d * HEAD:(2 * d + 1) * HEAD])
            gi = _sigmoid(g[:, (2 * d + 1) * HEAD:(2 * d + 2) * HEAD])
            a = jnp.exp(-LRU_C * r * sp[d])
            a_ref[rows, :] = a
            h_ref[rows, :] = jnp.sqrt(1.0 - a * a) * (gi * xs)
        return 0

    lax.fori_loop(0, nb, gates_block, 0)

    def scan_region(r0, length, h0):
        ls = length // 8

        def step(s, carry):
            out = []
            for (a_ref, h_ref), idx, (h, ac) in zip(dirs, (s, ls - 1 - s), carry):
                rows = pl.ds(r0 + idx, 8, stride=ls)
                a = a_ref[rows, :]
                h = a * h + h_ref[rows, :]
                ac = ac * a
                h_ref[rows, :] = h
                a_ref[rows, :] = ac
                out.append((h, ac))
            return tuple(out)

        init = (jnp.zeros((8, HEAD), F32), jnp.ones((8, HEAD), F32))
        local = lax.fori_loop(0, ls, step, (init, init), unroll=4)
        carry_in, final = [], []
        for d, order in enumerate((range(8), range(7, -1, -1))):
            hk, ak = local[d]
            cin = [None] * 8
            c = h0[d]
            for k in order:
                cin[k] = c
                c = ak[k:k + 1] * c + hk[k:k + 1]
            carry_in.append(jnp.concatenate(cin, axis=0))
            final.append(c)
        return carry_in, final

    def emit(rows, cf, cb_):
        h = (hf_ref[rows, :] + af_ref[rows, :] * cf) + (hb_ref[rows, :] + ab_ref[rows, :] * cb_)
        o_ref[rows, :] = (h * _silu(z_ref[rows, :].astype(F32))).astype(BF16)

    zero = jnp.zeros((1, HEAD), F32)
    cin_ctx, s_ctx = scan_region(lat, ctx, (zero, zero))
    for k in range(8):
        emit(pl.ds(lat + k * (ctx // 8), ctx // 8), cin_ctx[0][k:k + 1], cin_ctx[1][k:k + 1])
    cin_lat, _ = scan_region(0, lat, s_ctx)
    cin_ref[0:8, :] = cin_lat[0]
    cin_ref[8:16, :] = cin_lat[1]
    blocks_per_segment = lat // 8 // TOK

    def out_block(b, _):
        k = b // blocks_per_segment
        emit(pl.ds(pl.multiple_of(b * TOK, TOK), TOK), cin_ref[pl.ds(k, 1), :], cin_ref[pl.ds(8 + k, 1), :])
        return 0

    lax.fori_loop(0, nb_lat, out_block, 0)


def _lru(p, conv_w, conv_b, w_gates, b_gates, lam, lat):
    t = p.shape[0]
    col = lambda c0: pl.BlockSpec((t, HEAD), lambda n: (0, c0 // HEAD + n))
    vec = lambda r: pl.BlockSpec((r, HEAD), lambda n: (0, n))
    return pl.pallas_call(
        functools.partial(_lru_kernel, lat=lat),
        grid=(LRU_BLOCKS,),
        in_specs=[col(P_BX), col(P_BZ), vec(4), vec(1),
                  pl.BlockSpec((1, HEAD, 4 * HEAD), lambda n: (n, 0, 0)),
                  pl.BlockSpec((1, 1, 4 * HEAD), lambda n: (n, 0, 0)), vec(2)],
        out_specs=pl.BlockSpec((t, HEAD), lambda n: (0, n)),
        out_shape=jax.ShapeDtypeStruct((t, LRU_WIDTH), BF16),
        scratch_shapes=[pltpu.VMEM((t, HEAD), F32)] * 4 + [pltpu.VMEM((16, HEAD), F32)],
        compiler_params=_params("parallel"),
        name="lru",
    )(p, p, conv_w, conv_b.reshape(1, -1), w_gates, b_gates, lam)


def _dn_prep_kernel(q_ref, k_ref, v_ref, qp_ref, kp_ref, vp_ref, qx_ref, kx_ref, vx_ref, cw_ref, sm_ref,
                    alog_ref, dtb_ref, u_ref, w_ref, qd_ref, kd_ref, qk_ref, gl_ref, *, nb_lat):
    i = pl.program_id(0)
    tb = q_ref.shape[0]
    cc = DN_CHUNK
    first = jnp.logical_or(i == 0, i == nb_lat)
    last = jnp.logical_or(i == nb_lat - 1, i == nb_lat)
    cw = cw_ref[...]

    def conv_silu(x_ref, prev_ref, next_ref, c0):
        xe = jnp.concatenate([jnp.where(first, 0.0, prev_ref[...].astype(F32)), x_ref[...].astype(F32),
                              jnp.where(last, 0.0, next_ref[...].astype(F32))], axis=0)
        y = xe[HALO - 2:HALO - 2 + tb] * cw[0:1, c0:c0 + MIX]
        for j in range(1, 4):
            y = y + xe[HALO - 2 + j:HALO - 2 + j + tb] * cw[j:j + 1, c0:c0 + MIX]
        return _silu(y)

    q = conv_silu(q_ref, qp_ref, qx_ref, 0)
    k = conv_silu(k_ref, kp_ref, kx_ref, MIX)
    v = conv_silu(v_ref, vp_ref, vx_ref, 2 * MIX)

    sm = sm_ref[...].astype(F32)
    beta_all = _sigmoid(sm)
    g_all = -jnp.exp(alog_ref[...]) * _softplus(sm + dtb_ref[...])

    r = lax.broadcasted_iota(jnp.int32, (tb, tb), 0)
    c = lax.broadcasted_iota(jnp.int32, (tb, tb), 1)
    same = (r // cc) == (c // cc)
    tri_f = jnp.where(same, jnp.where(c <= r, 1.0, 0.0), 0.0).astype(BF16)
    tri_b = jnp.where(same, jnp.where(c >= r, 1.0, 0.0), 0.0).astype(BF16)
    gcs = (_dot_exact_lhs(tri_f, g_all), _dot_exact_lhs(tri_b, g_all))

    ii = lax.broadcasted_iota(jnp.int32, (cc, cc), 0)
    jj = lax.broadcasted_iota(jnp.int32, (cc, cc), 1)
    eye = jnp.where(ii == jj, 1.0, 0.0)
    incl = (ii >= jj, ii <= jj)
    strict = (ii > jj, ii < jj)

    gl_ref[...] = jnp.zeros_like(gl_ref)
    qs, ks = [], []
    for h in range(DN_HEADS):
        hs = slice(h * HEAD, (h + 1) * HEAD)
        qh, kh = q[:, hs], k[:, hs]
        qs.append(qh * lax.rsqrt(jnp.sum(qh * qh, axis=-1, keepdims=True) + NORM_EPS) * (HEAD ** -0.5))
        ks.append(kh * lax.rsqrt(jnp.sum(kh * kh, axis=-1, keepdims=True) + NORM_EPS))

    for ch in range(tb // cc):
        rows = slice(ch * cc, (ch + 1) * cc)
        gc = [gcs[d][rows] for d in range(2)]
        gct = [g.T for g in gc]
        kk, qk0 = [], []
        for h in range(DN_HEADS):
            k_split = _split2(ks[h][rows])
            kk.append(_dot3s(k_split, k_split, NT))
            qk0.append(_dot3s(_split2(qs[h][rows]), k_split, NT))
        probs = []
        for d in range(2):
            end = cc - 1 if d == 0 else 0
            for h in range(DN_HEADS):
                hs = slice(h * HEAD, (h + 1) * HEAD)
                lb = SM_AB + d * 2 * DN_HEADS + h
                lg = lb + DN_HEADS
                beta = beta_all[rows, lb:lb + 1]
                gcol, grow = gc[d][:, lg:lg + 1], gct[d][lg:lg + 1, :]
                glast = gc[d][end:end + 1, lg:lg + 1]
                qh, kh, vh = qs[h][rows], ks[h][rows], v[rows, hs]
                decay = jnp.where(incl[d], jnp.exp(jnp.where(incl[d], gcol - grow, 0.0)), 0.0)
                eg = jnp.exp(gcol)
                kb = kh * beta
                lm = jnp.where(strict[d], beta * kk[h] * decay, 0.0)
                qk_ref[d, h, rows, :] = qk0[h] * decay
                qd_ref[d, rows, hs] = qh * eg
                kd_ref[d, rows, hs] = kh * jnp.exp(glast - gcol)
                gl_ref[d, ch, h:h + 1, :] = jnp.broadcast_to(jnp.exp(glast), (1, HEAD))
                probs.append((d, hs, lm, jnp.concatenate([vh * beta, kb * eg], axis=1)))
        xs = [eye - lm for _, _, lm, _ in probs]
        pw = [_split2(lm) for _, _, lm, _ in probs]
        pw = [_dot3s(s, s) for s in pw]
        for _ in range(4):
            pw = [_split2(m) for m in pw]
            xs = [x + _dot3s(_split2(x), s) for x, s in zip(xs, pw)]
            pw = [_dot3s(s, s) for s in pw]
        pw = [_split2(m) for m in pw]
        xs = [x + _dot3s(_split2(x), s) for x, s in zip(xs, pw)]
        sols = [_dot3(x, rhs) for x, (_, _, _, rhs) in zip(xs, probs)]
        for sol, (d, hs, _, _) in zip(sols, probs):
            u_ref[d, rows, hs] = sol[:, :HEAD]
            w_ref[d, rows, hs] = sol[:, HEAD:]


def _dn_prep(p, conv_w, alog_row, dtb_row, lat):
    t = p.shape[0]
    tb = TOK
    nb, nb_lat = t // tb, lat // tb
    rh = tb // HALO
    blk = lambda c0: pl.BlockSpec((tb, MIX), lambda i: (i, c0 // MIX))
    prev = lambda c0: pl.BlockSpec((HALO, MIX), lambda i: (jnp.maximum(i * rh - 1, 0), c0 // MIX))
    nxt = lambda c0: pl.BlockSpec((HALO, MIX), lambda i: (jnp.minimum((i + 1) * rh, t // HALO - 1), c0 // MIX))
    cols = (P_CQ, P_CK, P_CV)
    big = pl.BlockSpec((2, tb, MIX), lambda i: (0, i, 0))
    big_shape = lambda dt: jax.ShapeDtypeStruct((2, t, MIX), dt)
    return pl.pallas_call(
        functools.partial(_dn_prep_kernel, nb_lat=nb_lat),
        grid=(nb,),
        in_specs=[blk(c0) for c0 in cols] + [prev(c0) for c0 in cols] + [nxt(c0) for c0 in cols] + [
            pl.BlockSpec((4, 3 * MIX), lambda i: (0, 0)),
            pl.BlockSpec((tb, HEAD), lambda i: (i, P_SM // HEAD)),
            pl.BlockSpec((1, HEAD), lambda i: (0, 0)),
            pl.BlockSpec((1, HEAD), lambda i: (0, 0))],
        out_specs=[big, big, big, big,
                   pl.BlockSpec((2, DN_HEADS, tb, DN_CHUNK), lambda i: (0, 0, i, 0)),
                   pl.BlockSpec((2, tb // DN_CHUNK, 8, HEAD), lambda i: (0, i, 0, 0))],
        out_shape=[big_shape(F32), big_shape(F32), big_shape(F32), big_shape(F32),
                   jax.ShapeDtypeStruct((2, DN_HEADS, t, DN_CHUNK), F32),
                   jax.ShapeDtypeStruct((2, t // DN_CHUNK, 8, HEAD), F32)],
        compiler_params=_params("parallel"),
        name="dn_prep",
    )(*([p] * 9), conv_w, p, alog_row, dtb_row)


def _dn_scan_kernel(uf, wf, qdf, kdf, qkf, glf, ub, wb, qdb, kdb, qkb, glb, of_ref, ob_ref, s_ref):
    @pl.when(pl.program_id(0) == 0)
    def _():
        s_ref[...] = jnp.zeros_like(s_ref)

    cc = DN_CHUNK
    dirs = ((uf, wf, qdf, kdf, qkf, glf, of_ref), (ub, wb, qdb, kdb, qkb, glb, ob_ref))
    probs = [(d, h, slice(h * HEAD, (h + 1) * HEAD)) for d in range(2) for h in range(DN_HEADS)]
    ws = [_dot3(jnp.concatenate([dirs[d][1][0, :, hs], dirs[d][2][0, :, hs]], axis=0), s_ref[d, h])
          for d, h, hs in probs]
    v_new = [_split2(dirs[d][0][0, :, hs] - t[:cc]) for t, (d, h, hs) in zip(ws, probs)]
    outs = [t[cc:] + _dot3s(_split2(dirs[d][4][0, h]), vn) for t, vn, (d, h, hs) in zip(ws, v_new, probs)]
    upd = [_dot3s(_split2(dirs[d][3][0, :, hs]), vn, TN) for vn, (d, h, hs) in zip(v_new, probs)]
    for o, ds, (d, h, hs) in zip(outs, upd, probs):
        dirs[d][6][:, hs] = o
        s_ref[d, h] = s_ref[d, h] * dirs[d][5][0, 0, h:h + 1, :] + ds


def _dn_scan(u, w, qd, kd, qk, gl, lat):
    t = u.shape[1]
    cc = DN_CHUNK
    n, n_lat = t // cc, lat // cc
    n_ctx = n - n_lat
    cf = lambda i: jnp.where(i < n_ctx, n_lat + i, i - n_ctx)
    cb = lambda i: n - 1 - i
    specs = []
    for d, ch in ((0, cf), (1, cb)):
        big = pl.BlockSpec((1, cc, MIX), lambda i, d=d, ch=ch: (d, ch(i), 0))
        specs += [big, big, big, big,
                  pl.BlockSpec((1, DN_HEADS, cc, cc), lambda i, d=d, ch=ch: (d, 0, ch(i), 0)),
                  pl.BlockSpec((1, 1, 8, HEAD), lambda i, d=d, ch=ch: (d, ch(i), 0, 0))]
    return pl.pallas_call(
        _dn_scan_kernel,
        grid=(n,),
        in_specs=specs,
        out_specs=[pl.BlockSpec((cc, MIX), lambda i: (cf(i), 0)), pl.BlockSpec((cc, MIX), lambda i: (cb(i), 0))],
        out_shape=[jax.ShapeDtypeStruct((t, MIX), F32)] * 2,
        scratch_shapes=[pltpu.VMEM((2, DN_HEADS, HEAD, HEAD), F32)],
        compiler_params=_params("arbitrary"),
        name="dn_scan",
    )(u, w, qd, kd, qk, gl, u, w, qd, kd, qk, gl)


def _dn_out_kernel(of_ref, ob_ref, z_ref, nw_ref, o_ref):
    for h in range(DN_HEADS):
        hs = slice(h * HEAD, (h + 1) * HEAD)
        o = of_ref[:, hs] + ob_ref[:, hs]
        o_ref[:, hs] = (_rms(o, nw_ref[...], HEAD) * _silu(z_ref[:, hs].astype(F32))).astype(BF16)


def _dn_out(o_f, o_b, p, norm_w):
    t = p.shape[0]
    tb = _row_tile(t)
    row = lambda c: pl.BlockSpec((tb, MIX), lambda i: (i, c))
    return pl.pallas_call(
        _dn_out_kernel,
        grid=(t // tb,),
        in_specs=[row(0), row(0), row(P_CZ // MIX), pl.BlockSpec((1, HEAD), lambda i: (0, 0))],
        out_specs=row(0),
        out_shape=jax.ShapeDtypeStruct((t, MIX), BF16),
        compiler_params=_params("parallel"),
        name="dn_out",
    )(o_f, o_b, p, norm_w.reshape(1, HEAD))


def _arrange_w_in(w_in):
    pad = jnp.zeros(w_in.shape[:2] + (P_W - w_in.shape[2],), w_in.dtype)
    parts = [w_in[..., 0:4608],
             w_in[..., 5328:5840],
             w_in[..., 5008:5264],
             w_in[..., 4624:5008],
             w_in[..., 5264:5328],
             w_in[..., 4608:4624],
             pad]
    return jnp.concatenate(parts, axis=-1).astype(BF16)


def _arrange_mla_wq(w_uq):
    l, r, _ = w_uq.shape
    w = w_uq.reshape(l, r, MLA_HEADS, MLA_QK)
    w = jnp.pad(w, ((0, 0), (0, 0), (0, 0), (0, 2 * HEAD - MLA_QK)))
    return w.reshape(l, r, MLA_HEADS * 2 * HEAD).astype(BF16)


def _pad_qk_norm(w):
    return jnp.pad(w, ((0, 0), (0, 2 * HEAD - MLA_QK)))[:, None, :]


def _small_lane_row(vals):
    l = vals.shape[0]
    row = jnp.zeros((l, HEAD), F32)
    for d in range(2):
        lo = SM_AB + d * 2 * DN_HEADS + DN_HEADS
        row = row.at[:, lo:lo + DN_HEADS].set(vals[:, d, :])
    return row[:, None, :]


def _arrange_lru_gates(w_a, b_a, w_x, b_x):
    w = jnp.concatenate([w_a[:, 0], w_x[:, 0], w_a[:, 1], w_x[:, 1]], axis=-1)
    l = b_a.shape[0]
    blk = lambda b, d: b[:, d].reshape(l, LRU_BLOCKS, 1, HEAD)
    b = jnp.concatenate([blk(b_a, 0), blk(b_x, 0), blk(b_a, 1), blk(b_x, 1)], axis=-1)
    return w.astype(BF16), b


def _rope_tables(lat, ctx):
    t = jnp.arange(lat)
    rows, cols = (t // GRID_W).astype(F32), (t % GRID_W).astype(F32)
    lane = jnp.arange(HEAD)

    def table(half, width):
        inv_freq = ROPE_THETA ** (-jnp.arange(half, dtype=F32) / half)
        pos = jnp.where((lane // (2 * half))[None, :] == 0, rows[:, None], cols[:, None])
        ang = pos * inv_freq[lane % half][None, :]
        live = (lane < width)[None, :]
        cos = jnp.where(live, jnp.cos(ang), 0.0)
        sin = jnp.where(live, jnp.sin(ang), 0.0) * jnp.where((lane % (2 * half)) < half, -1.0, 1.0)[None, :]
        cos_c = jnp.broadcast_to(jnp.where(live, 1.0, 0.0), (ctx, HEAD))
        return (jnp.concatenate([cos, cos_c], axis=0).astype(F32),
                jnp.concatenate([sin, jnp.zeros((ctx, HEAD), F32)], axis=0).astype(F32))

    return table(32, HEAD), table(16, MLA_ROPE)


def kernel(x, c, ctx, c_ctx, norm_w, w_ada, b_ada, w_in, w_out, attn_q_norm, attn_k_norm, lru_conv_w, lru_conv_b, lru_w_a, lru_b_a, lru_w_x, lru_b_x, lru_lambda, dn_conv_w, dn_a_log, dn_dt_bias, dn_norm_w, mla_q_norm, mla_kv_norm, mla_w_uq, mla_w_ukv, mla_q_qk_norm, mla_k_qk_norm):
    assert x.shape[0] == 1 and ctx.shape[1] == TOK and x.shape[1] % (8 * TOK) == 0
    lat, n_ctx = x.shape[1], ctx.shape[1]
    depth = w_in.shape[0]

    xs = jnp.concatenate([x[0], ctx[0]], axis=0)
    mod = _ada(jnp.stack([c[0], c_ctx], axis=1), w_ada, b_ada)
    (cos_a, sin_a), (cos_m, sin_m) = _rope_tables(lat, n_ctx)
    w_in_r = _arrange_w_in(w_in)
    w_out_r = w_out.reshape(depth, 4, MIX, D_MODEL).astype(BF16)
    wq_r = _arrange_mla_wq(mla_w_uq)
    wkv_r = mla_w_ukv.astype(BF16)
    qn_r, kn_r = _pad_qk_norm(mla_q_qk_norm), _pad_qk_norm(mla_k_qk_norm)
    alog_r, dtb_r = _small_lane_row(dn_a_log), _small_lane_row(dn_dt_bias)
    lru_w, lru_b = _arrange_lru_gates(lru_w_a, lru_b_a, lru_w_x, lru_b_x)

    for l in range(depth):
        p = _inproj(xs, norm_w[l], mod[l, 0], mod[l, 1], w_in_r, l, lat)
        qa, ka = _gqa_prep(p, cos_a, sin_a, attn_q_norm[l], attn_k_norm[l])
        bound_a = jnp.max(jnp.abs(attn_q_norm[l])) * jnp.max(jnp.abs(attn_k_norm[l])) * (HEAD ** 0.5 * LOG2E)
        y_a = _flash(qa, ka, p, P_AV, p, P_AZ, A_HEADS, A_KV_HEADS, HEAD, lat, bound_a, "gqa_attn")
        y_b = _lru(p, lru_conv_w[l], lru_conv_b[l], lru_w[l], lru_b[l], lru_lambda[l], lat)
        u, w, qd, kd, qk, gl = _dn_prep(p, dn_conv_w[l], alog_r[l], dtb_r[l], lat)
        o_f, o_b = _dn_scan(u, w, qd, kd, qk, gl, lat)
        y_c = _dn_out(o_f, o_b, p, dn_norm_w[l])
        qm, km, vm = _mla_prep(p, cos_m, sin_m, mla_q_norm[l], mla_kv_norm[l], wq_r[l], wkv_r[l],
                               qn_r[l], kn_r[l])
        bound_d = (jnp.max(jnp.abs(mla_q_qk_norm[l])) * jnp.max(jnp.abs(mla_k_qk_norm[l]))
                   * (MLA_QK ** 0.5 * LOG2E))
        y_d = _flash(qm, km, vm, 0, p, P_DZ, MLA_HEADS, MLA_HEADS, 2 * HEAD, lat, bound_d, "mla_attn")
        last = l == depth - 1
        xs = _outproj((y_a, y_b, y_c, y_d), w_out_r, l, xs, mod[l, 2], lat, lat if last else lat + n_ctx)
    return xs[None]
```

```python
import functools
import math

import jax
import jax.numpy as jnp
from jax import lax
from jax.experimental import pallas as pl
from jax.experimental.pallas import tpu as pltpu

F32 = jnp.float32
BF16 = jnp.bfloat16

D_MODEL = 2048
DEPTH = 4
GRID_W = 64
ROPE_THETA = 10000.0
NORM_EPS = 1e-6
HEAD = 128
A_HEADS, A_KV_HEADS = 4, 2
LRU_WIDTH, LRU_BLOCKS, LRU_C = 512, 4, 8.0
DN_HEADS, DN_CHUNK = 4, 64
MLA_HEADS, MLA_Q_RANK, MLA_KV_RANK, MLA_NOPE, MLA_ROPE = 4, 384, 256, 128, 64
MLA_QK = MLA_NOPE + MLA_ROPE
MIX = 512

P_AQ, P_AK, P_AV, P_AZ = 0, 512, 768, 1024
P_BX, P_BZ = 1536, 2048
P_CQ, P_CK, P_CV, P_CZ = 2560, 3072, 3584, 4096
P_DZ, P_DCKV, P_DCQ = 4608, 5120, 5376
P_SM = 5760
P_W = 6144
P_MAIN = 4608
SM_AB = MLA_ROPE

TOK = 256
HALO = 16
VMEM_LIMIT = 56 * 1024 * 1024
LOG2E = math.log2(math.e)
MAX_EXP2_LOGIT = 60.0

NN = ((1,), (0,))
NT = ((1,), (1,))
TN = ((0,), (0,))


def _dg(a, b, dims=NN):
    return lax.dot_general(a, b, (dims, ((), ())), preferred_element_type=F32)


def _split2(a):
    hi = a.astype(BF16)
    lo = (a - hi.astype(F32)).astype(BF16)
    return hi, lo


def _dot3s(a, b, dims=NN):
    return _dg(a[0], b[0], dims) + (_dg(a[0], b[1], dims) + _dg(a[1], b[0], dims))


def _dot3(a, b, dims=NN):
    return _dot3s(_split2(a), _split2(b), dims)


def _dot_exact_lhs(m, b):
    b0 = b.astype(BF16)
    r1 = b - b0.astype(F32)
    b1 = r1.astype(BF16)
    b2 = (r1 - b1.astype(F32)).astype(BF16)
    return _dg(m, b0) + (_dg(m, b1) + _dg(m, b2))


def _sigmoid(x):
    return 1.0 / (1.0 + jnp.exp(-x))


def _silu(x):
    return x * _sigmoid(x)


def _softplus(x):
    return jnp.maximum(x, 0.0) + jnp.log(1.0 + jnp.exp(-jnp.abs(x)))


def _rms(x, w, n):
    return x * lax.rsqrt(jnp.sum(x * x, axis=-1, keepdims=True) * (1.0 / n) + NORM_EPS) * w


def _rope(x, cos, sin_signed, half):
    lane = lax.broadcasted_iota(jnp.int32, x.shape, 1)
    first = (lane % (2 * half)) < half
    rot = jnp.where(first, pltpu.roll(x, HEAD - half, 1), pltpu.roll(x, half, 1))
    return x * cos + rot * sin_signed


def _params(*sem):
    return pltpu.CompilerParams(dimension_semantics=sem, vmem_limit_bytes=VMEM_LIMIT)


def _row_tile(rows):
    for t in (768, 512, 256):
        if rows % t == 0:
            return t
    raise ValueError(f"row count {rows} is not a multiple of {TOK}")


def _ada_kernel(c_ref, w_ref, b_ref, o_ref):
    s = _silu(c_ref[...])
    w = w_ref[0]
    r0 = jnp.sum(s[:, 0:1] * w, axis=0, keepdims=True)
    r1 = jnp.sum(s[:, 1:2] * w, axis=0, keepdims=True)
    o_ref[0, 0] = jnp.concatenate([r0, r1], axis=0) + b_ref[0]


def _ada(c_cols, w_ada, b_ada):
    depth, d, _ = w_ada.shape
    tn = 512
    per = d // tn
    return pl.pallas_call(
        _ada_kernel,
        grid=(depth, 3 * per),
        in_specs=[
            pl.BlockSpec((d, 2), lambda l, j: (0, 0)),
            pl.BlockSpec((1, d, tn), lambda l, j: (l, 0, j)),
            pl.BlockSpec((1, 1, tn), lambda l, j: (l, 0, j)),
        ],
        out_specs=pl.BlockSpec((1, 1, 2, tn), lambda l, j: (l, j // per, 0, j % per)),
        out_shape=jax.ShapeDtypeStruct((depth, 3, 2, d), F32),
        compiler_params=_params("parallel", "parallel"),
        name="ada",
    )(c_cols, w_ada, b_ada.reshape(depth, 1, 3 * d))


def _inproj_kernel(x_ref, nw_ref, shift_ref, scale_ref, wm_ref, wt_ref, o_ref, h_ref, *, lat, n_main):
    i = pl.program_id(0)
    tm = x_ref.shape[0]

    @pl.when(pl.program_id(1) == 0)
    def _():
        rc = 32

        def chunk(ci, _):
            rows = pl.ds(pl.multiple_of(ci * rc, rc), rc)
            y = _rms(x_ref[rows, :], nw_ref[...], D_MODEL)
            is_ctx = i * tm + ci * rc + lax.broadcasted_iota(jnp.int32, (rc, 1), 0) >= lat
            scale = jnp.where(is_ctx, scale_ref[1:2, :], scale_ref[0:1, :])
            shift = jnp.where(is_ctx, shift_ref[1:2, :], shift_ref[0:1, :])
            h_ref[rows, :] = (y * (1.0 + scale) + shift).astype(BF16)
            return 0

        lax.fori_loop(0, tm // rc, chunk, 0)

    @pl.when(pl.program_id(1) < n_main)
    def _():
        o_ref[...] = _dg(h_ref[...], wm_ref[...]).astype(o_ref.dtype)

    @pl.when(pl.program_id(1) >= n_main)
    def _():
        o_ref[...] = _dg(h_ref[...], wt_ref[...]).astype(o_ref.dtype)


def _inproj(xs, norm_w, shift, scale, w_main, w_tail, layer, lat):
    t, d = xs.shape
    tm, tn = (1408 if t % 1408 == 0 else _row_tile(t)), 768
    n_main = P_MAIN // tn
    return pl.pallas_call(
        functools.partial(_inproj_kernel, lat=lat, n_main=n_main),
        grid=(t // tm, P_W // tn),
        in_specs=[
            pl.BlockSpec((tm, d), lambda i, j: (i, 0)),
            pl.BlockSpec((1, d), lambda i, j: (0, 0)),
            pl.BlockSpec((2, d), lambda i, j: (0, 0)),
            pl.BlockSpec((2, d), lambda i, j: (0, 0)),
            pl.BlockSpec((None, d, tn), lambda i, j: (layer, 0, jnp.minimum(j, n_main - 1))),
            pl.BlockSpec((None, d, tn), lambda i, j: (layer, 0, jnp.maximum(j - n_main, 0))),
        ],
        out_specs=pl.BlockSpec((tm, tn), lambda i, j: (i, j)),
        out_shape=jax.ShapeDtypeStruct((t, P_W), BF16),
        scratch_shapes=[pltpu.VMEM((tm, d), BF16)],
        compiler_params=_params("parallel", "arbitrary"),
        name="inproj",
    )(xs, norm_w.reshape(1, d), shift, scale, w_main, w_tail)


def _outproj_kernel(ya_ref, yb_ref, yc_ref, yd_ref, w_ref, x_ref, g_ref, o_ref, *, lat):
    i = pl.program_id(0)
    tm = x_ref.shape[0]
    acc = _dg(ya_ref[...], w_ref[0])
    acc += _dg(yb_ref[...], w_ref[1])
    acc += _dg(yc_ref[...], w_ref[2])
    acc += _dg(yd_ref[...], w_ref[3])
    row = i * tm + lax.broadcasted_iota(jnp.int32, (tm, 1), 0)
    gate = jnp.where(row >= lat, g_ref[1:2, :], g_ref[0:1, :])
    o_ref[...] = x_ref[...] + gate * acc


def _outproj(ys, w_out, layer, xs, gate, lat, out_rows):
    d = xs.shape[1]
    tm, tn = _row_tile(out_rows), 1024
    yspec = pl.BlockSpec((tm, MIX), lambda i, j: (i, 0))
    return pl.pallas_call(
        functools.partial(_outproj_kernel, lat=lat),
        grid=(out_rows // tm, d // tn),
        in_specs=[yspec, yspec, yspec, yspec,
                  pl.BlockSpec((None, 4, MIX, tn), lambda i, j: (layer, 0, 0, j)),
                  pl.BlockSpec((tm, tn), lambda i, j: (i, j)),
                  pl.BlockSpec((2, tn), lambda i, j: (0, j))],
        out_specs=pl.BlockSpec((tm, tn), lambda i, j: (i, j)),
        out_shape=jax.ShapeDtypeStruct((out_rows, d), F32),
        compiler_params=_params("parallel", "parallel"),
        name="outproj",
    )(*ys, w_out, xs, gate)


def _flash_kernel(q_ref, k_ref, v_ref, z_ref, *rest, n_loop, tk, tail, dq, shared_kv, bounded):
    o_ref = rest[-1]
    tq = q_ref.shape[0]
    nh = q_ref.shape[1] // dq

    def scores(c, rows):
        kc = 0 if shared_kv else c
        s = _dg(q_ref[:, c * dq:(c + 1) * dq], k_ref[rows, kc * dq:(kc + 1) * dq], NT)
        return s, v_ref[rows, kc * HEAD:(kc + 1) * HEAD]

    def attend_bounded(c, carry, rows):
        l, acc = carry
        s, v = scores(c, rows)
        p = jnp.exp2(s)
        for j in range(s.shape[1] // HEAD):
            l = l + p[:, j * HEAD:(j + 1) * HEAD]
        return l, acc + _dg(p.astype(BF16), v)

    def attend_online(c, carry, rows):
        m, l, acc = carry
        s, v = scores(c, rows)
        m_new = jnp.maximum(m, jnp.max(s, axis=-1, keepdims=True))
        alpha = jnp.exp2(m - m_new)
        p = jnp.exp2(s - m_new)
        l = alpha * l + jnp.sum(p, axis=-1, keepdims=True)
        return m_new, l, alpha * acc + _dg(p.astype(BF16), v)

    zeros = jnp.zeros((tq, HEAD), F32)
    if bounded:
        attend, init = attend_bounded, (zeros, zeros)
    else:
        attend, init = attend_online, (jnp.full((tq, 1), -1e30, F32), jnp.zeros((tq, 1), F32), zeros)

    def step(kb, carries):
        rows = pl.ds(pl.multiple_of(kb * tk, tk), tk)
        return tuple(attend(c, carries[c], rows) for c in range(nh))

    carries = (init,) * nh
    if n_loop:
        carries = lax.fori_loop(0, n_loop, step, carries, unroll=2)
    for c in range(nh):
        carry = attend(c, carries[c], pl.ds(tail[0], tail[1]))
        l, acc = carry[-2], carry[-1]
        cs = slice(c * HEAD, (c + 1) * HEAD)
        gate = _silu(z_ref[:, cs].astype(F32))
        o_ref[:, cs] = (acc / jnp.sum(l, axis=-1, keepdims=True) * gate).astype(o_ref.dtype)


def _flash(q, k, v, v_col, p, z_col, heads, kv_heads, dq, lat, logit_bound, name):
    t = q.shape[0]
    ctx = t - lat
    nh = 2
    shared_kv = heads // kv_heads == nh
    nk = 1 if shared_kv else nh
    tq = 512 if lat % 512 == 0 else TOK
    tk = 2048 if lat % 4096 == 0 else TOK
    zb = z_col // (nh * HEAD)
    vb = v_col // (nk * HEAD)
    cb = lat // ctx
    out_shape = jax.ShapeDtypeStruct((t, heads * HEAD), BF16)

    def call(bounded):
        common = dict(tk=tk, dq=dq, shared_kv=shared_kv, bounded=bounded)
        suffix = "_bounded" if bounded else "_online"
        y = pl.pallas_call(
            functools.partial(_flash_kernel, n_loop=lat // tk, tail=(lat, ctx), **common),
            grid=(heads // nh, lat // tq),
            in_specs=[
                pl.BlockSpec((tq, nh * dq), lambda g, i: (i, g)),
                pl.BlockSpec((t, nk * dq), lambda g, i: (0, g)),
                pl.BlockSpec((t, nk * HEAD), lambda g, i: (0, vb + g)),
                pl.BlockSpec((tq, nh * HEAD), lambda g, i: (i, zb + g)),
            ],
            out_specs=pl.BlockSpec((tq, nh * HEAD), lambda g, i: (i, g)),
            out_shape=out_shape,
            compiler_params=_params("parallel", "parallel"),
            name=name + suffix,
        )(q, k, v, p)
        return pl.pallas_call(
            functools.partial(_flash_kernel, n_loop=0, tail=(0, ctx), **common),
            grid=(heads // nh,),
            in_specs=[
                pl.BlockSpec((ctx, nh * dq), lambda g: (cb, g)),
                pl.BlockSpec((ctx, nk * dq), lambda g: (cb, g)),
                pl.BlockSpec((ctx, nk * HEAD), lambda g: (cb, vb + g)),
                pl.BlockSpec((ctx, nh * HEAD), lambda g: (cb, zb + g)),
                pl.BlockSpec(memory_space=pl.ANY),
            ],
            out_specs=pl.BlockSpec((ctx, nh * HEAD), lambda g: (cb, g)),
            out_shape=out_shape,
            input_output_aliases={4: 0},
            compiler_params=_params("parallel"),
            name=name + "_ctx" + suffix,
        )(q, k, v, p, y)

    return lax.cond(logit_bound <= MAX_EXP2_LOGIT, lambda: call(True), lambda: call(False))


def _gqa_prep_kernel(q_ref, k_ref, cos_ref, sin_ref, qn_ref, kn_ref, oq_ref, ok_ref):
    cos, sin = cos_ref[...], sin_ref[...]
    scale = HEAD ** -0.5 * LOG2E
    for h in range(A_HEADS):
        c = slice(h * HEAD, (h + 1) * HEAD)
        oq_ref[:, c] = (_rope(_rms(q_ref[:, c].astype(F32), qn_ref[...], HEAD), cos, sin, 32) * scale).astype(BF16)
    for g in range(A_KV_HEADS):
        c = slice(g * HEAD, (g + 1) * HEAD)
        ok_ref[:, c] = _rope(_rms(k_ref[:, c].astype(F32), kn_ref[...], HEAD), cos, sin, 32).astype(BF16)


def _gqa_prep(p, cos, sin, qn, kn):
    t = p.shape[0]
    tb = _row_tile(t)
    qw, kw = A_HEADS * HEAD, A_KV_HEADS * HEAD
    row = lambda w, c: pl.BlockSpec((tb, w), lambda i: (i, c))
    vec = pl.BlockSpec((1, HEAD), lambda i: (0, 0))
    return pl.pallas_call(
        _gqa_prep_kernel,
        grid=(t // tb,),
        in_specs=[row(qw, P_AQ // qw), row(kw, P_AK // kw), row(HEAD, 0), row(HEAD, 0), vec, vec],
        out_specs=[row(qw, 0), row(kw, 0)],
        out_shape=[jax.ShapeDtypeStruct((t, qw), BF16), jax.ShapeDtypeStruct((t, kw), BF16)],
        compiler_params=_params("parallel"),
        name="gqa_prep",
    )(p, p, cos, sin, qn.reshape(1, HEAD), kn.reshape(1, HEAD))


def _mla_prep_kernel(cq_ref, ckv_ref, sm_ref, cos_ref, sin_ref, qnw_ref, kvnw_ref, wq_ref, wkv_ref,
                     qn_ref, kn_ref, oq_ref, ok_ref, ov_ref):
    cos, sin = cos_ref[...], sin_ref[...]
    scale = MLA_QK ** -0.5 * LOG2E
    qn, kn = qn_ref[...], kn_ref[...]
    q = _dg(_rms(cq_ref[...].astype(F32), qnw_ref[...], MLA_Q_RANK).astype(BF16), wq_ref[...])
    kv = _dg(_rms(ckv_ref[...].astype(F32), kvnw_ref[...], MLA_KV_RANK).astype(BF16), wkv_ref[...])
    lane = lax.broadcasted_iota(jnp.int32, sm_ref.shape, 1)
    kr = jnp.where(lane < MLA_ROPE, sm_ref[...].astype(F32), 0.0)
    kr_ss = jnp.sum(kr * kr, axis=-1, keepdims=True)
    for h in range(MLA_HEADS):
        lo = slice(2 * h * HEAD, (2 * h + 1) * HEAD)
        hi = slice((2 * h + 1) * HEAD, (2 * h + 2) * HEAD)
        q_nope, q_rope = q[:, lo], q[:, hi]
        r = lax.rsqrt((jnp.sum(q_nope * q_nope, axis=-1, keepdims=True)
                       + jnp.sum(q_rope * q_rope, axis=-1, keepdims=True)) * (1.0 / MLA_QK) + NORM_EPS)
        oq_ref[:, lo] = (q_nope * r * qn[:, :HEAD] * scale).astype(BF16)
        oq_ref[:, hi] = (_rope(q_rope * r * qn[:, HEAD:], cos, sin, 16) * scale).astype(BF16)
        k_nope, v = kv[:, lo], kv[:, hi]
        r = lax.rsqrt((jnp.sum(k_nope * k_nope, axis=-1, keepdims=True) + kr_ss) * (1.0 / MLA_QK) + NORM_EPS)
        ok_ref[:, lo] = (k_nope * r * kn[:, :HEAD]).astype(BF16)
        ok_ref[:, hi] = _rope(kr * r * kn[:, HEAD:], cos, sin, 16).astype(BF16)
        ov_ref[:, h * HEAD:(h + 1) * HEAD] = v.astype(BF16)


def _mla_prep(p, cos, sin, q_norm_w, kv_norm_w, wq, wkv, qn, kn):
    t = p.shape[0]
    tb = _row_tile(t)
    w2 = MLA_HEADS * 2 * HEAD
    row = lambda w, c: pl.BlockSpec((tb, w), lambda i: (i, c))
    full = lambda a: pl.BlockSpec(a.shape, lambda i: (0, 0))
    args = (q_norm_w.reshape(1, -1), kv_norm_w.reshape(1, -1), wq, wkv, qn, kn)
    return pl.pallas_call(
        _mla_prep_kernel,
        grid=(t // tb,),
        in_specs=[row(MLA_Q_RANK, P_DCQ // MLA_Q_RANK), row(MLA_KV_RANK, P_DCKV // MLA_KV_RANK),
                  row(HEAD, P_SM // HEAD), row(HEAD, 0), row(HEAD, 0)] + [full(a) for a in args],
        out_specs=[row(w2, 0), row(w2, 0), row(MLA_HEADS * HEAD, 0)],
        out_shape=[jax.ShapeDtypeStruct((t, w2), BF16), jax.ShapeDtypeStruct((t, w2), BF16),
                   jax.ShapeDtypeStruct((t, MLA_HEADS * HEAD), BF16)],
        compiler_params=_params("parallel"),
        name="mla_prep",
    )(p, p, p, cos, sin, *args)


def _lru_kernel(ux_ref, z_ref, cw_ref, cb_ref, w_ref, b_ref, lam_ref, o_ref,
                af_ref, ab_ref, hf_ref, hb_ref, cin_ref, *, lat):
    t = ux_ref.shape[0]
    ctx = t - lat
    nb, nb_lat = t // TOK, lat // TOK
    cw, cb = cw_ref[...], cb_ref[...]
    bias = b_ref[0]
    sp = (_softplus(-lam_ref[0:1, :]), _softplus(-lam_ref[1:2, :]))
    dirs = ((af_ref, hf_ref), (ab_ref, hb_ref))

    def gates_block(b, _):
        t0 = pl.multiple_of(b * TOK, TOK)
        rows = pl.ds(t0, TOK)
        first = jnp.logical_or(b == 0, b == nb_lat)
        last = jnp.logical_or(b == nb_lat - 1, b == nb - 1)
        prev = ux_ref[pl.ds(pl.multiple_of(jnp.maximum(t0 - HALO, 0), HALO), HALO), :].astype(F32)
        nxt = ux_ref[pl.ds(pl.multiple_of(jnp.minimum(t0 + TOK, t - HALO), HALO), HALO), :].astype(F32)
        xe = jnp.concatenate([jnp.where(first, 0.0, prev), ux_ref[rows, :].astype(F32),
                              jnp.where(last, 0.0, nxt)], axis=0)
        xs = cb + xe[HALO - 2:HALO - 2 + TOK] * cw[0:1]
        for j in range(1, 4):
            xs = xs + xe[HALO - 2 + j:HALO - 2 + j + TOK] * cw[j:j + 1]
        g = _dg(xs.astype(BF16), w_ref[0]) + bias
        for d, (a_ref, h_ref) in enumerate(dirs):
            r = _sigmoid(g[:, 2 * d * HEAD:(2 * d + 1) * HEAD])
            gi = _sigmoid(g[:, (2 * d + 1) * HEAD:(2 * d + 2) * HEAD])
            a = jnp.exp(-LRU_C * r * sp[d])
            a_ref[rows, :] = a
            h_ref[rows, :] = jnp.sqrt(1.0 - a * a) * (gi * xs)
        return 0

    lax.fori_loop(0, nb, gates_block, 0)

    def scan_region(r0, length, h0):
        ls = length // 8

        def step(s, carry):
            out = []
            for (a_ref, h_ref), idx, (h, ac) in zip(dirs, (s, ls - 1 - s), carry):
                rows = pl.ds(r0 + idx, 8, stride=ls)
                a = a_ref[rows, :]
                h = a * h + h_ref[rows, :]
                ac = ac * a
                h_ref[rows, :] = h
                a_ref[rows, :] = ac
                out.append((h, ac))
            return tuple(out)

        init = (jnp.zeros((8, HEAD), F32), jnp.ones((8, HEAD), F32))
        local = lax.fori_loop(0, ls, step, (init, init), unroll=4)
        carry_in, final = [], []
        for d, order in enumerate((range(8), range(7, -1, -1))):
            hk, ak = local[d]
            cin = [None] * 8
            c = h0[d]
            for k in order:
                cin[k] = c
                c = ak[k:k + 1] * c + hk[k:k + 1]
            carry_in.append(jnp.concatenate(cin, axis=0))
            final.append(c)
        return carry_in, final

    def emit(rows, cf, cb_):
        h = (hf_ref[rows, :] + af_ref[rows, :] * cf) + (hb_ref[rows, :] + ab_ref[rows, :] * cb_)
        o_ref[rows, :] = (h * _silu(z_ref[rows, :].astype(F32))).astype(BF16)

    zero = jnp.zeros((1, HEAD), F32)
    cin_ctx, s_ctx = scan_region(lat, ctx, (zero, zero))
    for k in range(8):
        emit(pl.ds(lat + k * (ctx // 8), ctx // 8), cin_ctx[0][k:k + 1], cin_ctx[1][k:k + 1])
    cin_lat, _ = scan_region(0, lat, s_ctx)
    cin_ref[0:8, :] = cin_lat[0]
    cin_ref[8:16, :] = cin_lat[1]
    blocks_per_segment = lat // 8 // TOK

    def out_block(b, _):
        k = b // blocks_per_segment
        emit(pl.ds(pl.multiple_of(b * TOK, TOK), TOK), cin_ref[pl.ds(k, 1), :], cin_ref[pl.ds(8 + k, 1), :])
        return 0

    lax.fori_loop(0, nb_lat, out_block, 0)


def _lru(p, conv_w, conv_b, w_gates, b_gates, lam, lat):
    t = p.shape[0]
    col = lambda c0: pl.BlockSpec((t, HEAD), lambda n: (0, c0 // HEAD + n))
    vec = lambda r: pl.BlockSpec((r, HEAD), lambda n: (0, n))
    return pl.pallas_call(
        functools.partial(_lru_kernel, lat=lat),
        grid=(LRU_BLOCKS,),
        in_specs=[col(P_BX), col(P_BZ), vec(4), vec(1),
                  pl.BlockSpec((1, HEAD, 4 * HEAD), lambda n: (n, 0, 0)),
                  pl.BlockSpec((1, 1, 4 * HEAD), lambda n: (n, 0, 0)), vec(2)],
        out_specs=pl.BlockSpec((t, HEAD), lambda n: (0, n)),
        out_shape=jax.ShapeDtypeStruct((t, LRU_WIDTH), BF16),
        scratch_shapes=[pltpu.VMEM((t, HEAD), F32)] * 4 + [pltpu.VMEM((16, HEAD), F32)],
        compiler_params=_params("parallel"),
        name="lru",
    )(p, p, conv_w, conv_b.reshape(1, -1), w_gates, b_gates, lam)


def _dn_prep_kernel(q_ref, k_ref, v_ref, qp_ref, kp_ref, vp_ref, qx_ref, kx_ref, vx_ref, cw_ref, sm_ref,
                    alog_ref, dtb_ref, u_ref, w_ref, qd_ref, kd_ref, qk_ref, gl_ref, *, nb_lat):
    i = pl.program_id(0)
    tb = q_ref.shape[0]
    cc = DN_CHUNK
    first = jnp.logical_or(i == 0, i == nb_lat)
    last = jnp.logical_or(i == nb_lat - 1, i == nb_lat)
    cw = cw_ref[...]

    def conv_silu(x_ref, prev_ref, next_ref, c0):
        xe = jnp.concatenate([jnp.where(first, 0.0, prev_ref[...].astype(F32)), x_ref[...].astype(F32),
                              jnp.where(last, 0.0, next_ref[...].astype(F32))], axis=0)
        y = xe[HALO - 2:HALO - 2 + tb] * cw[0:1, c0:c0 + MIX]
        for j in range(1, 4):
            y = y + xe[HALO - 2 + j:HALO - 2 + j + tb] * cw[j:j + 1, c0:c0 + MIX]
        return _silu(y)

    q = conv_silu(q_ref, qp_ref, qx_ref, 0)
    k = conv_silu(k_ref, kp_ref, kx_ref, MIX)
    v = conv_silu(v_ref, vp_ref, vx_ref, 2 * MIX)

    sm = sm_ref[...].astype(F32)
    beta_all = _sigmoid(sm)
    g_all = -jnp.exp(alog_ref[...]) * _softplus(sm + dtb_ref[...])

    r = lax.broadcasted_iota(jnp.int32, (tb, tb), 0)
    c = lax.broadcasted_iota(jnp.int32, (tb, tb), 1)
    same = (r // cc) == (c // cc)
    tri_f = jnp.where(same, jnp.where(c <= r, 1.0, 0.0), 0.0).astype(BF16)
    tri_b = jnp.where(same, jnp.where(c >= r, 1.0, 0.0), 0.0).astype(BF16)
    gcs = (_dot_exact_lhs(tri_f, g_all), _dot_exact_lhs(tri_b, g_all))

    ii = lax.broadcasted_iota(jnp.int32, (cc, cc), 0)
    jj = lax.broadcasted_iota(jnp.int32, (cc, cc), 1)
    eye = jnp.where(ii == jj, 1.0, 0.0)
    incl = (ii >= jj, ii <= jj)
    strict = (ii > jj, ii < jj)

    gl_ref[...] = jnp.zeros_like(gl_ref)
    qs, ks = [], []
    for h in range(DN_HEADS):
        hs = slice(h * HEAD, (h + 1) * HEAD)
        qh, kh = q[:, hs], k[:, hs]
        qs.append(qh * lax.rsqrt(jnp.sum(qh * qh, axis=-1, keepdims=True) + NORM_EPS) * (HEAD ** -0.5))
        ks.append(kh * lax.rsqrt(jnp.sum(kh * kh, axis=-1, keepdims=True) + NORM_EPS))

    for ch in range(tb // cc):
        rows = slice(ch * cc, (ch + 1) * cc)
        gc = [gcs[d][rows] for d in range(2)]
        gct = [g.T for g in gc]
        kk, qk0 = [], []
        for h in range(DN_HEADS):
            k_split = _split2(ks[h][rows])
            kk.append(_dot3s(k_split, k_split, NT))
            qk0.append(_dot3s(_split2(qs[h][rows]), k_split, NT))
        probs = []
        for d in range(2):
            end = cc - 1 if d == 0 else 0
            for h in range(DN_HEADS):
                hs = slice(h * HEAD, (h + 1) * HEAD)
                lb = SM_AB + d * 2 * DN_HEADS + h
                lg = lb + DN_HEADS
                beta = beta_all[rows, lb:lb + 1]
                gcol, grow = gc[d][:, lg:lg + 1], gct[d][lg:lg + 1, :]
                glast = gc[d][end:end + 1, lg:lg + 1]
                qh, kh, vh = qs[h][rows], ks[h][rows], v[rows, hs]
                decay = jnp.where(incl[d], jnp.exp(jnp.where(incl[d], gcol - grow, 0.0)), 0.0)
                eg = jnp.exp(gcol)
                kb = kh * beta
                lm = jnp.where(strict[d], beta * kk[h] * decay, 0.0)
                qk_ref[d, h, rows, :] = (qk0[h] * decay).astype(BF16)
                qd_ref[d, rows, hs] = (qh * eg).astype(BF16)
                kd_ref[d, rows, hs] = (kh * jnp.exp(glast - gcol)).astype(BF16)
                gl_ref[d, ch, h:h + 1, :] = jnp.broadcast_to(jnp.exp(glast), (1, HEAD))
                probs.append((d, hs, lm, jnp.concatenate([vh * beta, kb * eg], axis=1)))
        xs = [eye - lm for _, _, lm, _ in probs]
        pw = [_split2(lm) for _, _, lm, _ in probs]
        pw = [_dot3s(s, s) for s in pw]
        for _ in range(4):
            pw = [_split2(m) for m in pw]
            xs = [x + _dot3s(_split2(x), s) for x, s in zip(xs, pw)]
            pw = [_dot3s(s, s) for s in pw]
        pw = [_split2(m) for m in pw]
        xs = [x + _dot3s(_split2(x), s) for x, s in zip(xs, pw)]
        sols = [_dot3(x, rhs) for x, (_, _, _, rhs) in zip(xs, probs)]
        for sol, (d, hs, _, _) in zip(sols, probs):
            u_ref[d, rows, hs] = sol[:, :HEAD]
            w_ref[d, rows, hs] = sol[:, HEAD:].astype(BF16)


def _dn_prep(p, conv_w, alog_row, dtb_row, lat):
    t = p.shape[0]
    tb = TOK
    nb, nb_lat = t // tb, lat // tb
    rh = tb // HALO
    blk = lambda c0: pl.BlockSpec((tb, MIX), lambda i: (i, c0 // MIX))
    prev = lambda c0: pl.BlockSpec((HALO, MIX), lambda i: (jnp.maximum(i * rh - 1, 0), c0 // MIX))
    nxt = lambda c0: pl.BlockSpec((HALO, MIX), lambda i: (jnp.minimum((i + 1) * rh, t // HALO - 1), c0 // MIX))
    cols = (P_CQ, P_CK, P_CV)
    big = pl.BlockSpec((2, tb, MIX), lambda i: (0, i, 0))
    big_shape = lambda dt: jax.ShapeDtypeStruct((2, t, MIX), dt)
    return pl.pallas_call(
        functools.partial(_dn_prep_kernel, nb_lat=nb_lat),
        grid=(nb,),
        in_specs=[blk(c0) for c0 in cols] + [prev(c0) for c0 in cols] + [nxt(c0) for c0 in cols] + [
            pl.BlockSpec((4, 3 * MIX), lambda i: (0, 0)),
            pl.BlockSpec((tb, HEAD), lambda i: (i, P_SM // HEAD)),
            pl.BlockSpec((1, HEAD), lambda i: (0, 0)),
            pl.BlockSpec((1, HEAD), lambda i: (0, 0))],
        out_specs=[big, big, big, big,
                   pl.BlockSpec((2, DN_HEADS, tb, DN_CHUNK), lambda i: (0, 0, i, 0)),
                   pl.BlockSpec((2, tb // DN_CHUNK, 8, HEAD), lambda i: (0, i, 0, 0))],
        out_shape=[big_shape(F32), big_shape(BF16), big_shape(BF16), big_shape(BF16),
                   jax.ShapeDtypeStruct((2, DN_HEADS, t, DN_CHUNK), BF16),
                   jax.ShapeDtypeStruct((2, t // DN_CHUNK, 8, HEAD), F32)],
        compiler_params=_params("parallel"),
        name="dn_prep",
    )(*([p] * 9), conv_w, p, alog_row, dtb_row)


def _dn_scan_kernel(uf, wf, qdf, kdf, qkf, glf, ub, wb, qdb, kdb, qkb, glb, of_ref, ob_ref, s_ref):
    @pl.when(pl.program_id(0) == 0)
    def _():
        s_ref[...] = jnp.zeros_like(s_ref)

    cc = DN_CHUNK
    n_chunks = uf.shape[1] // cc
    dirs = ((uf, wf, qdf, kdf, qkf, glf, of_ref), (ub, wb, qdb, kdb, qkb, glb, ob_ref))
    probs = [(d, h, slice(h * HEAD, (h + 1) * HEAD)) for d in range(2) for h in range(DN_HEADS)]
    for step in range(n_chunks):
        ch = (step, n_chunks - 1 - step)
        rows = tuple(slice(c * cc, (c + 1) * cc) for c in ch)
        ws = [_dg(jnp.concatenate([dirs[d][1][0, rows[d], hs], dirs[d][2][0, rows[d], hs]], axis=0),
                  s_ref[d, h].astype(BF16)) for d, h, hs in probs]
        v_new = [(dirs[d][0][0, rows[d], hs] - t[:cc]).astype(BF16) for t, (d, h, hs) in zip(ws, probs)]
        outs = [t[cc:] + _dg(dirs[d][4][0, h, rows[d], :], vn) for t, vn, (d, h, hs) in zip(ws, v_new, probs)]
        upd = [_dg(dirs[d][3][0, rows[d], hs], vn, TN) for vn, (d, h, hs) in zip(v_new, probs)]
        for o, ds, (d, h, hs) in zip(outs, upd, probs):
            dirs[d][6][rows[d], hs] = o
            s_ref[d, h] = s_ref[d, h] * dirs[d][5][0, ch[d], h:h + 1, :] + ds


def _dn_scan(u, w, qd, kd, qk, gl, lat):
    t = u.shape[1]
    tb = TOK
    cpb = tb // DN_CHUNK
    n, n_lat = t // tb, lat // tb
    n_ctx = n - n_lat
    bf = lambda i: jnp.where(i < n_ctx, n_lat + i, i - n_ctx)
    bb = lambda i: n - 1 - i
    specs = []
    for d, blk in ((0, bf), (1, bb)):
        big = pl.BlockSpec((1, tb, MIX), lambda i, d=d, blk=blk: (d, blk(i), 0))
        specs += [big, big, big, big,
                  pl.BlockSpec((1, DN_HEADS, tb, DN_CHUNK), lambda i, d=d, blk=blk: (d, 0, blk(i), 0)),
                  pl.BlockSpec((1, cpb, 8, HEAD), lambda i, d=d, blk=blk: (d, blk(i), 0, 0))]
    return pl.pallas_call(
        _dn_scan_kernel,
        grid=(n,),
        in_specs=specs,
        out_specs=[pl.BlockSpec((tb, MIX), lambda i: (bf(i), 0)), pl.BlockSpec((tb, MIX), lambda i: (bb(i), 0))],
        out_shape=[jax.ShapeDtypeStruct((t, MIX), F32)] * 2,
        scratch_shapes=[pltpu.VMEM((2, DN_HEADS, HEAD, HEAD), F32)],
        compiler_params=_params("arbitrary"),
        name="dn_scan",
    )(u, w, qd, kd, qk, gl, u, w, qd, kd, qk, gl)


def _dn_out_kernel(of_ref, ob_ref, z_ref, nw_ref, o_ref):
    for h in range(DN_HEADS):
        hs = slice(h * HEAD, (h + 1) * HEAD)
        o = of_ref[:, hs] + ob_ref[:, hs]
        o_ref[:, hs] = (_rms(o, nw_ref[...], HEAD) * _silu(z_ref[:, hs].astype(F32))).astype(BF16)


def _dn_out(o_f, o_b, p, norm_w):
    t = p.shape[0]
    tb = _row_tile(t)
    row = lambda c: pl.BlockSpec((tb, MIX), lambda i: (i, c))
    return pl.pallas_call(
        _dn_out_kernel,
        grid=(t // tb,),
        in_specs=[row(0), row(0), row(P_CZ // MIX), pl.BlockSpec((1, HEAD), lambda i: (0, 0))],
        out_specs=row(0),
        out_shape=jax.ShapeDtypeStruct((t, MIX), BF16),
        compiler_params=_params("parallel"),
        name="dn_out",
    )(o_f, o_b, p, norm_w.reshape(1, HEAD))


def _arrange_w_in(w_in):
    pad = jnp.zeros(w_in.shape[:2] + (P_W - w_in.shape[2],), w_in.dtype)
    tail = [w_in[..., 5328:5840],
            w_in[..., 5008:5264],
            w_in[..., 4624:5008],
            w_in[..., 5264:5328],
            w_in[..., 4608:4624],
            pad]
    return w_in[..., :P_MAIN].astype(BF16), jnp.concatenate(tail, axis=-1).astype(BF16)


def _arrange_mla_wq(w_uq):
    l, r, _ = w_uq.shape
    w = w_uq.reshape(l, r, MLA_HEADS, MLA_QK)
    w = jnp.pad(w, ((0, 0), (0, 0), (0, 0), (0, 2 * HEAD - MLA_QK)))
    return w.reshape(l, r, MLA_HEADS * 2 * HEAD).astype(BF16)


def _pad_qk_norm(w):
    return jnp.pad(w, ((0, 0), (0, 2 * HEAD - MLA_QK)))[:, None, :]


def _small_lane_row(vals):
    l = vals.shape[0]
    row = jnp.zeros((l, HEAD), F32)
    for d in range(2):
        lo = SM_AB + d * 2 * DN_HEADS + DN_HEADS
        row = row.at[:, lo:lo + DN_HEADS].set(vals[:, d, :])
    return row[:, None, :]


def _arrange_lru_gates(w_a, b_a, w_x, b_x):
    w = jnp.concatenate([w_a[:, 0], w_x[:, 0], w_a[:, 1], w_x[:, 1]], axis=-1)
    l = b_a.shape[0]
    blk = lambda b, d: b[:, d].reshape(l, LRU_BLOCKS, 1, HEAD)
    b = jnp.concatenate([blk(b_a, 0), blk(b_x, 0), blk(b_a, 1), blk(b_x, 1)], axis=-1)
    return w.astype(BF16), b


def _rope_tables(lat, ctx):
    t = jnp.arange(lat)
    rows, cols = (t // GRID_W).astype(F32), (t % GRID_W).astype(F32)
    lane = jnp.arange(HEAD)

    def table(half, width):
        inv_freq = ROPE_THETA ** (-jnp.arange(half, dtype=F32) / half)
        pos = jnp.where((lane // (2 * half))[None, :] == 0, rows[:, None], cols[:, None])
        ang = pos * inv_freq[lane % half][None, :]
        live = (lane < width)[None, :]
        cos = jnp.where(live, jnp.cos(ang), 0.0)
        sin = jnp.where(live, jnp.sin(ang), 0.0) * jnp.where((lane % (2 * half)) < half, -1.0, 1.0)[None, :]
        cos_c = jnp.broadcast_to(jnp.where(live, 1.0, 0.0), (ctx, HEAD))
        return (jnp.concatenate([cos, cos_c], axis=0).astype(F32),
                jnp.concatenate([sin, jnp.zeros((ctx, HEAD), F32)], axis=0).astype(F32))

    return table(32, HEAD), table(16, MLA_ROPE)


def kernel(x, c, ctx, c_ctx, norm_w, w_ada, b_ada, w_in, w_out, attn_q_norm, attn_k_norm, lru_conv_w, lru_conv_b, lru_w_a, lru_b_a, lru_w_x, lru_b_x, lru_lambda, dn_conv_w, dn_a_log, dn_dt_bias, dn_norm_w, mla_q_norm, mla_kv_norm, mla_w_uq, mla_w_ukv, mla_q_qk_norm, mla_k_qk_norm):
    assert x.shape[0] == 1 and ctx.shape[1] == TOK and x.shape[1] % (8 * TOK) == 0
    lat, n_ctx = x.shape[1], ctx.shape[1]
    depth = w_in.shape[0]

    xs = jnp.concatenate([x[0], ctx[0]], axis=0)
    mod = _ada(jnp.stack([c[0], c_ctx], axis=1), w_ada, b_ada)
    (cos_a, sin_a), (cos_m, sin_m) = _rope_tables(lat, n_ctx)
    w_main, w_tail = _arrange_w_in(w_in)
    w_out_r = w_out.reshape(depth, 4, MIX, D_MODEL).astype(BF16)
    wq_r = _arrange_mla_wq(mla_w_uq)
    wkv_r = mla_w_ukv.astype(BF16)
    qn_r, kn_r = _pad_qk_norm(mla_q_qk_norm), _pad_qk_norm(mla_k_qk_norm)
    alog_r, dtb_r = _small_lane_row(dn_a_log), _small_lane_row(dn_dt_bias)
    lru_w, lru_b = _arrange_lru_gates(lru_w_a, lru_b_a, lru_w_x, lru_b_x)

    for l in range(depth):
        p = _inproj(xs, norm_w[l], mod[l, 0], mod[l, 1], w_main, w_tail, l, lat)
        qa, ka = _gqa_prep(p, cos_a, sin_a, attn_q_norm[l], attn_k_norm[l])
        bound_a = jnp.max(jnp.abs(attn_q_norm[l])) * jnp.max(jnp.abs(attn_k_norm[l])) * (HEAD ** 0.5 * LOG2E)
        y_a = _flash(qa, ka, p, P_AV, p, P_AZ, A_HEADS, A_KV_HEADS, HEAD, lat, bound_a, "gqa_attn")
        y_b = _lru(p, lru_conv_w[l], lru_conv_b[l], lru_w[l], lru_b[l], lru_lambda[l], lat)
        u, w, qd, kd, qk, gl = _dn_prep(p, dn_conv_w[l], alog_r[l], dtb_r[l], lat)
        o_f, o_b = _dn_scan(u, w, qd, kd, qk, gl, lat)
        y_c = _dn_out(o_f, o_b, p, dn_norm_w[l])
        qm, km, vm = _mla_prep(p, cos_m, sin_m, mla_q_norm[l], mla_kv_norm[l], wq_r[l], wkv_r[l],
                               qn_r[l], kn_r[l])
        bound_d = (jnp.max(jnp.abs(mla_q_qk_norm[l])) * jnp.max(jnp.abs(mla_k_qk_norm[l]))
                   * (MLA_QK ** 0.5 * LOG2E))
        y_d = _flash(qm, km, vm, 0, p, P_DZ, MLA_HEADS, MLA_HEADS, 2 * HEAD, lat, bound_d, "mla_attn")
        last = l == depth - 1
        xs = _outproj((y_a, y_b, y_c, y_d), w_out_r, l, xs, mod[l, 2], lat, lat if last else lat + n_ctx)
    return xs[None]
```

```python
import functools
import math

import jax
import jax.numpy as jnp
from jax import lax
from jax.experimental import pallas as pl
from jax.experimental.pallas import tpu as pltpu

F32 = jnp.float32
BF16 = jnp.bfloat16

D_MODEL = 2048
DEPTH = 4
GRID_W = 64
ROPE_THETA = 10000.0
NORM_EPS = 1e-6
HEAD = 128
A_HEADS, A_KV_HEADS = 4, 2
LRU_WIDTH, LRU_BLOCKS, LRU_C = 512, 4, 8.0
DN_HEADS, DN_CHUNK = 4, 64
MLA_HEADS, MLA_Q_RANK, MLA_KV_RANK, MLA_NOPE, MLA_ROPE = 4, 384, 256, 128, 64
MLA_QK = MLA_NOPE + MLA_ROPE
MIX = 512

P_AQ, P_AK, P_AV, P_AZ = 0, 512, 768, 1024
P_BX, P_BZ = 1536, 2048
P_CQ, P_CK, P_CV, P_CZ = 2560, 3072, 3584, 4096
P_DZ, P_DCKV, P_DCQ = 4608, 5120, 5376
P_SM = 5760
P_W = 6144
P_MAIN = 4608
W_TILE = 768
SM_AB = MLA_ROPE

TOK = 256
HALO = 16
VMEM_LIMIT = 56 * 1024 * 1024
LOG2E = math.log2(math.e)
MAX_EXP2_LOGIT = 60.0

NN = ((1,), (0,))
NT = ((1,), (1,))
TN = ((0,), (0,))


def _dg(a, b, dims=NN):
    return lax.dot_general(a, b, (dims, ((), ())), preferred_element_type=F32)


def _dot_exact_lhs(m, b):
    b0 = b.astype(BF16)
    r1 = b - b0.astype(F32)
    b1 = r1.astype(BF16)
    b2 = (r1 - b1.astype(F32)).astype(BF16)
    return _dg(m, b0) + (_dg(m, b1) + _dg(m, b2))


def _sigmoid(x):
    return 1.0 / (1.0 + jnp.exp(-x))


def _silu(x):
    return x * _sigmoid(x)


def _softplus(x):
    return jnp.maximum(x, 0.0) + jnp.log(1.0 + jnp.exp(-jnp.abs(x)))


def _rms(x, w, n):
    return x * lax.rsqrt(jnp.sum(x * x, axis=-1, keepdims=True) * (1.0 / n) + NORM_EPS) * w


def _rope(x, cos, sin_signed, half):
    lane = lax.broadcasted_iota(jnp.int32, x.shape, 1)
    first = (lane % (2 * half)) < half
    rot = jnp.where(first, pltpu.roll(x, HEAD - half, 1), pltpu.roll(x, half, 1))
    return x * cos + rot * sin_signed


def _params(*sem):
    return pltpu.CompilerParams(dimension_semantics=sem, vmem_limit_bytes=VMEM_LIMIT)


def _row_tile(rows):
    for t in (768, 512, 256):
        if rows % t == 0:
            return t
    raise ValueError(f"row count {rows} is not a multiple of {TOK}")


def _ada_kernel(c_ref, w_ref, b_ref, o_ref):
    s = _silu(c_ref[...])
    w = w_ref[0]
    r0 = jnp.sum(s[:, 0:1] * w, axis=0, keepdims=True)
    r1 = jnp.sum(s[:, 1:2] * w, axis=0, keepdims=True)
    o_ref[0, 0] = jnp.concatenate([r0, r1], axis=0) + b_ref[0]


def _ada(c_cols, w_ada, b_ada):
    depth, d, _ = w_ada.shape
    tn = 512
    per = d // tn
    return pl.pallas_call(
        _ada_kernel,
        grid=(depth, 3 * per),
        in_specs=[
            pl.BlockSpec((d, 2), lambda l, j: (0, 0)),
            pl.BlockSpec((1, d, tn), lambda l, j: (l, 0, j)),
            pl.BlockSpec((1, 1, tn), lambda l, j: (l, 0, j)),
        ],
        out_specs=pl.BlockSpec((1, 1, 2, tn), lambda l, j: (l, j // per, 0, j % per)),
        out_shape=jax.ShapeDtypeStruct((depth, 3, 2, d), F32),
        compiler_params=_params("parallel", "parallel"),
        name="ada",
    )(c_cols, w_ada, b_ada.reshape(depth, 1, 3 * d))


def _inproj_kernel(x_ref, nw_ref, shift_ref, scale_ref, wm_ref, wt_ref, o_ref, h_ref, *, lat, n_main):
    i = pl.program_id(0)
    tm = x_ref.shape[0]

    @pl.when(pl.program_id(1) == 0)
    def _():
        rc = 32

        def chunk(ci, _):
            rows = pl.ds(pl.multiple_of(ci * rc, rc), rc)
            y = _rms(x_ref[rows, :], nw_ref[...], D_MODEL)
            is_ctx = i * tm + ci * rc + lax.broadcasted_iota(jnp.int32, (rc, 1), 0) >= lat
            scale = jnp.where(is_ctx, scale_ref[1:2, :], scale_ref[0:1, :])
            shift = jnp.where(is_ctx, shift_ref[1:2, :], shift_ref[0:1, :])
            h_ref[rows, :] = (y * (1.0 + scale) + shift).astype(BF16)
            return 0

        lax.fori_loop(0, tm // rc, chunk, 0)

    @pl.when(pl.program_id(1) < n_main)
    def _():
        o_ref[...] = _dg(h_ref[...], wm_ref[...]).astype(o_ref.dtype)

    @pl.when(pl.program_id(1) >= n_main)
    def _():
        o_ref[...] = _dg(h_ref[...], wt_ref[...]).astype(o_ref.dtype)


def _inproj(xs, norm_w, shift, scale, w_main, w_tail, layer, lat):
    t, d = xs.shape
    tm, tn = (1408 if t % 1408 == 0 else _row_tile(t)), W_TILE
    n_main = P_MAIN // tn
    return pl.pallas_call(
        functools.partial(_inproj_kernel, lat=lat, n_main=n_main),
        grid=(t // tm, P_W // tn),
        in_specs=[
            pl.BlockSpec((tm, d), lambda i, j: (i, 0)),
            pl.BlockSpec((1, d), lambda i, j: (0, 0)),
            pl.BlockSpec((2, d), lambda i, j: (0, 0)),
            pl.BlockSpec((2, d), lambda i, j: (0, 0)),
            pl.BlockSpec((None, None, d, tn), lambda i, j: (layer, jnp.minimum(j, n_main - 1), 0, 0)),
            pl.BlockSpec((None, None, d, tn), lambda i, j: (layer, jnp.maximum(j - n_main, 0), 0, 0)),
        ],
        out_specs=pl.BlockSpec((tm, tn), lambda i, j: (i, j)),
        out_shape=jax.ShapeDtypeStruct((t, P_W), BF16),
        scratch_shapes=[pltpu.VMEM((tm, d), BF16)],
        compiler_params=_params("parallel", "arbitrary"),
        name="inproj",
    )(xs, norm_w.reshape(1, d), shift, scale, w_main, w_tail)


def _outproj_kernel(ya_ref, yb_ref, yc_ref, yd_ref, w_ref, x_ref, g_ref, o_ref, *, lat):
    i = pl.program_id(0)
    tm = x_ref.shape[0]
    acc = _dg(ya_ref[...], w_ref[0])
    acc += _dg(yb_ref[...], w_ref[1])
    acc += _dg(yc_ref[...], w_ref[2])
    acc += _dg(yd_ref[...], w_ref[3])
    row = i * tm + lax.broadcasted_iota(jnp.int32, (tm, 1), 0)
    gate = jnp.where(row >= lat, g_ref[1:2, :], g_ref[0:1, :])
    o_ref[...] = x_ref[...] + gate * acc


def _outproj(ys, w_out, layer, xs, gate, lat, out_rows):
    d = xs.shape[1]
    tm, tn = _row_tile(out_rows), 1024
    yspec = pl.BlockSpec((tm, MIX), lambda i, j: (i, 0))
    return pl.pallas_call(
        functools.partial(_outproj_kernel, lat=lat),
        grid=(out_rows // tm, d // tn),
        in_specs=[yspec, yspec, yspec, yspec,
                  pl.BlockSpec((None, 4, MIX, tn), lambda i, j: (layer, 0, 0, j)),
                  pl.BlockSpec((tm, tn), lambda i, j: (i, j)),
                  pl.BlockSpec((2, tn), lambda i, j: (0, j))],
        out_specs=pl.BlockSpec((tm, tn), lambda i, j: (i, j)),
        out_shape=jax.ShapeDtypeStruct((out_rows, d), F32),
        compiler_params=_params("parallel", "parallel"),
        name="outproj",
    )(*ys, w_out, xs, gate)


def _flash_kernel(q_ref, k_ref, v_ref, z_ref, o_ref, *, n_loop, tk, tail, dq, shared_kv, bounded, out_row0=0):
    tq = q_ref.shape[0]
    nh = q_ref.shape[1] // dq
    if out_row0:
        o_ref[0:out_row0, :] = jnp.zeros((out_row0, o_ref.shape[1]), o_ref.dtype)

    def scores(c, rows):
        kc = 0 if shared_kv else c
        s = _dg(q_ref[:, c * dq:(c + 1) * dq], k_ref[rows, kc * dq:(kc + 1) * dq], NT)
        return s, v_ref[rows, kc * HEAD:(kc + 1) * HEAD]

    def attend_bounded(c, carry, rows):
        l, acc = carry
        s, v = scores(c, rows)
        p = jnp.exp2(s)
        for j in range(s.shape[1] // HEAD):
            l = l + p[:, j * HEAD:(j + 1) * HEAD]
        return l, acc + _dg(p.astype(BF16), v)

    def attend_online(c, carry, rows):
        m, l, acc = carry
        s, v = scores(c, rows)
        m_new = jnp.maximum(m, jnp.max(s, axis=-1, keepdims=True))
        alpha = jnp.exp2(m - m_new)
        p = jnp.exp2(s - m_new)
        l = alpha * l + jnp.sum(p, axis=-1, keepdims=True)
        return m_new, l, alpha * acc + _dg(p.astype(BF16), v)

    zeros = jnp.zeros((tq, HEAD), F32)
    if bounded:
        attend, init = attend_bounded, (zeros, zeros)
    else:
        attend, init = attend_online, (jnp.full((tq, 1), -1e30, F32), jnp.zeros((tq, 1), F32), zeros)

    def step(kb, carries):
        rows = pl.ds(pl.multiple_of(kb * tk, tk), tk)
        return tuple(attend(c, carries[c], rows) for c in range(nh))

    carries = (init,) * nh
    if n_loop:
        carries = lax.fori_loop(0, n_loop, step, carries, unroll=2)
    for c in range(nh):
        carry = attend(c, carries[c], pl.ds(tail[0], tail[1]))
        l, acc = carry[-2], carry[-1]
        cs = slice(c * HEAD, (c + 1) * HEAD)
        gate = _silu(z_ref[:, cs].astype(F32))
        o_ref[out_row0:out_row0 + tq, cs] = (acc / jnp.sum(l, axis=-1, keepdims=True) * gate).astype(o_ref.dtype)


def _flash_lat_kernel(q_ref, k_ref, v_ref, z_ref, y_ref, o_ref, **kw):
    _flash_kernel(q_ref, k_ref, v_ref, z_ref, o_ref, **kw)


def _flash_t_kernel(q_ref, k_ref, vt_ref, z_ref, y_ref, o_ref, *, n_loop, tk, n_tail, dq, shared_kv):
    tq = q_ref.shape[0]
    nh = q_ref.shape[1] // dq
    sub = tk // TOK

    def attend(c, carry, row0, tile0, n_tiles):
        l, acc = carry
        kc = 0 if shared_kv else c
        keys = n_tiles * TOK
        s = _dg(k_ref[pl.ds(row0, keys), kc * dq:(kc + 1) * dq], q_ref[:, c * dq:(c + 1) * dq], NT)
        p = jnp.exp2(s)
        for r in range(keys // 8):
            l = l + p[r * 8:(r + 1) * 8, :]
        p = p.astype(BF16)
        for j in range(n_tiles):
            acc = acc + _dg(vt_ref[tile0 + j, kc * HEAD:(kc + 1) * HEAD, :], p[j * TOK:(j + 1) * TOK, :])
        return l, acc

    init = (jnp.zeros((8, tq), F32), jnp.zeros((HEAD, tq), F32))

    def step(kb, carries):
        return tuple(attend(c, carries[c], pl.multiple_of(kb * tk, tk), kb * sub, sub) for c in range(nh))

    carries = lax.fori_loop(0, n_loop, step, (init,) * nh, unroll=2)
    for c in range(nh):
        l, acc = attend(c, carries[c], n_loop * tk, n_loop * sub, n_tail)
        cs = slice(c * HEAD, (c + 1) * HEAD)
        out = (acc / jnp.sum(l, axis=0, keepdims=True)).T
        o_ref[:, cs] = (out * _silu(z_ref[:, cs].astype(F32))).astype(o_ref.dtype)


def _flash(q, k, v, v_col, vt, p, z_col, heads, kv_heads, dq, lat, logit_bound, name):
    t = q.shape[0]
    ctx = t - lat
    nh = 2
    shared_kv = heads // kv_heads == nh
    nk = 1 if shared_kv else nh
    tq = 512 if lat % 512 == 0 else TOK
    tk = 2048 if lat % 4096 == 0 else TOK
    zb = z_col // (nh * HEAD)
    vb = v_col // (nk * HEAD)
    cb = lat // ctx
    out_shape = jax.ShapeDtypeStruct((t, heads * HEAD), BF16)

    def call(bounded):
        common = dict(tk=tk, dq=dq, shared_kv=shared_kv, bounded=bounded)
        suffix = "_bounded" if bounded else "_online"
        y = pl.pallas_call(
            functools.partial(_flash_kernel, n_loop=0, tail=(0, ctx), out_row0=lat, **common),
            grid=(heads // nh,),
            in_specs=[
                pl.BlockSpec((ctx, nh * dq), lambda g: (cb, g)),
                pl.BlockSpec((ctx, nk * dq), lambda g: (cb, g)),
                pl.BlockSpec((ctx, nk * HEAD), lambda g: (cb, vb + g)),
                pl.BlockSpec((ctx, nh * HEAD), lambda g: (cb, zb + g)),
            ],
            out_specs=pl.BlockSpec((t, nh * HEAD), lambda g: (0, g)),
            out_shape=out_shape,
            compiler_params=_params("parallel"),
            name=name + "_ctx" + suffix,
        )(q, k, v, p)
        if bounded:
            body = functools.partial(_flash_t_kernel, n_loop=lat // tk, tk=tk, n_tail=ctx // TOK, dq=dq,
                                     shared_kv=shared_kv)
            v_spec, v_arg = pl.BlockSpec((t // TOK, nk * HEAD, TOK), lambda g, i: (0, g, 0)), vt
        else:
            body = functools.partial(_flash_lat_kernel, n_loop=lat // tk, tail=(lat, ctx), **common)
            v_spec, v_arg = pl.BlockSpec((t, nk * HEAD), lambda g, i: (0, vb + g)), v
        return pl.pallas_call(
            body,
            grid=(heads // nh, lat // tq),
            in_specs=[
                pl.BlockSpec((tq, nh * dq), lambda g, i: (i, g)),
                pl.BlockSpec((t, nk * dq), lambda g, i: (0, g)),
                v_spec,
                pl.BlockSpec((tq, nh * HEAD), lambda g, i: (i, zb + g)),
                pl.BlockSpec(memory_space=pl.ANY),
            ],
            out_specs=pl.BlockSpec((tq, nh * HEAD), lambda g, i: (i, g)),
            out_shape=out_shape,
            input_output_aliases={4: 0},
            compiler_params=_params("parallel", "parallel"),
            name=name + suffix,
        )(q, k, v_arg, p, y)

    return lax.cond(logit_bound <= MAX_EXP2_LOGIT, lambda: call(True), lambda: call(False))


def _gqa_prep_kernel(q_ref, k_ref, v_ref, cos_ref, sin_ref, qn_ref, kn_ref, oq_ref, ok_ref, ovt_ref):
    cos, sin = cos_ref[...], sin_ref[...]
    scale = HEAD ** -0.5 * LOG2E
    for h in range(A_HEADS):
        c = slice(h * HEAD, (h + 1) * HEAD)
        oq_ref[:, c] = (_rope(_rms(q_ref[:, c].astype(F32), qn_ref[...], HEAD), cos, sin, 32) * scale).astype(BF16)
    for g in range(A_KV_HEADS):
        c = slice(g * HEAD, (g + 1) * HEAD)
        ok_ref[:, c] = _rope(_rms(k_ref[:, c].astype(F32), kn_ref[...], HEAD), cos, sin, 32).astype(BF16)
    for j in range(v_ref.shape[0] // TOK):
        ovt_ref[j] = v_ref[j * TOK:(j + 1) * TOK, :].astype(F32).T.astype(BF16)


def _gqa_prep(p, cos, sin, qn, kn):
    t = p.shape[0]
    tb = _row_tile(t)
    qw, kw = A_HEADS * HEAD, A_KV_HEADS * HEAD
    row = lambda w, c: pl.BlockSpec((tb, w), lambda i: (i, c))
    vec = pl.BlockSpec((1, HEAD), lambda i: (0, 0))
    return pl.pallas_call(
        _gqa_prep_kernel,
        grid=(t // tb,),
        in_specs=[row(qw, P_AQ // qw), row(kw, P_AK // kw), row(kw, P_AV // kw), row(HEAD, 0), row(HEAD, 0), vec, vec],
        out_specs=[row(qw, 0), row(kw, 0), pl.BlockSpec((tb // TOK, kw, TOK), lambda i: (i, 0, 0))],
        out_shape=[jax.ShapeDtypeStruct((t, qw), BF16), jax.ShapeDtypeStruct((t, kw), BF16),
                   jax.ShapeDtypeStruct((t // TOK, kw, TOK), BF16)],
        compiler_params=_params("parallel"),
        name="gqa_prep",
    )(p, p, p, cos, sin, qn.reshape(1, HEAD), kn.reshape(1, HEAD))


def _mla_prep_kernel(cq_ref, ckv_ref, sm_ref, cos_ref, sin_ref, qnw_ref, kvnw_ref, wq_ref, wkv_ref,
                     qn_ref, kn_ref, oq_ref, ok_ref, ov_ref, ovt_ref):
    cos, sin = cos_ref[...], sin_ref[...]
    scale = MLA_QK ** -0.5 * LOG2E
    qn, kn = qn_ref[...], kn_ref[...]
    q = _dg(_rms(cq_ref[...].astype(F32), qnw_ref[...], MLA_Q_RANK).astype(BF16), wq_ref[...])
    kv = _dg(_rms(ckv_ref[...].astype(F32), kvnw_ref[...], MLA_KV_RANK).astype(BF16), wkv_ref[...])
    lane = lax.broadcasted_iota(jnp.int32, sm_ref.shape, 1)
    kr = jnp.where(lane < MLA_ROPE, sm_ref[...].astype(F32), 0.0)
    kr_ss = jnp.sum(kr * kr, axis=-1, keepdims=True)
    for h in range(MLA_HEADS):
        lo = slice(2 * h * HEAD, (2 * h + 1) * HEAD)
        hi = slice((2 * h + 1) * HEAD, (2 * h + 2) * HEAD)
        q_nope, q_rope = q[:, lo], q[:, hi]
        r = lax.rsqrt((jnp.sum(q_nope * q_nope, axis=-1, keepdims=True)
                       + jnp.sum(q_rope * q_rope, axis=-1, keepdims=True)) * (1.0 / MLA_QK) + NORM_EPS)
        oq_ref[:, lo] = (q_nope * r * qn[:, :HEAD] * scale).astype(BF16)
        oq_ref[:, hi] = (_rope(q_rope * r * qn[:, HEAD:], cos, sin, 16) * scale).astype(BF16)
        k_nope, v = kv[:, lo], kv[:, hi]
        r = lax.rsqrt((jnp.sum(k_nope * k_nope, axis=-1, keepdims=True) + kr_ss) * (1.0 / MLA_QK) + NORM_EPS)
        ok_ref[:, lo] = (k_nope * r * kn[:, :HEAD]).astype(BF16)
        ok_ref[:, hi] = _rope(kr * r * kn[:, HEAD:], cos, sin, 16).astype(BF16)
        ov_ref[:, h * HEAD:(h + 1) * HEAD] = v.astype(BF16)
        for j in range(v.shape[0] // TOK):
            ovt_ref[j, h * HEAD:(h + 1) * HEAD, :] = v[j * TOK:(j + 1) * TOK, :].T.astype(BF16)


def _mla_prep(p, cos, sin, q_norm_w, kv_norm_w, wq, wkv, qn, kn):
    t = p.shape[0]
    tb = _row_tile(t)
    w2 = MLA_HEADS * 2 * HEAD
    row = lambda w, c: pl.BlockSpec((tb, w), lambda i: (i, c))
    full = lambda a: pl.BlockSpec(a.shape, lambda i: (0, 0))
    args = (q_norm_w.reshape(1, -1), kv_norm_w.reshape(1, -1), wq, wkv, qn, kn)
    return pl.pallas_call(
        _mla_prep_kernel,
        grid=(t // tb,),
        in_specs=[row(MLA_Q_RANK, P_DCQ // MLA_Q_RANK), row(MLA_KV_RANK, P_DCKV // MLA_KV_RANK),
                  row(HEAD, P_SM // HEAD), row(HEAD, 0), row(HEAD, 0)] + [full(a) for a in args],
        out_specs=[row(w2, 0), row(w2, 0), row(MLA_HEADS * HEAD, 0),
                   pl.BlockSpec((tb // TOK, MLA_HEADS * HEAD, TOK), lambda i: (i, 0, 0))],
        out_shape=[jax.ShapeDtypeStruct((t, w2), BF16), jax.ShapeDtypeStruct((t, w2), BF16),
                   jax.ShapeDtypeStruct((t, MLA_HEADS * HEAD), BF16),
                   jax.ShapeDtypeStruct((t // TOK, MLA_HEADS * HEAD, TOK), BF16)],
        compiler_params=_params("parallel"),
        name="mla_prep",
    )(p, p, p, cos, sin, *args)


def _lru_kernel(ux_ref, z_ref, cw_ref, cb_ref, w_ref, b_ref, lam_ref, o_ref,
                af_ref, ab_ref, hf_ref, hb_ref, cin_ref, *, lat):
    t = ux_ref.shape[0]
    ctx = t - lat
    nb, nb_lat = t // TOK, lat // TOK
    cw, cb = cw_ref[...], cb_ref[...]
    bias = b_ref[0]
    sp = (_softplus(-lam_ref[0:1, :]), _softplus(-lam_ref[1:2, :]))
    dirs = ((af_ref, hf_ref), (ab_ref, hb_ref))

    def gates_block(b, _):
        t0 = pl.multiple_of(b * TOK, TOK)
        rows = pl.ds(t0, TOK)
        first = jnp.logical_or(b == 0, b == nb_lat)
        last = jnp.logical_or(b == nb_lat - 1, b == nb - 1)
        prev = ux_ref[pl.ds(pl.multiple_of(jnp.maximum(t0 - HALO, 0), HALO), HALO), :].astype(F32)
        nxt = ux_ref[pl.ds(pl.multiple_of(jnp.minimum(t0 + TOK, t - HALO), HALO), HALO), :].astype(F32)
        xe = jnp.concatenate([jnp.where(first, 0.0, prev), ux_ref[rows, :].astype(F32),
                              jnp.where(last, 0.0, nxt)], axis=0)
        xs = cb + xe[HALO - 2:HALO - 2 + TOK] * cw[0:1]
        for j in range(1, 4):
            xs = xs + xe[HALO - 2 + j:HALO - 2 + j + TOK] * cw[j:j + 1]
        g = _dg(xs.astype(BF16), w_ref[0]) + bias
        for d, (a_ref, h_ref) in enumerate(dirs):
            r = _sigmoid(g[:, 2 * d * HEAD:(2 * d + 1) * HEAD])
            gi = _sigmoid(g[:, (2 * d + 1) * HEAD:(2 * d + 2) * HEAD])
            a = jnp.exp(-LRU_C * r * sp[d])
            a_ref[rows, :] = a
            h_ref[rows, :] = jnp.sqrt(1.0 - a * a) * (gi * xs)
        return 0

    lax.fori_loop(0, nb, gates_block, 0)

    def scan_region(r0, length, h0):
        ls = length // 8

        def step(s, carry):
            out = []
            for (a_ref, h_ref), idx, (h, ac) in zip(dirs, (s, ls - 1 - s), carry):
                rows = pl.ds(r0 + idx, 8, stride=ls)
                a = a_ref[rows, :]
                h = a * h + h_ref[rows, :]
                ac = ac * a
                h_ref[rows, :] = h
                a_ref[rows, :] = ac
                out.append((h, ac))
            return tuple(out)

        init = (jnp.zeros((8, HEAD), F32), jnp.ones((8, HEAD), F32))
        local = lax.fori_loop(0, ls, step, (init, init), unroll=4)
        carry_in, final = [], []
        for d, order in enumerate((range(8), range(7, -1, -1))):
            hk, ak = local[d]
            cin = [None] * 8
            c = h0[d]
            for k in order:
                cin[k] = c
                c = ak[k:k + 1] * c + hk[k:k + 1]
            carry_in.append(jnp.concatenate(cin, axis=0))
            final.append(c)
        return carry_in, final

    def emit(rows, cf, cb_):
        h = (hf_ref[rows, :] + af_ref[rows, :] * cf) + (hb_ref[rows, :] + ab_ref[rows, :] * cb_)
        o_ref[rows, :] = (h * _silu(z_ref[rows, :].astype(F32))).astype(BF16)

    zero = jnp.zeros((1, HEAD), F32)
    cin_ctx, s_ctx = scan_region(lat, ctx, (zero, zero))
    for k in range(8):
        emit(pl.ds(lat + k * (ctx // 8), ctx // 8), cin_ctx[0][k:k + 1], cin_ctx[1][k:k + 1])
    cin_lat, _ = scan_region(0, lat, s_ctx)
    cin_ref[0:8, :] = cin_lat[0]
    cin_ref[8:16, :] = cin_lat[1]
    blocks_per_segment = lat // 8 // TOK

    def out_block(b, _):
        k = b // blocks_per_segment
        emit(pl.ds(pl.multiple_of(b * TOK, TOK), TOK), cin_ref[pl.ds(k, 1), :], cin_ref[pl.ds(8 + k, 1), :])
        return 0

    lax.fori_loop(0, nb_lat, out_block, 0)


def _lru(p, conv_w, conv_b, w_gates, b_gates, lam, lat):
    t = p.shape[0]
    col = lambda c0: pl.BlockSpec((t, HEAD), lambda n: (0, c0 // HEAD + n))
    vec = lambda r: pl.BlockSpec((r, HEAD), lambda n: (0, n))
    return pl.pallas_call(
        functools.partial(_lru_kernel, lat=lat),
        grid=(LRU_BLOCKS,),
        in_specs=[col(P_BX), col(P_BZ), vec(4), vec(1),
                  pl.BlockSpec((1, HEAD, 4 * HEAD), lambda n: (n, 0, 0)),
                  pl.BlockSpec((1, 1, 4 * HEAD), lambda n: (n, 0, 0)), vec(2)],
        out_specs=pl.BlockSpec((t, HEAD), lambda n: (0, n)),
        out_shape=jax.ShapeDtypeStruct((t, LRU_WIDTH), BF16),
        scratch_shapes=[pltpu.VMEM((t, HEAD), F32)] * 4 + [pltpu.VMEM((16, HEAD), F32)],
        compiler_params=_params("parallel"),
        name="lru",
    )(p, p, conv_w, conv_b.reshape(1, -1), w_gates, b_gates, lam)


def _dn_prep_kernel(q_ref, k_ref, v_ref, qp_ref, kp_ref, vp_ref, qx_ref, kx_ref, vx_ref, cw_ref, sm_ref,
                    alog_ref, dtb_ref, u_ref, w_ref, qd_ref, kd_ref, qk_ref, gl_ref, *, nb_lat):
    i = pl.program_id(0)
    tb = q_ref.shape[0]
    cc = DN_CHUNK
    first = jnp.logical_or(i == 0, i == nb_lat)
    last = jnp.logical_or(i == nb_lat - 1, i == nb_lat)
    cw = cw_ref[...]

    def conv_silu(x_ref, prev_ref, next_ref, c0):
        xe = jnp.concatenate([jnp.where(first, 0.0, prev_ref[...].astype(F32)), x_ref[...].astype(F32),
                              jnp.where(last, 0.0, next_ref[...].astype(F32))], axis=0)
        y = xe[HALO - 2:HALO - 2 + tb] * cw[0:1, c0:c0 + MIX]
        for j in range(1, 4):
            y = y + xe[HALO - 2 + j:HALO - 2 + j + tb] * cw[j:j + 1, c0:c0 + MIX]
        return _silu(y)

    q = conv_silu(q_ref, qp_ref, qx_ref, 0)
    k = conv_silu(k_ref, kp_ref, kx_ref, MIX)
    v = conv_silu(v_ref, vp_ref, vx_ref, 2 * MIX)

    sm = sm_ref[...].astype(F32)
    beta_all = _sigmoid(sm)
    g_all = -jnp.exp(alog_ref[...]) * _softplus(sm + dtb_ref[...])

    r = lax.broadcasted_iota(jnp.int32, (tb, tb), 0)
    c = lax.broadcasted_iota(jnp.int32, (tb, tb), 1)
    same = (r // cc) == (c // cc)
    tri_f = jnp.where(same, jnp.where(c <= r, 1.0, 0.0), 0.0).astype(BF16)
    tri_b = jnp.where(same, jnp.where(c >= r, 1.0, 0.0), 0.0).astype(BF16)
    gcs = (_dot_exact_lhs(tri_f, g_all), _dot_exact_lhs(tri_b, g_all))

    ii = lax.broadcasted_iota(jnp.int32, (cc, cc), 0)
    jj = lax.broadcasted_iota(jnp.int32, (cc, cc), 1)
    eye = jnp.where(ii == jj, 1.0, 0.0)
    blk16 = (ii // 16) == (jj // 16)
    blk32 = (ii // 32) == (jj // 32)
    off32 = jnp.logical_and(blk32, jnp.logical_not(blk16))
    off64 = jnp.logical_not(blk32)
    b16 = lambda m: m.astype(BF16)
    incl = (ii >= jj, ii <= jj)
    strict = (ii > jj, ii < jj)

    gl_ref[...] = jnp.zeros_like(gl_ref)
    qs, ks = [], []
    for h in range(DN_HEADS):
        hs = slice(h * HEAD, (h + 1) * HEAD)
        qh, kh = q[:, hs], k[:, hs]
        qs.append(qh * lax.rsqrt(jnp.sum(qh * qh, axis=-1, keepdims=True) + NORM_EPS) * (HEAD ** -0.5))
        ks.append(kh * lax.rsqrt(jnp.sum(kh * kh, axis=-1, keepdims=True) + NORM_EPS))

    for ch in range(tb // cc):
        rows = slice(ch * cc, (ch + 1) * cc)
        gc = [gcs[d][rows] for d in range(2)]
        gct = [g.T for g in gc]
        kk, qk0 = [], []
        for h in range(DN_HEADS):
            k16 = b16(ks[h][rows])
            kk.append(_dg(k16, k16, NT))
            qk0.append(_dg(b16(qs[h][rows]), k16, NT))
        probs = []
        for d in range(2):
            end = cc - 1 if d == 0 else 0
            for h in range(DN_HEADS):
                hs = slice(h * HEAD, (h + 1) * HEAD)
                lb = SM_AB + d * 2 * DN_HEADS + h
                lg = lb + DN_HEADS
                beta = beta_all[rows, lb:lb + 1]
                gcol, grow = gc[d][:, lg:lg + 1], gct[d][lg:lg + 1, :]
                glast = gc[d][end:end + 1, lg:lg + 1]
                qh, kh, vh = qs[h][rows], ks[h][rows], v[rows, hs]
                decay = jnp.where(incl[d], jnp.exp(jnp.where(incl[d], gcol - grow, 0.0)), 0.0)
                eg = jnp.exp(gcol)
                kb = kh * beta
                lm = jnp.where(strict[d], beta * kk[h] * decay, 0.0)
                qk_ref[d, h, rows, :] = (qk0[h] * decay).astype(BF16)
                qd_ref[d, rows, hs] = (qh * eg).astype(BF16)
                kd_ref[d, rows, hs] = (kh * jnp.exp(glast - gcol)).astype(BF16)
                gl_ref[d, ch, h:h + 1, :] = jnp.broadcast_to(jnp.exp(glast), (1, HEAD))
                probs.append((d, hs, lm, jnp.concatenate([vh * beta, kb * eg], axis=1)))
        lms = [lm for _, _, lm, _ in probs]
        diag = [jnp.where(blk16, lm, 0.0) for lm in lms]
        xs = [eye - m for m in diag]
        pw = [b16(m) for m in diag]
        pw = [_dg(m, m) for m in pw]
        for _ in range(2):
            pw = [b16(m) for m in pw]
            xs = [x + _dg(b16(x), m) for x, m in zip(xs, pw)]
            pw = [_dg(m, m) for m in pw]
        xs = [x + _dg(b16(x), b16(m)) for x, m in zip(xs, pw)]
        for off in (off32, off64):
            x16 = [b16(x) for x in xs]
            cx = [_dg(b16(jnp.where(off, lm, 0.0)), x) for lm, x in zip(lms, x16)]
            xs = [x - _dg(xb, b16(c)) for x, xb, c in zip(xs, x16, cx)]
        sols = [_dg(b16(x), b16(rhs)) for x, (_, _, _, rhs) in zip(xs, probs)]
        for sol, (d, hs, _, _) in zip(sols, probs):
            u_ref[d, rows, hs] = sol[:, :HEAD]
            w_ref[d, rows, hs] = sol[:, HEAD:].astype(BF16)


def _dn_prep(p, conv_w, alog_row, dtb_row, lat):
    t = p.shape[0]
    tb = TOK
    nb, nb_lat = t // tb, lat // tb
    rh = tb // HALO
    blk = lambda c0: pl.BlockSpec((tb, MIX), lambda i: (i, c0 // MIX))
    prev = lambda c0: pl.BlockSpec((HALO, MIX), lambda i: (jnp.maximum(i * rh - 1, 0), c0 // MIX))
    nxt = lambda c0: pl.BlockSpec((HALO, MIX), lambda i: (jnp.minimum((i + 1) * rh, t // HALO - 1), c0 // MIX))
    cols = (P_CQ, P_CK, P_CV)
    big = pl.BlockSpec((2, tb, MIX), lambda i: (0, i, 0))
    big_shape = lambda dt: jax.ShapeDtypeStruct((2, t, MIX), dt)
    return pl.pallas_call(
        functools.partial(_dn_prep_kernel, nb_lat=nb_lat),
        grid=(nb,),
        in_specs=[blk(c0) for c0 in cols] + [prev(c0) for c0 in cols] + [nxt(c0) for c0 in cols] + [
            pl.BlockSpec((4, 3 * MIX), lambda i: (0, 0)),
            pl.BlockSpec((tb, HEAD), lambda i: (i, P_SM // HEAD)),
            pl.BlockSpec((1, HEAD), lambda i: (0, 0)),
            pl.BlockSpec((1, HEAD), lambda i: (0, 0))],
        out_specs=[big, big, big, big,
                   pl.BlockSpec((2, DN_HEADS, tb, DN_CHUNK), lambda i: (0, 0, i, 0)),
                   pl.BlockSpec((2, tb // DN_CHUNK, 8, HEAD), lambda i: (0, i, 0, 0))],
        out_shape=[big_shape(F32), big_shape(BF16), big_shape(BF16), big_shape(BF16),
                   jax.ShapeDtypeStruct((2, DN_HEADS, t, DN_CHUNK), BF16),
                   jax.ShapeDtypeStruct((2, t // DN_CHUNK, 8, HEAD), F32)],
        compiler_params=_params("parallel"),
        name="dn_prep",
    )(*([p] * 9), conv_w, p, alog_row, dtb_row)


def _dn_scan_kernel(uf, wf, qdf, kdf, qkf, glf, ub, wb, qdb, kdb, qkb, glb, of_ref, ob_ref, s_ref):
    @pl.when(pl.program_id(0) == 0)
    def _():
        s_ref[...] = jnp.zeros_like(s_ref)

    cc = DN_CHUNK
    n_chunks = uf.shape[1] // cc
    dirs = ((uf, wf, qdf, kdf, qkf, glf, of_ref), (ub, wb, qdb, kdb, qkb, glb, ob_ref))
    probs = [(d, h, slice(h * HEAD, (h + 1) * HEAD)) for d in range(2) for h in range(DN_HEADS)]
    for step in range(n_chunks):
        ch = (step, n_chunks - 1 - step)
        rows = tuple(slice(c * cc, (c + 1) * cc) for c in ch)
        ws = [_dg(jnp.concatenate([dirs[d][1][0, rows[d], hs], dirs[d][2][0, rows[d], hs]], axis=0),
                  s_ref[d, h].astype(BF16)) for d, h, hs in probs]
        v_new = [(dirs[d][0][0, rows[d], hs] - t[:cc]).astype(BF16) for t, (d, h, hs) in zip(ws, probs)]
        outs = [t[cc:] + _dg(dirs[d][4][0, h, rows[d], :], vn) for t, vn, (d, h, hs) in zip(ws, v_new, probs)]
        upd = [_dg(dirs[d][3][0, rows[d], hs], vn, TN) for vn, (d, h, hs) in zip(v_new, probs)]
        for o, ds, (d, h, hs) in zip(outs, upd, probs):
            dirs[d][6][rows[d], hs] = o
            s_ref[d, h] = s_ref[d, h] * dirs[d][5][0, ch[d], h:h + 1, :] + ds


def _dn_scan(u, w, qd, kd, qk, gl, lat):
    t = u.shape[1]
    tb = TOK
    cpb = tb // DN_CHUNK
    n, n_lat = t // tb, lat // tb
    n_ctx = n - n_lat
    bf = lambda i: jnp.where(i < n_ctx, n_lat + i, i - n_ctx)
    bb = lambda i: n - 1 - i
    specs = []
    for d, blk in ((0, bf), (1, bb)):
        big = pl.BlockSpec((1, tb, MIX), lambda i, d=d, blk=blk: (d, blk(i), 0))
        specs += [big, big, big, big,
                  pl.BlockSpec((1, DN_HEADS, tb, DN_CHUNK), lambda i, d=d, blk=blk: (d, 0, blk(i), 0)),
                  pl.BlockSpec((1, cpb, 8, HEAD), lambda i, d=d, blk=blk: (d, blk(i), 0, 0))]
    return pl.pallas_call(
        _dn_scan_kernel,
        grid=(n,),
        in_specs=specs,
        out_specs=[pl.BlockSpec((tb, MIX), lambda i: (bf(i), 0)), pl.BlockSpec((tb, MIX), lambda i: (bb(i), 0))],
        out_shape=[jax.ShapeDtypeStruct((t, MIX), F32)] * 2,
        scratch_shapes=[pltpu.VMEM((2, DN_HEADS, HEAD, HEAD), F32)],
        compiler_params=_params("arbitrary"),
        name="dn_scan",
    )(u, w, qd, kd, qk, gl, u, w, qd, kd, qk, gl)


def _dn_out_kernel(of_ref, ob_ref, z_ref, nw_ref, o_ref):
    for h in range(DN_HEADS):
        hs = slice(h * HEAD, (h + 1) * HEAD)
        o = of_ref[:, hs] + ob_ref[:, hs]
        o_ref[:, hs] = (_rms(o, nw_ref[...], HEAD) * _silu(z_ref[:, hs].astype(F32))).astype(BF16)


def _dn_out(o_f, o_b, p, norm_w):
    t = p.shape[0]
    tb = _row_tile(t)
    row = lambda c: pl.BlockSpec((tb, MIX), lambda i: (i, c))
    return pl.pallas_call(
        _dn_out_kernel,
        grid=(t // tb,),
        in_specs=[row(0), row(0), row(P_CZ // MIX), pl.BlockSpec((1, HEAD), lambda i: (0, 0))],
        out_specs=row(0),
        out_shape=jax.ShapeDtypeStruct((t, MIX), BF16),
        compiler_params=_params("parallel"),
        name="dn_out",
    )(o_f, o_b, p, norm_w.reshape(1, HEAD))


def _arrange_w_in(w_in):
    pad = jnp.zeros(w_in.shape[:2] + (P_W - w_in.shape[2],), w_in.dtype)
    tail = [w_in[..., 5328:5840],
            w_in[..., 5008:5264],
            w_in[..., 4624:5008],
            w_in[..., 5264:5328],
            w_in[..., 4608:4624],
            pad]
    def tiles(w):
        l, d, n = w.shape
        return w.reshape(l, d, n // W_TILE, W_TILE).transpose(0, 2, 1, 3).astype(BF16)

    return tiles(w_in[..., :P_MAIN]), tiles(jnp.concatenate(tail, axis=-1))


def _arrange_mla_wq(w_uq):
    l, r, _ = w_uq.shape
    w = w_uq.reshape(l, r, MLA_HEADS, MLA_QK)
    w = jnp.pad(w, ((0, 0), (0, 0), (0, 0), (0, 2 * HEAD - MLA_QK)))
    return w.reshape(l, r, MLA_HEADS * 2 * HEAD).astype(BF16)


def _pad_qk_norm(w):
    return jnp.pad(w, ((0, 0), (0, 2 * HEAD - MLA_QK)))[:, None, :]


def _small_lane_row(vals):
    l = vals.shape[0]
    row = jnp.zeros((l, HEAD), F32)
    for d in range(2):
        lo = SM_AB + d * 2 * DN_HEADS + DN_HEADS
        row = row.at[:, lo:lo + DN_HEADS].set(vals[:, d, :])
    return row[:, None, :]


def _arrange_lru_gates(w_a, b_a, w_x, b_x):
    w = jnp.concatenate([w_a[:, 0], w_x[:, 0], w_a[:, 1], w_x[:, 1]], axis=-1)
    l = b_a.shape[0]
    blk = lambda b, d: b[:, d].reshape(l, LRU_BLOCKS, 1, HEAD)
    b = jnp.concatenate([blk(b_a, 0), blk(b_x, 0), blk(b_a, 1), blk(b_x, 1)], axis=-1)
    return w.astype(BF16), b


def _rope_tables(lat, ctx):
    n_rows = lat // GRID_W
    pos = jnp.arange(max(n_rows, GRID_W), dtype=F32)
    lane = jnp.arange(HEAD)

    def table(half, width):
        inv_freq = ROPE_THETA ** (-jnp.arange(half, dtype=F32) / half)
        ang = pos[:, None] * inv_freq[lane % half][None, :]
        live = (lane < width)[None, :]
        by_row = ((lane // (2 * half)) == 0)[None, :]
        sign = jnp.where((lane % (2 * half)) < half, -1.0, 1.0)[None, :]

        def expand(f):
            per_row = jnp.repeat(f[:n_rows], GRID_W, axis=0)
            per_col = jnp.tile(f[:GRID_W], (n_rows, 1))
            return jnp.where(by_row, per_row, per_col)

        cos = jnp.where(live, expand(jnp.cos(ang)), 0.0)
        sin = jnp.where(live, expand(jnp.sin(ang)), 0.0) * sign
        cos_c = jnp.broadcast_to(jnp.where(live, 1.0, 0.0), (ctx, HEAD))
        return (jnp.concatenate([cos, cos_c], axis=0).astype(F32),
                jnp.concatenate([sin, jnp.zeros((ctx, HEAD), F32)], axis=0).astype(F32))

    return table(32, HEAD), table(16, MLA_ROPE)


def kernel(x, c, ctx, c_ctx, norm_w, w_ada, b_ada, w_in, w_out, attn_q_norm, attn_k_norm, lru_conv_w, lru_conv_b, lru_w_a, lru_b_a, lru_w_x, lru_b_x, lru_lambda, dn_conv_w, dn_a_log, dn_dt_bias, dn_norm_w, mla_q_norm, mla_kv_norm, mla_w_uq, mla_w_ukv, mla_q_qk_norm, mla_k_qk_norm):
    assert x.shape[0] == 1 and ctx.shape[1] == TOK and x.shape[1] % (8 * TOK) == 0
    lat, n_ctx = x.shape[1], ctx.shape[1]
    depth = w_in.shape[0]

    xs = jnp.concatenate([x[0], ctx[0]], axis=0)
    mod = _ada(jnp.stack([c[0], c_ctx], axis=1), w_ada, b_ada)
    (cos_a, sin_a), (cos_m, sin_m) = _rope_tables(lat, n_ctx)
    w_main, w_tail = _arrange_w_in(w_in)
    w_out_r = w_out.reshape(depth, 4, MIX, D_MODEL).astype(BF16)
    wq_r = _arrange_mla_wq(mla_w_uq)
    wkv_r = mla_w_ukv.astype(BF16)
    qn_r, kn_r = _pad_qk_norm(mla_q_qk_norm), _pad_qk_norm(mla_k_qk_norm)
    alog_r, dtb_r = _small_lane_row(dn_a_log), _small_lane_row(dn_dt_bias)
    lru_w, lru_b = _arrange_lru_gates(lru_w_a, lru_b_a, lru_w_x, lru_b_x)

    for l in range(depth):
        p = _inproj(xs, norm_w[l], mod[l, 0], mod[l, 1], w_main, w_tail, l, lat)
        qa, ka, vta = _gqa_prep(p, cos_a, sin_a, attn_q_norm[l], attn_k_norm[l])
        bound_a = jnp.max(jnp.abs(attn_q_norm[l])) * jnp.max(jnp.abs(attn_k_norm[l])) * (HEAD ** 0.5 * LOG2E)
        y_a = _flash(qa, ka, p, P_AV, vta, p, P_AZ, A_HEADS, A_KV_HEADS, HEAD, lat, bound_a, "gqa_attn")
        y_b = _lru(p, lru_conv_w[l], lru_conv_b[l], lru_w[l], lru_b[l], lru_lambda[l], lat)
        u, w, qd, kd, qk, gl = _dn_prep(p, dn_conv_w[l], alog_r[l], dtb_r[l], lat)
        o_f, o_b = _dn_scan(u, w, qd, kd, qk, gl, lat)
        y_c = _dn_out(o_f, o_b, p, dn_norm_w[l])
        qm, km, vm, vtm = _mla_prep(p, cos_m, sin_m, mla_q_norm[l], mla_kv_norm[l], wq_r[l], wkv_r[l],
                               qn_r[l], kn_r[l])
        bound_d = (jnp.max(jnp.abs(mla_q_qk_norm[l])) * jnp.max(jnp.abs(mla_k_qk_norm[l]))
                   * (MLA_QK ** 0.5 * LOG2E))
        y_d = _flash(qm, km, vm, 0, vtm, p, P_DZ, MLA_HEADS, MLA_HEADS, 2 * HEAD, lat, bound_d, "mla_attn")
        last = l == depth - 1
        xs = _outproj((y_a, y_b, y_c, y_d), w_out_r, l, xs, mod[l, 2], lat, lat if last else lat + n_ctx)
    return xs[None]
```

```python
import functools
import math

import jax
import jax.numpy as jnp
from jax import lax
from jax.experimental import pallas as pl
from jax.experimental.pallas import tpu as pltpu

F32 = jnp.float32
BF16 = jnp.bfloat16

D_MODEL = 2048
DEPTH = 4
GRID_W = 64
ROPE_THETA = 10000.0
NORM_EPS = 1e-6
HEAD = 128
A_HEADS, A_KV_HEADS = 4, 2
LRU_WIDTH, LRU_BLOCKS, LRU_C = 512, 4, 8.0
DN_HEADS, DN_CHUNK = 4, 64
MLA_HEADS, MLA_Q_RANK, MLA_KV_RANK, MLA_NOPE, MLA_ROPE = 4, 384, 256, 128, 64
MLA_QK = MLA_NOPE + MLA_ROPE
MIX = 512

P_AQ, P_AK, P_AV, P_AZ = 0, 512, 768, 1024
P_BX, P_BZ = 1536, 2048
P_CQ, P_CK, P_CV, P_CZ = 2560, 3072, 3584, 4096
P_DZ, P_DCKV, P_DCQ = 4608, 5120, 5376
P_SM = 5760
P_W = 6144
P_MAIN = 4608
W_TILE = 768
SM_AB = MLA_ROPE

TOK = 256
HALO = 16
VMEM_LIMIT = 56 * 1024 * 1024
LOG2E = math.log2(math.e)
NEG_BIG = -1e30
MAX_EXP2_LOGIT = 60.0

NN = ((1,), (0,))
NT = ((1,), (1,))
TN = ((0,), (0,))


def _dg(a, b, dims=NN):
    return lax.dot_general(a, b, (dims, ((), ())), preferred_element_type=F32)


def _dot_exact_lhs(m, b):
    b0 = b.astype(BF16)
    r1 = b - b0.astype(F32)
    b1 = r1.astype(BF16)
    b2 = (r1 - b1.astype(F32)).astype(BF16)
    return _dg(m, b0) + (_dg(m, b1) + _dg(m, b2))


def _sigmoid(x):
    return 1.0 / (1.0 + jnp.exp(-x))


def _silu(x):
    return x * _sigmoid(x)


def _softplus(x):
    return jnp.maximum(x, 0.0) + jnp.log(1.0 + jnp.exp(-jnp.abs(x)))


def _rms(x, w, n):
    return x * lax.rsqrt(jnp.sum(x * x, axis=-1, keepdims=True) * (1.0 / n) + NORM_EPS) * w


def _rope(x, cos, sin_signed, half):
    lane = lax.broadcasted_iota(jnp.int32, x.shape, 1)
    first = (lane % (2 * half)) < half
    rot = jnp.where(first, pltpu.roll(x, HEAD - half, 1), pltpu.roll(x, half, 1))
    return x * cos + rot * sin_signed


def _params(*sem):
    return pltpu.CompilerParams(dimension_semantics=sem, vmem_limit_bytes=VMEM_LIMIT)


def _row_tile(rows):
    for t in (768, 512, 256):
        if rows % t == 0:
            return t
    raise ValueError(f"row count {rows} is not a multiple of {TOK}")


def _ada_kernel(c_ref, w_ref, b_ref, o_ref):
    s = _silu(c_ref[...])
    w = w_ref[0]
    r0 = jnp.sum(s[:, 0:1] * w, axis=0, keepdims=True)
    r1 = jnp.sum(s[:, 1:2] * w, axis=0, keepdims=True)
    o_ref[0, 0] = jnp.concatenate([r0, r1], axis=0) + b_ref[0]


def _ada(c_cols, w_ada, b_ada):
    depth, d, _ = w_ada.shape
    tn = 512
    per = d // tn
    return pl.pallas_call(
        _ada_kernel,
        grid=(depth, 3 * per),
        in_specs=[
            pl.BlockSpec((d, 2), lambda l, j: (0, 0)),
            pl.BlockSpec((1, d, tn), lambda l, j: (l, 0, j)),
            pl.BlockSpec((1, 1, tn), lambda l, j: (l, 0, j)),
        ],
        out_specs=pl.BlockSpec((1, 1, 2, tn), lambda l, j: (l, j // per, 0, j % per)),
        out_shape=jax.ShapeDtypeStruct((depth, 3, 2, d), F32),
        compiler_params=_params("parallel", "parallel"),
        name="ada",
    )(c_cols, w_ada, b_ada.reshape(depth, 1, 3 * d))


def _inproj_kernel(x_ref, nw_ref, shift_ref, scale_ref, wm_ref, wt_ref, o_ref, h_ref, *, lat, n_main):
    i = pl.program_id(0)
    tm = x_ref.shape[0]

    @pl.when(pl.program_id(1) == 0)
    def _():
        rc = 32
        gain = nw_ref[...] * (1.0 + scale_ref[...])
        shift = shift_ref[...]

        def chunk(ci, _):
            rows = pl.ds(pl.multiple_of(ci * rc, rc), rc)
            is_ctx = i * tm + ci * rc >= lat
            g = jnp.where(is_ctx, gain[1:2, :], gain[0:1, :])
            b = jnp.where(is_ctx, shift[1:2, :], shift[0:1, :])
            x = x_ref[rows, :]
            r = lax.rsqrt(jnp.sum(x * x, axis=-1, keepdims=True) * (1.0 / D_MODEL) + NORM_EPS)
            h_ref[rows, :] = ((x * r) * g + b).astype(BF16)
            return 0

        lax.fori_loop(0, tm // rc, chunk, 0)

    @pl.when(pl.program_id(1) < n_main)
    def _():
        o_ref[...] = _dg(h_ref[...], wm_ref[...]).astype(o_ref.dtype)

    @pl.when(pl.program_id(1) >= n_main)
    def _():
        o_ref[...] = _dg(h_ref[...], wt_ref[...]).astype(o_ref.dtype)


def _inproj(xs, norm_w, shift, scale, w_main, w_tail, layer, lat):
    t, d = xs.shape
    tm, tn = (1408 if t % 1408 == 0 else _row_tile(t)), W_TILE
    n_main = P_MAIN // tn
    return pl.pallas_call(
        functools.partial(_inproj_kernel, lat=lat, n_main=n_main),
        grid=(t // tm, P_W // tn),
        in_specs=[
            pl.BlockSpec((tm, d), lambda i, j: (i, 0)),
            pl.BlockSpec((1, d), lambda i, j: (0, 0)),
            pl.BlockSpec((2, d), lambda i, j: (0, 0)),
            pl.BlockSpec((2, d), lambda i, j: (0, 0)),
            pl.BlockSpec((None, None, d, tn), lambda i, j: (layer, jnp.minimum(j, n_main - 1), 0, 0)),
            pl.BlockSpec((None, None, d, tn), lambda i, j: (layer, jnp.maximum(j - n_main, 0), 0, 0)),
        ],
        out_specs=pl.BlockSpec((tm, tn), lambda i, j: (i, j)),
        out_shape=jax.ShapeDtypeStruct((t, P_W), BF16),
        scratch_shapes=[pltpu.VMEM((tm, d), BF16)],
        compiler_params=_params("parallel", "arbitrary"),
        name="inproj",
    )(xs, norm_w.reshape(1, d), shift, scale, w_main, w_tail)


def _outproj_kernel(ya_ref, yb_ref, yc_ref, yd_ref, w_ref, x_ref, g_ref, o_ref, *, lat):
    i = pl.program_id(0)
    tm = x_ref.shape[0]
    acc = _dg(ya_ref[...], w_ref[0])
    acc += _dg(yb_ref[...], w_ref[1])
    acc += _dg(yc_ref[...], w_ref[2])
    acc += _dg(yd_ref[...], w_ref[3])
    row = i * tm + lax.broadcasted_iota(jnp.int32, (tm, 1), 0)
    gate = jnp.where(row >= lat, g_ref[1:2, :], g_ref[0:1, :])
    o_ref[...] = x_ref[...] + gate * acc


def _outproj(ys, w_out, layer, xs, gate, lat, out_rows):
    d = xs.shape[1]
    tm, tn = _row_tile(out_rows), 1024
    yspec = pl.BlockSpec((tm, MIX), lambda i, j: (i, 0))
    return pl.pallas_call(
        functools.partial(_outproj_kernel, lat=lat),
        grid=(out_rows // tm, d // tn),
        in_specs=[yspec, yspec, yspec, yspec,
                  pl.BlockSpec((None, 4, MIX, tn), lambda i, j: (layer, 0, 0, j)),
                  pl.BlockSpec((tm, tn), lambda i, j: (i, j)),
                  pl.BlockSpec((2, tn), lambda i, j: (0, j))],
        out_specs=pl.BlockSpec((tm, tn), lambda i, j: (i, j)),
        out_shape=jax.ShapeDtypeStruct((out_rows, d), F32),
        compiler_params=_params("parallel", "parallel"),
        name="outproj",
    )(*ys, w_out, xs, gate)


def _flash_kernel(q_ref, k_ref, v_ref, z_ref, o_ref, *, n_loop, tk, tail, dq, shared_kv, bounded, out_row0=0):
    tq = q_ref.shape[0]
    nh = q_ref.shape[1] // dq
    if out_row0:
        o_ref[0:out_row0, :] = jnp.zeros((out_row0, o_ref.shape[1]), o_ref.dtype)

    def scores(c, rows):
        kc = 0 if shared_kv else c
        s = _dg(q_ref[:, c * dq:(c + 1) * dq], k_ref[rows, kc * dq:(kc + 1) * dq], NT)
        return s, v_ref[rows, kc * HEAD:(kc + 1) * HEAD]

    def attend_bounded(c, carry, rows):
        l, acc = carry
        s, v = scores(c, rows)
        p = jnp.exp2(s)
        for j in range(s.shape[1] // HEAD):
            l = l + p[:, j * HEAD:(j + 1) * HEAD]
        return l, acc + _dg(p.astype(BF16), v)

    def attend_online(c, carry, rows):
        m, l, acc = carry
        s, v = scores(c, rows)
        m_new = jnp.maximum(m, jnp.max(s, axis=-1, keepdims=True))
        alpha = jnp.exp2(m - m_new)
        p = jnp.exp2(s - m_new)
        l = alpha * l + jnp.sum(p, axis=-1, keepdims=True)
        return m_new, l, alpha * acc + _dg(p.astype(BF16), v)

    zeros = jnp.zeros((tq, HEAD), F32)
    if bounded:
        attend, init = attend_bounded, (zeros, zeros)
    else:
        attend, init = attend_online, (jnp.full((tq, 1), -1e30, F32), jnp.zeros((tq, 1), F32), zeros)

    def step(kb, carries):
        rows = pl.ds(pl.multiple_of(kb * tk, tk), tk)
        return tuple(attend(c, carries[c], rows) for c in range(nh))

    carries = (init,) * nh
    if n_loop:
        carries = lax.fori_loop(0, n_loop, step, carries, unroll=2)
    for c in range(nh):
        carry = attend(c, carries[c], pl.ds(tail[0], tail[1]))
        l, acc = carry[-2], carry[-1]
        cs = slice(c * HEAD, (c + 1) * HEAD)
        gate = _silu(z_ref[:, cs].astype(F32))
        o_ref[out_row0:out_row0 + tq, cs] = (acc / jnp.sum(l, axis=-1, keepdims=True) * gate).astype(o_ref.dtype)


def _flash_lat_kernel(q_ref, k_ref, v_ref, z_ref, y_ref, o_ref, **kw):
    _flash_kernel(q_ref, k_ref, v_ref, z_ref, o_ref, **kw)


def _flash_t_kernel(q_ref, k_ref, vt_ref, z_ref, y_ref, o_ref, *, n_loop, tk, n_tail, dq, shared_kv):
    tq = q_ref.shape[0]
    nh = q_ref.shape[1] // dq
    sub = tk // TOK

    def attend(c, carry, row0, tile0, n_tiles):
        l, acc = carry
        kc = 0 if shared_kv else c
        keys = n_tiles * TOK
        s = _dg(k_ref[pl.ds(row0, keys), kc * dq:(kc + 1) * dq], q_ref[:, c * dq:(c + 1) * dq], NT)
        p = jnp.exp2(s)
        for r in range(keys // 8):
            l = l + p[r * 8:(r + 1) * 8, :]
        p = p.astype(BF16)
        for j in range(n_tiles):
            acc = acc + _dg(vt_ref[tile0 + j, kc * HEAD:(kc + 1) * HEAD, :], p[j * TOK:(j + 1) * TOK, :])
        return l, acc

    init = (jnp.zeros((8, tq), F32), jnp.zeros((HEAD, tq), F32))

    def step(kb, carries):
        return tuple(attend(c, carries[c], pl.multiple_of(kb * tk, tk), kb * sub, sub) for c in range(nh))

    carries = lax.fori_loop(0, n_loop, step, (init,) * nh, unroll=2)
    for c in range(nh):
        l, acc = attend(c, carries[c], n_loop * tk, n_loop * sub, n_tail)
        cs = slice(c * HEAD, (c + 1) * HEAD)
        out = (acc / jnp.sum(l, axis=0, keepdims=True)).T
        o_ref[:, cs] = (out * _silu(z_ref[:, cs].astype(F32))).astype(o_ref.dtype)


def _flash(q, k, v, v_col, vt, p, z_col, heads, kv_heads, dq, lat, logit_bound, name):
    t = q.shape[0]
    ctx = t - lat
    nh = 2
    shared_kv = heads // kv_heads == nh
    nk = 1 if shared_kv else nh
    tq = 512 if lat % 512 == 0 else TOK
    tk = 4096 if lat % 8192 == 0 else TOK
    zb = z_col // (nh * HEAD)
    vb = v_col // (nk * HEAD)
    cb = lat // ctx
    out_shape = jax.ShapeDtypeStruct((t, heads * HEAD), BF16)

    def call(bounded):
        common = dict(tk=tk, dq=dq, shared_kv=shared_kv, bounded=bounded)
        suffix = "_bounded" if bounded else "_online"
        y = pl.pallas_call(
            functools.partial(_flash_kernel, n_loop=0, tail=(0, ctx), out_row0=lat, **common),
            grid=(heads // nh,),
            in_specs=[
                pl.BlockSpec((ctx, nh * dq), lambda g: (cb, g)),
                pl.BlockSpec((ctx, nk * dq), lambda g: (cb, g)),
                pl.BlockSpec((ctx, nk * HEAD), lambda g: (cb, vb + g)),
                pl.BlockSpec((ctx, nh * HEAD), lambda g: (cb, zb + g)),
            ],
            out_specs=pl.BlockSpec((t, nh * HEAD), lambda g: (0, g)),
            out_shape=out_shape,
            compiler_params=_params("parallel"),
            name=name + "_ctx" + suffix,
        )(q, k, v, p)
        if bounded:
            body = functools.partial(_flash_t_kernel, n_loop=lat // tk, tk=tk, n_tail=ctx // TOK, dq=dq,
                                     shared_kv=shared_kv)
            v_spec, v_arg = pl.BlockSpec((t // TOK, nk * HEAD, TOK), lambda g, i: (0, g, 0)), vt
        else:
            body = functools.partial(_flash_lat_kernel, n_loop=lat // tk, tail=(lat, ctx), **common)
            v_spec, v_arg = pl.BlockSpec((t, nk * HEAD), lambda g, i: (0, vb + g)), v
        return pl.pallas_call(
            body,
            grid=(heads // nh, lat // tq),
            in_specs=[
                pl.BlockSpec((tq, nh * dq), lambda g, i: (i, g)),
                pl.BlockSpec((t, nk * dq), lambda g, i: (0, g)),
                v_spec,
                pl.BlockSpec((tq, nh * HEAD), lambda g, i: (i, zb + g)),
                pl.BlockSpec(memory_space=pl.ANY),
            ],
            out_specs=pl.BlockSpec((tq, nh * HEAD), lambda g, i: (i, g)),
            out_shape=out_shape,
            input_output_aliases={4: 0},
            compiler_params=_params("parallel", "parallel"),
            name=name + suffix,
        )(q, k, v_arg, p, y)

    return lax.cond(logit_bound <= MAX_EXP2_LOGIT, lambda: call(True), lambda: call(False))


def _gqa_prep_kernel(q_ref, k_ref, v_ref, cos_ref, sin_ref, qn_ref, kn_ref, oq_ref, ok_ref, ovt_ref):
    cos, sin = cos_ref[...], sin_ref[...]
    scale = HEAD ** -0.5 * LOG2E
    for h in range(A_HEADS):
        c = slice(h * HEAD, (h + 1) * HEAD)
        oq_ref[:, c] = (_rope(_rms(q_ref[:, c].astype(F32), qn_ref[...], HEAD), cos, sin, 32) * scale).astype(BF16)
    for g in range(A_KV_HEADS):
        c = slice(g * HEAD, (g + 1) * HEAD)
        ok_ref[:, c] = _rope(_rms(k_ref[:, c].astype(F32), kn_ref[...], HEAD), cos, sin, 32).astype(BF16)
    for j in range(v_ref.shape[0] // TOK):
        ovt_ref[j] = v_ref[j * TOK:(j + 1) * TOK, :].astype(F32).T.astype(BF16)


def _gqa_prep(p, cos, sin, qn, kn):
    t = p.shape[0]
    tb = _row_tile(t)
    qw, kw = A_HEADS * HEAD, A_KV_HEADS * HEAD
    row = lambda w, c: pl.BlockSpec((tb, w), lambda i: (i, c))
    vec = pl.BlockSpec((1, HEAD), lambda i: (0, 0))
    return pl.pallas_call(
        _gqa_prep_kernel,
        grid=(t // tb,),
        in_specs=[row(qw, P_AQ // qw), row(kw, P_AK // kw), row(kw, P_AV // kw), row(HEAD, 0), row(HEAD, 0), vec, vec],
        out_specs=[row(qw, 0), row(kw, 0), pl.BlockSpec((tb // TOK, kw, TOK), lambda i: (i, 0, 0))],
        out_shape=[jax.ShapeDtypeStruct((t, qw), BF16), jax.ShapeDtypeStruct((t, kw), BF16),
                   jax.ShapeDtypeStruct((t // TOK, kw, TOK), BF16)],
        compiler_params=_params("parallel"),
        name="gqa_prep",
    )(p, p, p, cos, sin, qn.reshape(1, HEAD), kn.reshape(1, HEAD))


def _mla_prep_kernel(cq_ref, ckv_ref, sm_ref, cos_ref, sin_ref, qnw_ref, kvnw_ref, wq_ref, wkv_ref,
                     qn_ref, kn_ref, oq_ref, ok_ref, ov_ref, ovt_ref):
    cos, sin = cos_ref[...], sin_ref[...]
    scale = MLA_QK ** -0.5 * LOG2E
    qn, kn = qn_ref[...], kn_ref[...]
    q = _dg(_rms(cq_ref[...].astype(F32), qnw_ref[...], MLA_Q_RANK).astype(BF16), wq_ref[...])
    kv = _dg(_rms(ckv_ref[...].astype(F32), kvnw_ref[...], MLA_KV_RANK).astype(BF16), wkv_ref[...])
    lane = lax.broadcasted_iota(jnp.int32, sm_ref.shape, 1)
    kr = jnp.where(lane < MLA_ROPE, sm_ref[...].astype(F32), 0.0)
    kr_ss = jnp.sum(kr * kr, axis=-1, keepdims=True)
    for h in range(MLA_HEADS):
        lo = slice(2 * h * HEAD, (2 * h + 1) * HEAD)
        hi = slice((2 * h + 1) * HEAD, (2 * h + 2) * HEAD)
        q_nope, q_rope = q[:, lo], q[:, hi]
        r = lax.rsqrt((jnp.sum(q_nope * q_nope, axis=-1, keepdims=True)
                       + jnp.sum(q_rope * q_rope, axis=-1, keepdims=True)) * (1.0 / MLA_QK) + NORM_EPS)
        oq_ref[:, lo] = (q_nope * r * qn[:, :HEAD] * scale).astype(BF16)
        oq_ref[:, hi] = (_rope(q_rope * r * qn[:, HEAD:], cos, sin, 16) * scale).astype(BF16)
        k_nope, v = kv[:, lo], kv[:, hi]
        r = lax.rsqrt((jnp.sum(k_nope * k_nope, axis=-1, keepdims=True) + kr_ss) * (1.0 / MLA_QK) + NORM_EPS)
        ok_ref[:, lo] = (k_nope * r * kn[:, :HEAD]).astype(BF16)
        ok_ref[:, hi] = _rope(kr * r * kn[:, HEAD:], cos, sin, 16).astype(BF16)
        ov_ref[:, h * HEAD:(h + 1) * HEAD] = v.astype(BF16)
        for j in range(v.shape[0] // TOK):
            ovt_ref[j, h * HEAD:(h + 1) * HEAD, :] = v[j * TOK:(j + 1) * TOK, :].T.astype(BF16)


def _mla_prep(p, cos, sin, q_norm_w, kv_norm_w, wq, wkv, qn, kn):
    t = p.shape[0]
    tb = _row_tile(t)
    w2 = MLA_HEADS * 2 * HEAD
    row = lambda w, c: pl.BlockSpec((tb, w), lambda i: (i, c))
    full = lambda a: pl.BlockSpec(a.shape, lambda i: (0, 0))
    args = (q_norm_w.reshape(1, -1), kv_norm_w.reshape(1, -1), wq, wkv, qn, kn)
    return pl.pallas_call(
        _mla_prep_kernel,
        grid=(t // tb,),
        in_specs=[row(MLA_Q_RANK, P_DCQ // MLA_Q_RANK), row(MLA_KV_RANK, P_DCKV // MLA_KV_RANK),
                  row(HEAD, P_SM // HEAD), row(HEAD, 0), row(HEAD, 0)] + [full(a) for a in args],
        out_specs=[row(w2, 0), row(w2, 0), row(MLA_HEADS * HEAD, 0),
                   pl.BlockSpec((tb // TOK, MLA_HEADS * HEAD, TOK), lambda i: (i, 0, 0))],
        out_shape=[jax.ShapeDtypeStruct((t, w2), BF16), jax.ShapeDtypeStruct((t, w2), BF16),
                   jax.ShapeDtypeStruct((t, MLA_HEADS * HEAD), BF16),
                   jax.ShapeDtypeStruct((t // TOK, MLA_HEADS * HEAD, TOK), BF16)],
        compiler_params=_params("parallel"),
        name="mla_prep",
    )(p, p, p, cos, sin, *args)


def _lru_kernel(ux_ref, z_ref, cw_ref, cb_ref, w_ref, b_ref, lam_ref, o_ref, af_ref, ab_ref, hf_ref, hb_ref, *, lat):
    t = ux_ref.shape[0]
    ctx = t - lat
    nb = t // TOK
    nb_lat = lat // TOK
    cw, cb = cw_ref[...], cb_ref[...]
    bias = b_ref[0]
    sp = (_softplus(-lam_ref[0:1, :]), _softplus(-lam_ref[1:2, :]))
    dirs = ((af_ref, hf_ref), (ab_ref, hb_ref))
    sub = lax.broadcasted_iota(jnp.int32, (TOK, HEAD), 0) % 8

    def tile_scan(a, b, reverse):
        for k in (1, 2, 4):
            keep = (sub + k <= 7) if reverse else (sub >= k)
            shift = TOK - k if reverse else k
            a_prev = jnp.where(keep, pltpu.roll(a, shift, 0), 1.0)
            b_prev = jnp.where(keep, pltpu.roll(b, shift, 0), 0.0)
            b = b + a * b_prev
            a = a * a_prev
        return a, b

    def gates_block(blk, _):
        t0 = pl.multiple_of(blk * TOK, TOK)
        rows = pl.ds(t0, TOK)
        first = jnp.logical_or(blk == 0, blk == nb_lat)
        last = jnp.logical_or(blk == nb_lat - 1, blk == nb - 1)
        prev = ux_ref[pl.ds(pl.multiple_of(jnp.maximum(t0 - HALO, 0), HALO), HALO), :].astype(F32)
        nxt = ux_ref[pl.ds(pl.multiple_of(jnp.minimum(t0 + TOK, t - HALO), HALO), HALO), :].astype(F32)
        xe = jnp.concatenate([jnp.where(first, 0.0, prev), ux_ref[rows, :].astype(F32),
                              jnp.where(last, 0.0, nxt)], axis=0)
        xs = cb + xe[HALO - 2:HALO - 2 + TOK] * cw[0:1]
        for j in range(1, 4):
            xs = xs + xe[HALO - 2 + j:HALO - 2 + j + TOK] * cw[j:j + 1]
        g = _dg(xs.astype(BF16), w_ref[0]) + bias
        for d, (a_ref, h_ref) in enumerate(dirs):
            r = _sigmoid(g[:, 2 * d * HEAD:(2 * d + 1) * HEAD])
            gi = _sigmoid(g[:, (2 * d + 1) * HEAD:(2 * d + 2) * HEAD])
            a = jnp.exp(-LRU_C * r * sp[d])
            a, h = tile_scan(a, jnp.sqrt(1.0 - a * a) * (gi * xs), reverse=(d == 1))
            a_ref[rows, :] = a
            h_ref[rows, :] = h
        return 0

    lax.fori_loop(0, nb, gates_block, 0)

    def carry_tiles(tile0, n_tiles, state):
        def step(j, st):
            rf = pl.ds(pl.multiple_of((tile0 + j) * 8, 8), 8)
            rb = pl.ds(pl.multiple_of((tile0 + n_tiles - 1 - j) * 8, 8), 8)
            tf = hf_ref[rf, :] + af_ref[rf, :] * st[0]
            tb = hb_ref[rb, :] + ab_ref[rb, :] * st[1]
            hf_ref[rf, :] = tf
            hb_ref[rb, :] = tb
            return tf[7:8], tb[0:1]

        return lax.fori_loop(0, n_tiles, step, state, unroll=8)

    zero = jnp.zeros((1, HEAD), F32)
    carry_tiles(0, lat // 8, carry_tiles(lat // 8, ctx // 8, (zero, zero)))

    def out_block(blk, _):
        rows = pl.ds(pl.multiple_of(blk * TOK, TOK), TOK)
        o_ref[rows, :] = ((hf_ref[rows, :] + hb_ref[rows, :]) * _silu(z_ref[rows, :].astype(F32))).astype(BF16)
        return 0

    lax.fori_loop(0, nb, out_block, 0)


def _lru(p, conv_w, conv_b, w_gates, b_gates, lam, lat):
    t = p.shape[0]
    col = lambda c0: pl.BlockSpec((t, HEAD), lambda n: (0, c0 // HEAD + n))
    vec = lambda r: pl.BlockSpec((r, HEAD), lambda n: (0, n))
    return pl.pallas_call(
        functools.partial(_lru_kernel, lat=lat),
        grid=(LRU_BLOCKS,),
        in_specs=[col(P_BX), col(P_BZ), vec(4), vec(1),
                  pl.BlockSpec((1, HEAD, 4 * HEAD), lambda n: (n, 0, 0)),
                  pl.BlockSpec((1, 1, 4 * HEAD), lambda n: (n, 0, 0)), vec(2)],
        out_specs=pl.BlockSpec((t, HEAD), lambda n: (0, n)),
        out_shape=jax.ShapeDtypeStruct((t, LRU_WIDTH), BF16),
        scratch_shapes=[pltpu.VMEM((t, HEAD), F32)] * 4,
        compiler_params=_params("parallel"),
        name="lru",
    )(p, p, conv_w, conv_b.reshape(1, -1), w_gates, b_gates, lam)


def _dn_prep_kernel(q_ref, k_ref, v_ref, qp_ref, kp_ref, vp_ref, qx_ref, kx_ref, vx_ref, cw_ref, sm_ref,
                    alog_ref, dtb_ref, u_ref, w_ref, qd_ref, kd_ref, qk_ref, gl_ref, *, nb_lat):
    i = pl.program_id(0)
    tb = q_ref.shape[0]
    cc = DN_CHUNK
    first = jnp.logical_or(i == 0, i == nb_lat)
    last = jnp.logical_or(i == nb_lat - 1, i == nb_lat)
    cw = cw_ref[...]

    def conv_silu(x_ref, prev_ref, next_ref, c0):
        xe = jnp.concatenate([jnp.where(first, 0.0, prev_ref[...].astype(F32)), x_ref[...].astype(F32),
                              jnp.where(last, 0.0, next_ref[...].astype(F32))], axis=0)
        y = xe[HALO - 2:HALO - 2 + tb] * cw[0:1, c0:c0 + MIX]
        for j in range(1, 4):
            y = y + xe[HALO - 2 + j:HALO - 2 + j + tb] * cw[j:j + 1, c0:c0 + MIX]
        return _silu(y)

    q = conv_silu(q_ref, qp_ref, qx_ref, 0)
    k = conv_silu(k_ref, kp_ref, kx_ref, MIX)
    v = conv_silu(v_ref, vp_ref, vx_ref, 2 * MIX)

    sm = sm_ref[...].astype(F32)
    beta_all = _sigmoid(sm)
    g_all = -jnp.exp(alog_ref[...]) * _softplus(sm + dtb_ref[...])

    r = lax.broadcasted_iota(jnp.int32, (tb, tb), 0)
    c = lax.broadcasted_iota(jnp.int32, (tb, tb), 1)
    same = (r // cc) == (c // cc)
    tri_f = jnp.where(same, jnp.where(c <= r, 1.0, 0.0), 0.0).astype(BF16)
    tri_b = jnp.where(same, jnp.where(c >= r, 1.0, 0.0), 0.0).astype(BF16)
    gcs = (_dot_exact_lhs(tri_f, g_all), _dot_exact_lhs(tri_b, g_all))

    ii = lax.broadcasted_iota(jnp.int32, (cc, cc), 0)
    jj = lax.broadcasted_iota(jnp.int32, (cc, cc), 1)
    eye = jnp.where(ii == jj, 1.0, 0.0)
    blk16 = (ii // 16) == (jj // 16)
    blk32 = (ii // 32) == (jj // 32)
    off32 = jnp.logical_and(blk32, jnp.logical_not(blk16))
    off64 = jnp.logical_not(blk32)
    b16 = lambda m: m.astype(BF16)
    strict = (ii > jj, ii < jj)

    gl_ref[...] = jnp.zeros_like(gl_ref)
    qs, ks = [], []
    for h in range(DN_HEADS):
        hs = slice(h * HEAD, (h + 1) * HEAD)
        qh, kh = q[:, hs], k[:, hs]
        qs.append(qh * lax.rsqrt(jnp.sum(qh * qh, axis=-1, keepdims=True) + NORM_EPS) * (HEAD ** -0.5))
        ks.append(kh * lax.rsqrt(jnp.sum(kh * kh, axis=-1, keepdims=True) + NORM_EPS))

    for ch in range(tb // cc):
        rows = slice(ch * cc, (ch + 1) * cc)
        gc = [gcs[d][rows] for d in range(2)]
        gct = [g.T for g in gc]
        kk, qk0 = [], []
        for h in range(DN_HEADS):
            k16 = b16(ks[h][rows])
            kk.append(_dg(k16, k16, NT))
            qk0.append(_dg(b16(qs[h][rows]), k16, NT))
        probs = []
        for d in range(2):
            end = cc - 1 if d == 0 else 0
            for h in range(DN_HEADS):
                hs = slice(h * HEAD, (h + 1) * HEAD)
                lb = SM_AB + d * 2 * DN_HEADS + h
                lg = lb + DN_HEADS
                beta = beta_all[rows, lb:lb + 1]
                gcol, grow = gc[d][:, lg:lg + 1], gct[d][lg:lg + 1, :]
                glast = gc[d][end:end + 1, lg:lg + 1]
                qh, kh, vh = qs[h][rows], ks[h][rows], v[rows, hs]
                decay = jnp.exp(jnp.where(strict[d], gcol - grow, NEG_BIG))
                eg = jnp.exp(gcol)
                kb = kh * beta
                lm = (beta * kk[h]) * decay
                qk_ref[d, h, rows, :] = (qk0[h] * (decay + eye)).astype(BF16)
                qd_ref[d, rows, hs] = (qh * eg).astype(BF16)
                kd_ref[d, rows, hs] = (kh * jnp.exp(glast - gcol)).astype(BF16)
                gl_ref[d, ch, h:h + 1, :] = jnp.broadcast_to(jnp.exp(glast), (1, HEAD))
                probs.append((d, hs, lm, jnp.concatenate([vh * beta, kb * eg], axis=1)))
        lms = [lm for _, _, lm, _ in probs]
        diag = [jnp.where(blk16, lm, 0.0) for lm in lms]
        xs = [eye - m for m in diag]
        pw = [b16(m) for m in diag]
        pw = [_dg(m, m) for m in pw]
        for _ in range(2):
            pw = [b16(m) for m in pw]
            xs = [x + _dg(b16(x), m) for x, m in zip(xs, pw)]
            pw = [_dg(m, m) for m in pw]
        xs = [x + _dg(b16(x), b16(m)) for x, m in zip(xs, pw)]
        for off in (off32, off64):
            x16 = [b16(x) for x in xs]
            cx = [_dg(b16(jnp.where(off, lm, 0.0)), x) for lm, x in zip(lms, x16)]
            xs = [x - _dg(xb, b16(c)) for x, xb, c in zip(xs, x16, cx)]
        sols = [_dg(b16(x), b16(rhs)) for x, (_, _, _, rhs) in zip(xs, probs)]
        for sol, (d, hs, _, _) in zip(sols, probs):
            u_ref[d, rows, hs] = sol[:, :HEAD]
            w_ref[d, rows, hs] = sol[:, HEAD:].astype(BF16)


def _dn_prep(p, conv_w, alog_row, dtb_row, lat):
    t = p.shape[0]
    tb = TOK
    nb, nb_lat = t // tb, lat // tb
    rh = tb // HALO
    blk = lambda c0: pl.BlockSpec((tb, MIX), lambda i: (i, c0 // MIX))
    prev = lambda c0: pl.BlockSpec((HALO, MIX), lambda i: (jnp.maximum(i * rh - 1, 0), c0 // MIX))
    nxt = lambda c0: pl.BlockSpec((HALO, MIX), lambda i: (jnp.minimum((i + 1) * rh, t // HALO - 1), c0 // MIX))
    cols = (P_CQ, P_CK, P_CV)
    big = pl.BlockSpec((2, tb, MIX), lambda i: (0, i, 0))
    big_shape = lambda dt: jax.ShapeDtypeStruct((2, t, MIX), dt)
    return pl.pallas_call(
        functools.partial(_dn_prep_kernel, nb_lat=nb_lat),
        grid=(nb,),
        in_specs=[blk(c0) for c0 in cols] + [prev(c0) for c0 in cols] + [nxt(c0) for c0 in cols] + [
            pl.BlockSpec((4, 3 * MIX), lambda i: (0, 0)),
            pl.BlockSpec((tb, HEAD), lambda i: (i, P_SM // HEAD)),
            pl.BlockSpec((1, HEAD), lambda i: (0, 0)),
            pl.BlockSpec((1, HEAD), lambda i: (0, 0))],
        out_specs=[big, big, big, big,
                   pl.BlockSpec((2, DN_HEADS, tb, DN_CHUNK), lambda i: (0, 0, i, 0)),
                   pl.BlockSpec((2, tb // DN_CHUNK, 8, HEAD), lambda i: (0, i, 0, 0))],
        out_shape=[big_shape(F32), big_shape(BF16), big_shape(BF16), big_shape(BF16),
                   jax.ShapeDtypeStruct((2, DN_HEADS, t, DN_CHUNK), BF16),
                   jax.ShapeDtypeStruct((2, t // DN_CHUNK, 8, HEAD), F32)],
        compiler_params=_params("parallel"),
        name="dn_prep",
    )(*([p] * 9), conv_w, p, alog_row, dtb_row)


def _dn_scan_kernel(uf, wf, qdf, kdf, qkf, glf, ub, wb, qdb, kdb, qkb, glb, of_ref, ob_ref, s_ref):
    @pl.when(pl.program_id(0) == 0)
    def _():
        s_ref[...] = jnp.zeros_like(s_ref)

    cc = DN_CHUNK
    n_chunks = uf.shape[1] // cc
    dirs = ((uf, wf, qdf, kdf, qkf, glf, of_ref), (ub, wb, qdb, kdb, qkb, glb, ob_ref))
    probs = [(d, h, slice(h * HEAD, (h + 1) * HEAD)) for d in range(2) for h in range(DN_HEADS)]
    for step in range(n_chunks):
        ch = (step, n_chunks - 1 - step)
        rows = tuple(slice(c * cc, (c + 1) * cc) for c in ch)
        ws = [_dg(jnp.concatenate([dirs[d][1][0, rows[d], hs], dirs[d][2][0, rows[d], hs]], axis=0),
                  s_ref[d, h].astype(BF16)) for d, h, hs in probs]
        v_new = [(dirs[d][0][0, rows[d], hs] - t[:cc]).astype(BF16) for t, (d, h, hs) in zip(ws, probs)]
        outs = [t[cc:] + _dg(dirs[d][4][0, h, rows[d], :], vn) for t, vn, (d, h, hs) in zip(ws, v_new, probs)]
        upd = [_dg(dirs[d][3][0, rows[d], hs], vn, TN) for vn, (d, h, hs) in zip(v_new, probs)]
        for o, ds, (d, h, hs) in zip(outs, upd, probs):
            dirs[d][6][rows[d], hs] = o
            s_ref[d, h] = s_ref[d, h] * dirs[d][5][0, ch[d], h:h + 1, :] + ds


def _dn_scan(u, w, qd, kd, qk, gl, lat):
    t = u.shape[1]
    tb = TOK
    cpb = tb // DN_CHUNK
    n, n_lat = t // tb, lat // tb
    n_ctx = n - n_lat
    bf = lambda i: jnp.where(i < n_ctx, n_lat + i, i - n_ctx)
    bb = lambda i: n - 1 - i
    specs = []
    for d, blk in ((0, bf), (1, bb)):
        big = pl.BlockSpec((1, tb, MIX), lambda i, d=d, blk=blk: (d, blk(i), 0))
        specs += [big, big, big, big,
                  pl.BlockSpec((1, DN_HEADS, tb, DN_CHUNK), lambda i, d=d, blk=blk: (d, 0, blk(i), 0)),
                  pl.BlockSpec((1, cpb, 8, HEAD), lambda i, d=d, blk=blk: (d, blk(i), 0, 0))]
    return pl.pallas_call(
        _dn_scan_kernel,
        grid=(n,),
        in_specs=specs,
        out_specs=[pl.BlockSpec((tb, MIX), lambda i: (bf(i), 0)), pl.BlockSpec((tb, MIX), lambda i: (bb(i), 0))],
        out_shape=[jax.ShapeDtypeStruct((t, MIX), F32)] * 2,
        scratch_shapes=[pltpu.VMEM((2, DN_HEADS, HEAD, HEAD), F32)],
        compiler_params=_params("arbitrary"),
        name="dn_scan",
    )(u, w, qd, kd, qk, gl, u, w, qd, kd, qk, gl)


def _dn_out_kernel(of_ref, ob_ref, z_ref, nw_ref, o_ref):
    for h in range(DN_HEADS):
        hs = slice(h * HEAD, (h + 1) * HEAD)
        o = of_ref[:, hs] + ob_ref[:, hs]
        o_ref[:, hs] = (_rms(o, nw_ref[...], HEAD) * _silu(z_ref[:, hs].astype(F32))).astype(BF16)


def _dn_out(o_f, o_b, p, norm_w):
    t = p.shape[0]
    tb = _row_tile(t)
    row = lambda c: pl.BlockSpec((tb, MIX), lambda i: (i, c))
    return pl.pallas_call(
        _dn_out_kernel,
        grid=(t // tb,),
        in_specs=[row(0), row(0), row(P_CZ // MIX), pl.BlockSpec((1, HEAD), lambda i: (0, 0))],
        out_specs=row(0),
        out_shape=jax.ShapeDtypeStruct((t, MIX), BF16),
        compiler_params=_params("parallel"),
        name="dn_out",
    )(o_f, o_b, p, norm_w.reshape(1, HEAD))


def _cast_tiles_kernel(w_ref, o_ref):
    o_ref[...] = w_ref[...].astype(o_ref.dtype)


def _cast_tiles(w, n_tiles):
    l, d, _ = w.shape
    return pl.pallas_call(
        _cast_tiles_kernel,
        grid=(l, n_tiles),
        in_specs=[pl.BlockSpec((None, d, W_TILE), lambda a, j: (a, 0, j))],
        out_specs=pl.BlockSpec((None, None, d, W_TILE), lambda a, j: (a, j, 0, 0)),
        out_shape=jax.ShapeDtypeStruct((l, n_tiles, d, W_TILE), BF16),
        compiler_params=_params("parallel", "parallel"),
        name="cast_tiles",
    )(w)


def _arrange_w_in(w_in):
    pad = jnp.zeros(w_in.shape[:2] + (P_W - w_in.shape[2],), w_in.dtype)
    tail = [w_in[..., 5328:5840],
            w_in[..., 5008:5264],
            w_in[..., 4624:5008],
            w_in[..., 5264:5328],
            w_in[..., 4608:4624],
            pad]
    def tiles(w):
        l, d, n = w.shape
        return w.reshape(l, d, n // W_TILE, W_TILE).transpose(0, 2, 1, 3).astype(BF16)

    return _cast_tiles(w_in, P_MAIN // W_TILE), tiles(jnp.concatenate(tail, axis=-1))


def _arrange_mla_wq(w_uq):
    l, r, _ = w_uq.shape
    w = w_uq.reshape(l, r, MLA_HEADS, MLA_QK)
    w = jnp.pad(w, ((0, 0), (0, 0), (0, 0), (0, 2 * HEAD - MLA_QK)))
    return w.reshape(l, r, MLA_HEADS * 2 * HEAD).astype(BF16)


def _pad_qk_norm(w):
    return jnp.pad(w, ((0, 0), (0, 2 * HEAD - MLA_QK)))[:, None, :]


def _small_lane_row(vals):
    l = vals.shape[0]
    row = jnp.zeros((l, HEAD), F32)
    for d in range(2):
        lo = SM_AB + d * 2 * DN_HEADS + DN_HEADS
        row = row.at[:, lo:lo + DN_HEADS].set(vals[:, d, :])
    return row[:, None, :]


def _arrange_lru_gates(w_a, b_a, w_x, b_x):
    w = jnp.concatenate([w_a[:, 0], w_x[:, 0], w_a[:, 1], w_x[:, 1]], axis=-1)
    l = b_a.shape[0]
    blk = lambda b, d: b[:, d].reshape(l, LRU_BLOCKS, 1, HEAD)
    b = jnp.concatenate([blk(b_a, 0), blk(b_x, 0), blk(b_a, 1), blk(b_x, 1)], axis=-1)
    return w.astype(BF16), b


def _rope_tables(lat, ctx):
    n_rows = lat // GRID_W
    pos = jnp.arange(max(n_rows, GRID_W), dtype=F32)
    lane = jnp.arange(HEAD)

    def table(half, width):
        inv_freq = ROPE_THETA ** (-jnp.arange(half, dtype=F32) / half)
        ang = pos[:, None] * inv_freq[lane % half][None, :]
        live = (lane < width)[None, :]
        by_row = ((lane // (2 * half)) == 0)[None, :]
        sign = jnp.where((lane % (2 * half)) < half, -1.0, 1.0)[None, :]

        def expand(f):
            per_row = jnp.repeat(f[:n_rows], GRID_W, axis=0)
            per_col = jnp.tile(f[:GRID_W], (n_rows, 1))
            return jnp.where(by_row, per_row, per_col)

        cos = jnp.where(live, expand(jnp.cos(ang)), 0.0)
        sin = jnp.where(live, expand(jnp.sin(ang)), 0.0) * sign
        cos_c = jnp.broadcast_to(jnp.where(live, 1.0, 0.0), (ctx, HEAD))
        return (jnp.concatenate([cos, cos_c], axis=0).astype(F32),
                jnp.concatenate([sin, jnp.zeros((ctx, HEAD), F32)], axis=0).astype(F32))

    return table(32, HEAD), table(16, MLA_ROPE)


def kernel(x, c, ctx, c_ctx, norm_w, w_ada, b_ada, w_in, w_out, attn_q_norm, attn_k_norm, lru_conv_w, lru_conv_b, lru_w_a, lru_b_a, lru_w_x, lru_b_x, lru_lambda, dn_conv_w, dn_a_log, dn_dt_bias, dn_norm_w, mla_q_norm, mla_kv_norm, mla_w_uq, mla_w_ukv, mla_q_qk_norm, mla_k_qk_norm):
    assert x.shape[0] == 1 and ctx.shape[1] == TOK and x.shape[1] % (8 * TOK) == 0
    lat, n_ctx = x.shape[1], ctx.shape[1]
    depth = w_in.shape[0]

    xs = jnp.concatenate([x[0], ctx[0]], axis=0)
    mod = _ada(jnp.stack([c[0], c_ctx], axis=1), w_ada, b_ada)
    (cos_a, sin_a), (cos_m, sin_m) = _rope_tables(lat, n_ctx)
    w_main, w_tail = _arrange_w_in(w_in)
    w_out_r = w_out.reshape(depth, 4, MIX, D_MODEL).astype(BF16)
    wq_r = _arrange_mla_wq(mla_w_uq)
    wkv_r = mla_w_ukv.astype(BF16)
    qn_r, kn_r = _pad_qk_norm(mla_q_qk_norm), _pad_qk_norm(mla_k_qk_norm)
    alog_r, dtb_r = _small_lane_row(dn_a_log), _small_lane_row(dn_dt_bias)
    lru_w, lru_b = _arrange_lru_gates(lru_w_a, lru_b_a, lru_w_x, lru_b_x)

    for l in range(depth):
        p = _inproj(xs, norm_w[l], mod[l, 0], mod[l, 1], w_main, w_tail, l, lat)
        qa, ka, vta = _gqa_prep(p, cos_a, sin_a, attn_q_norm[l], attn_k_norm[l])
        bound_a = jnp.max(jnp.abs(attn_q_norm[l])) * jnp.max(jnp.abs(attn_k_norm[l])) * (HEAD ** 0.5 * LOG2E)
        y_a = _flash(qa, ka, p, P_AV, vta, p, P_AZ, A_HEADS, A_KV_HEADS, HEAD, lat, bound_a, "gqa_attn")
        y_b = _lru(p, lru_conv_w[l], lru_conv_b[l], lru_w[l], lru_b[l], lru_lambda[l], lat)
        u, w, qd, kd, qk, gl = _dn_prep(p, dn_conv_w[l], alog_r[l], dtb_r[l], lat)
        o_f, o_b = _dn_scan(u, w, qd, kd, qk, gl, lat)
        y_c = _dn_out(o_f, o_b, p, dn_norm_w[l])
        qm, km, vm, vtm = _mla_prep(p, cos_m, sin_m, mla_q_norm[l], mla_kv_norm[l], wq_r[l], wkv_r[l],
                               qn_r[l], kn_r[l])
        bound_d = (jnp.max(jnp.abs(mla_q_qk_norm[l])) * jnp.max(jnp.abs(mla_k_qk_norm[l]))
                   * (MLA_QK ** 0.5 * LOG2E))
        y_d = _flash(qm, km, vm, 0, vtm, p, P_DZ, MLA_HEADS, MLA_HEADS, 2 * HEAD, lat, bound_d, "mla_attn")
        last = l == depth - 1
        xs = _outproj((y_a, y_b, y_c, y_d), w_out_r, l, xs, mod[l, 2], lat, lat if last else lat + n_ctx)
    return xs[None]
```

```python
import functools
import math

import jax
import jax.numpy as jnp
from jax import lax
from jax.experimental import pallas as pl
from jax.experimental.pallas import tpu as pltpu

F32 = jnp.float32
BF16 = jnp.bfloat16

D_MODEL = 2048
DEPTH = 4
GRID_W = 64
ROPE_THETA = 10000.0
NORM_EPS = 1e-6
HEAD = 128
A_HEADS, A_KV_HEADS = 4, 2
LRU_WIDTH, LRU_BLOCKS, LRU_C = 512, 4, 8.0
DN_HEADS, DN_CHUNK = 4, 64
DN_GROUP = 4
MLA_HEADS, MLA_Q_RANK, MLA_KV_RANK, MLA_NOPE, MLA_ROPE = 4, 384, 256, 128, 64
MLA_QK = MLA_NOPE + MLA_ROPE
MIX = 512

P_AQ, P_AK, P_AV, P_AZ = 0, 512, 768, 1024
P_BX, P_BZ = 1536, 2048
P_CQ, P_CK, P_CV, P_CZ = 2560, 3072, 3584, 4096
P_DZ, P_DCKV, P_DCQ = 4608, 5120, 5376
P_SM = 5760
P_W = 6144
P_MAIN = 4608
W_TILE = 768
SM_AB = MLA_ROPE

TOK = 256
HALO = 16
VMEM_LIMIT = 56 * 1024 * 1024
LOG2E = math.log2(math.e)
NEG_BIG = -1e30
MAX_EXP2_LOGIT = 60.0

NN = ((1,), (0,))
NT = ((1,), (1,))
TN = ((0,), (0,))


def _dg(a, b, dims=NN):
    return lax.dot_general(a, b, (dims, ((), ())), preferred_element_type=F32)


def _dot_exact_lhs(m, b):
    b0 = b.astype(BF16)
    r1 = b - b0.astype(F32)
    b1 = r1.astype(BF16)
    b2 = (r1 - b1.astype(F32)).astype(BF16)
    return _dg(m, b0) + (_dg(m, b1) + _dg(m, b2))


def _sigmoid(x):
    return 1.0 / (1.0 + jnp.exp(-x))


def _silu(x):
    return x * _sigmoid(x)


def _softplus(x):
    return jnp.maximum(x, 0.0) + jnp.log(1.0 + jnp.exp(-jnp.abs(x)))


def _rms(x, w, n):
    return x * lax.rsqrt(jnp.sum(x * x, axis=-1, keepdims=True) * (1.0 / n) + NORM_EPS) * w


def _rope(x, cos, sin_signed, half):
    lane = lax.broadcasted_iota(jnp.int32, x.shape, 1)
    first = (lane % (2 * half)) < half
    rot = jnp.where(first, pltpu.roll(x, HEAD - half, 1), pltpu.roll(x, half, 1))
    return x * cos + rot * sin_signed


def _params(*sem):
    return pltpu.CompilerParams(dimension_semantics=sem, vmem_limit_bytes=VMEM_LIMIT)


def _row_tile(rows):
    for t in (768, 512, 256):
        if rows % t == 0:
            return t
    raise ValueError(f"row count {rows} is not a multiple of {TOK}")


def _ada_kernel(c_ref, w_ref, b_ref, o_ref):
    s = _silu(c_ref[...])
    w = w_ref[0]
    r0 = jnp.sum(s[:, 0:1] * w, axis=0, keepdims=True)
    r1 = jnp.sum(s[:, 1:2] * w, axis=0, keepdims=True)
    o_ref[0, 0] = jnp.concatenate([r0, r1], axis=0) + b_ref[0]


def _ada(c_cols, w_ada, b_ada):
    depth, d, _ = w_ada.shape
    tn = 512
    per = d // tn
    return pl.pallas_call(
        _ada_kernel,
        grid=(depth, 3 * per),
        in_specs=[
            pl.BlockSpec((d, 2), lambda l, j: (0, 0)),
            pl.BlockSpec((1, d, tn), lambda l, j: (l, 0, j)),
            pl.BlockSpec((1, 1, tn), lambda l, j: (l, 0, j)),
        ],
        out_specs=pl.BlockSpec((1, 1, 2, tn), lambda l, j: (l, j // per, 0, j % per)),
        out_shape=jax.ShapeDtypeStruct((depth, 3, 2, d), F32),
        compiler_params=_params("parallel", "parallel"),
        name="ada",
    )(c_cols, w_ada, b_ada.reshape(depth, 1, 3 * d))


def _inproj_kernel(x_ref, nw_ref, shift_ref, scale_ref, wm_ref, wt_ref, o_ref, h_ref, *, lat, n_main):
    i = pl.program_id(0)
    tm = x_ref.shape[0]

    @pl.when(pl.program_id(1) == 0)
    def _():
        rc = 32
        gain = nw_ref[...] * (1.0 + scale_ref[...])
        shift = shift_ref[...]

        def chunk(ci, _):
            rows = pl.ds(pl.multiple_of(ci * rc, rc), rc)
            is_ctx = i * tm + ci * rc >= lat
            g = jnp.where(is_ctx, gain[1:2, :], gain[0:1, :])
            b = jnp.where(is_ctx, shift[1:2, :], shift[0:1, :])
            x = x_ref[rows, :]
            r = lax.rsqrt(jnp.sum(x * x, axis=-1, keepdims=True) * (1.0 / D_MODEL) + NORM_EPS)
            h_ref[rows, :] = ((x * r) * g + b).astype(BF16)
            return 0

        lax.fori_loop(0, tm // rc, chunk, 0)

    @pl.when(pl.program_id(1) < n_main)
    def _():
        o_ref[...] = _dg(h_ref[...], wm_ref[...]).astype(o_ref.dtype)

    @pl.when(pl.program_id(1) >= n_main)
    def _():
        o_ref[...] = _dg(h_ref[...], wt_ref[...]).astype(o_ref.dtype)


def _inproj(xs, norm_w, shift, scale, w_main, w_tail, layer, lat):
    t, d = xs.shape
    tm, tn = (1408 if t % 1408 == 0 else _row_tile(t)), W_TILE
    n_main = P_MAIN // tn
    return pl.pallas_call(
        functools.partial(_inproj_kernel, lat=lat, n_main=n_main),
        grid=(t // tm, P_W // tn),
        in_specs=[
            pl.BlockSpec((tm, d), lambda i, j: (i, 0)),
            pl.BlockSpec((1, d), lambda i, j: (0, 0)),
            pl.BlockSpec((2, d), lambda i, j: (0, 0)),
            pl.BlockSpec((2, d), lambda i, j: (0, 0)),
            pl.BlockSpec((None, None, d, tn), lambda i, j: (layer, jnp.minimum(j, n_main - 1), 0, 0)),
            pl.BlockSpec((None, None, d, tn), lambda i, j: (layer, jnp.maximum(j - n_main, 0), 0, 0)),
        ],
        out_specs=pl.BlockSpec((tm, tn), lambda i, j: (i, j)),
        out_shape=jax.ShapeDtypeStruct((t, P_W), BF16),
        scratch_shapes=[pltpu.VMEM((tm, d), BF16)],
        compiler_params=_params("parallel", "arbitrary"),
        name="inproj",
    )(xs, norm_w.reshape(1, d), shift, scale, w_main, w_tail)


def _outproj_kernel(ya_ref, yb_ref, yc_ref, yd_ref, w_ref, x_ref, g_ref, o_ref, *, lat):
    i = pl.program_id(0)
    tm = x_ref.shape[0]
    acc = _dg(ya_ref[...], w_ref[0])
    acc += _dg(yb_ref[...], w_ref[1])
    acc += _dg(yc_ref[...], w_ref[2])
    acc += _dg(yd_ref[...], w_ref[3])
    row = i * tm + lax.broadcasted_iota(jnp.int32, (tm, 1), 0)
    gate = jnp.where(row >= lat, g_ref[1:2, :], g_ref[0:1, :])
    o_ref[...] = x_ref[...] + gate * acc


def _outproj(ys, w_out, layer, xs, gate, lat, out_rows):
    d = xs.shape[1]
    tm, tn = _row_tile(out_rows), 1024
    yspec = pl.BlockSpec((tm, MIX), lambda i, j: (i, 0))
    return pl.pallas_call(
        functools.partial(_outproj_kernel, lat=lat),
        grid=(out_rows // tm, d // tn),
        in_specs=[yspec, yspec, yspec, yspec,
                  pl.BlockSpec((None, 4, MIX, tn), lambda i, j: (layer, 0, 0, j)),
                  pl.BlockSpec((tm, tn), lambda i, j: (i, j)),
                  pl.BlockSpec((2, tn), lambda i, j: (0, j))],
        out_specs=pl.BlockSpec((tm, tn), lambda i, j: (i, j)),
        out_shape=jax.ShapeDtypeStruct((out_rows, d), F32),
        compiler_params=_params("parallel", "parallel"),
        name="outproj",
    )(*ys, w_out, xs, gate)


def _flash_kernel(q_ref, k_ref, v_ref, z_ref, o_ref, *, n_loop, tk, tail, dq, shared_kv, bounded, out_row0=0):
    tq = q_ref.shape[0]
    nh = q_ref.shape[1] // dq
    if out_row0:
        o_ref[0:out_row0, :] = jnp.zeros((out_row0, o_ref.shape[1]), o_ref.dtype)

    def scores(c, rows):
        kc = 0 if shared_kv else c
        s = _dg(q_ref[:, c * dq:(c + 1) * dq], k_ref[rows, kc * dq:(kc + 1) * dq], NT)
        return s, v_ref[rows, kc * HEAD:(kc + 1) * HEAD]

    def attend_bounded(c, carry, rows):
        l, acc = carry
        s, v = scores(c, rows)
        p = jnp.exp2(s)
        for j in range(s.shape[1] // HEAD):
            l = l + p[:, j * HEAD:(j + 1) * HEAD]
        return l, acc + _dg(p.astype(BF16), v)

    def attend_online(c, carry, rows):
        m, l, acc = carry
        s, v = scores(c, rows)
        m_new = jnp.maximum(m, jnp.max(s, axis=-1, keepdims=True))
        alpha = jnp.exp2(m - m_new)
        p = jnp.exp2(s - m_new)
        l = alpha * l + jnp.sum(p, axis=-1, keepdims=True)
        return m_new, l, alpha * acc + _dg(p.astype(BF16), v)

    zeros = jnp.zeros((tq, HEAD), F32)
    if bounded:
        attend, init = attend_bounded, (zeros, zeros)
    else:
        attend, init = attend_online, (jnp.full((tq, 1), -1e30, F32), jnp.zeros((tq, 1), F32), zeros)

    def step(kb, carries):
        rows = pl.ds(pl.multiple_of(kb * tk, tk), tk)
        return tuple(attend(c, carries[c], rows) for c in range(nh))

    carries = (init,) * nh
    if n_loop:
        carries = lax.fori_loop(0, n_loop, step, carries, unroll=2)
    for c in range(nh):
        carry = attend(c, carries[c], pl.ds(tail[0], tail[1]))
        l, acc = carry[-2], carry[-1]
        cs = slice(c * HEAD, (c + 1) * HEAD)
        gate = _silu(z_ref[:, cs].astype(F32))
        o_ref[out_row0:out_row0 + tq, cs] = (acc / jnp.sum(l, axis=-1, keepdims=True) * gate).astype(o_ref.dtype)


def _flash_lat_kernel(q_ref, k_ref, v_ref, z_ref, y_ref, o_ref, **kw):
    _flash_kernel(q_ref, k_ref, v_ref, z_ref, o_ref, **kw)


def _flash_t_kernel(q_ref, k_ref, vt_ref, z_ref, y_ref, o_ref, *, n_loop, tk, n_tail, dq, shared_kv):
    tq = q_ref.shape[0]
    nh = q_ref.shape[1] // dq
    sub = tk // TOK

    def attend(c, carry, row0, tile0, n_tiles):
        l, acc = carry
        kc = 0 if shared_kv else c
        keys = n_tiles * TOK
        s = _dg(k_ref[pl.ds(row0, keys), kc * dq:(kc + 1) * dq], q_ref[:, c * dq:(c + 1) * dq], NT)
        p = jnp.exp2(s)
        for r in range(keys // 8):
            l = l + p[r * 8:(r + 1) * 8, :]
        p = p.astype(BF16)
        for j in range(n_tiles):
            acc = acc + _dg(vt_ref[tile0 + j, kc * HEAD:(kc + 1) * HEAD, :], p[j * TOK:(j + 1) * TOK, :])
        return l, acc

    init = (jnp.zeros((8, tq), F32), jnp.zeros((HEAD, tq), F32))

    def step(kb, carries):
        return tuple(attend(c, carries[c], pl.multiple_of(kb * tk, tk), kb * sub, sub) for c in range(nh))

    carries = lax.fori_loop(0, n_loop, step, (init,) * nh, unroll=2)
    for c in range(nh):
        l, acc = attend(c, carries[c], n_loop * tk, n_loop * sub, n_tail)
        cs = slice(c * HEAD, (c + 1) * HEAD)
        out = (acc / jnp.sum(l, axis=0, keepdims=True)).T
        o_ref[:, cs] = (out * _silu(z_ref[:, cs].astype(F32))).astype(o_ref.dtype)


def _flash(q, k, v, v_col, vt, p, z_col, heads, kv_heads, dq, lat, logit_bound, name):
    t = q.shape[0]
    ctx = t - lat
    nh = 2
    shared_kv = heads // kv_heads == nh
    nk = 1 if shared_kv else nh
    tq = 512 if lat % 512 == 0 else TOK
    tk = 4096 if lat % 8192 == 0 else TOK
    zb = z_col // (nh * HEAD)
    vb = v_col // (nk * HEAD)
    cb = lat // ctx
    out_shape = jax.ShapeDtypeStruct((t, heads * HEAD), BF16)

    def call(bounded):
        common = dict(tk=tk, dq=dq, shared_kv=shared_kv, bounded=bounded)
        suffix = "_bounded" if bounded else "_online"
        y = pl.pallas_call(
            functools.partial(_flash_kernel, n_loop=0, tail=(0, ctx), out_row0=lat, **common),
            grid=(heads // nh,),
            in_specs=[
                pl.BlockSpec((ctx, nh * dq), lambda g: (cb, g)),
                pl.BlockSpec((ctx, nk * dq), lambda g: (cb, g)),
                pl.BlockSpec((ctx, nk * HEAD), lambda g: (cb, vb + g)),
                pl.BlockSpec((ctx, nh * HEAD), lambda g: (cb, zb + g)),
            ],
            out_specs=pl.BlockSpec((t, nh * HEAD), lambda g: (0, g)),
            out_shape=out_shape,
            compiler_params=_params("parallel"),
            name=name + "_ctx" + suffix,
        )(q, k, v, p)
        if bounded:
            body = functools.partial(_flash_t_kernel, n_loop=lat // tk, tk=tk, n_tail=ctx // TOK, dq=dq,
                                     shared_kv=shared_kv)
            v_spec, v_arg = pl.BlockSpec((t // TOK, nk * HEAD, TOK), lambda g, i: (0, g, 0)), vt
        else:
            body = functools.partial(_flash_lat_kernel, n_loop=lat // tk, tail=(lat, ctx), **common)
            v_spec, v_arg = pl.BlockSpec((t, nk * HEAD), lambda g, i: (0, vb + g)), v
        return pl.pallas_call(
            body,
            grid=(heads // nh, lat // tq),
            in_specs=[
                pl.BlockSpec((tq, nh * dq), lambda g, i: (i, g)),
                pl.BlockSpec((t, nk * dq), lambda g, i: (0, g)),
                v_spec,
                pl.BlockSpec((tq, nh * HEAD), lambda g, i: (i, zb + g)),
                pl.BlockSpec(memory_space=pl.ANY),
            ],
            out_specs=pl.BlockSpec((tq, nh * HEAD), lambda g, i: (i, g)),
            out_shape=out_shape,
            input_output_aliases={4: 0},
            compiler_params=_params("parallel", "parallel"),
            name=name + suffix,
        )(q, k, v_arg, p, y)

    return lax.cond(logit_bound <= MAX_EXP2_LOGIT, lambda: call(True), lambda: call(False))


def _gqa_prep_kernel(q_ref, k_ref, v_ref, cos_ref, sin_ref, qn_ref, kn_ref, oq_ref, ok_ref, ovt_ref):
    cos, sin = cos_ref[...], sin_ref[...]
    scale = HEAD ** -0.5 * LOG2E
    for h in range(A_HEADS):
        c = slice(h * HEAD, (h + 1) * HEAD)
        oq_ref[:, c] = (_rope(_rms(q_ref[:, c].astype(F32), qn_ref[...], HEAD), cos, sin, 32) * scale).astype(BF16)
    for g in range(A_KV_HEADS):
        c = slice(g * HEAD, (g + 1) * HEAD)
        ok_ref[:, c] = _rope(_rms(k_ref[:, c].astype(F32), kn_ref[...], HEAD), cos, sin, 32).astype(BF16)
    for j in range(v_ref.shape[0] // TOK):
        ovt_ref[j] = v_ref[j * TOK:(j + 1) * TOK, :].astype(F32).T.astype(BF16)


def _gqa_prep(p, cos, sin, qn, kn):
    t = p.shape[0]
    tb = _row_tile(t)
    qw, kw = A_HEADS * HEAD, A_KV_HEADS * HEAD
    row = lambda w, c: pl.BlockSpec((tb, w), lambda i: (i, c))
    vec = pl.BlockSpec((1, HEAD), lambda i: (0, 0))
    return pl.pallas_call(
        _gqa_prep_kernel,
        grid=(t // tb,),
        in_specs=[row(qw, P_AQ // qw), row(kw, P_AK // kw), row(kw, P_AV // kw), row(HEAD, 0), row(HEAD, 0), vec, vec],
        out_specs=[row(qw, 0), row(kw, 0), pl.BlockSpec((tb // TOK, kw, TOK), lambda i: (i, 0, 0))],
        out_shape=[jax.ShapeDtypeStruct((t, qw), BF16), jax.ShapeDtypeStruct((t, kw), BF16),
                   jax.ShapeDtypeStruct((t // TOK, kw, TOK), BF16)],
        compiler_params=_params("parallel"),
        name="gqa_prep",
    )(p, p, p, cos, sin, qn.reshape(1, HEAD), kn.reshape(1, HEAD))


def _mla_prep_kernel(cq_ref, ckv_ref, sm_ref, cos_ref, sin_ref, qnw_ref, kvnw_ref, wq_ref, wkv_ref,
                     qn_ref, kn_ref, oq_ref, ok_ref, ov_ref, ovt_ref):
    cos, sin = cos_ref[...], sin_ref[...]
    scale = MLA_QK ** -0.5 * LOG2E
    qn, kn = qn_ref[...], kn_ref[...]
    q = _dg(_rms(cq_ref[...].astype(F32), qnw_ref[...], MLA_Q_RANK).astype(BF16), wq_ref[...])
    kv = _dg(_rms(ckv_ref[...].astype(F32), kvnw_ref[...], MLA_KV_RANK).astype(BF16), wkv_ref[...])
    lane = lax.broadcasted_iota(jnp.int32, sm_ref.shape, 1)
    kr = jnp.where(lane < MLA_ROPE, sm_ref[...].astype(F32), 0.0)
    kr_ss = jnp.sum(kr * kr, axis=-1, keepdims=True)
    for h in range(MLA_HEADS):
        lo = slice(2 * h * HEAD, (2 * h + 1) * HEAD)
        hi = slice((2 * h + 1) * HEAD, (2 * h + 2) * HEAD)
        q_nope, q_rope = q[:, lo], q[:, hi]
        r = lax.rsqrt((jnp.sum(q_nope * q_nope, axis=-1, keepdims=True)
                       + jnp.sum(q_rope * q_rope, axis=-1, keepdims=True)) * (1.0 / MLA_QK) + NORM_EPS)
        oq_ref[:, lo] = (q_nope * r * qn[:, :HEAD] * scale).astype(BF16)
        oq_ref[:, hi] = (_rope(q_rope * r * qn[:, HEAD:], cos, sin, 16) * scale).astype(BF16)
        k_nope, v = kv[:, lo], kv[:, hi]
        r = lax.rsqrt((jnp.sum(k_nope * k_nope, axis=-1, keepdims=True) + kr_ss) * (1.0 / MLA_QK) + NORM_EPS)
        ok_ref[:, lo] = (k_nope * r * kn[:, :HEAD]).astype(BF16)
        ok_ref[:, hi] = _rope(kr * r * kn[:, HEAD:], cos, sin, 16).astype(BF16)
        ov_ref[:, h * HEAD:(h + 1) * HEAD] = v.astype(BF16)
        for j in range(v.shape[0] // TOK):
            ovt_ref[j, h * HEAD:(h + 1) * HEAD, :] = v[j * TOK:(j + 1) * TOK, :].T.astype(BF16)


def _mla_prep(p, cos, sin, q_norm_w, kv_norm_w, wq, wkv, qn, kn):
    t = p.shape[0]
    tb = _row_tile(t)
    w2 = MLA_HEADS * 2 * HEAD
    row = lambda w, c: pl.BlockSpec((tb, w), lambda i: (i, c))
    full = lambda a: pl.BlockSpec(a.shape, lambda i: (0, 0))
    args = (q_norm_w.reshape(1, -1), kv_norm_w.reshape(1, -1), wq, wkv, qn, kn)
    return pl.pallas_call(
        _mla_prep_kernel,
        grid=(t // tb,),
        in_specs=[row(MLA_Q_RANK, P_DCQ // MLA_Q_RANK), row(MLA_KV_RANK, P_DCKV // MLA_KV_RANK),
                  row(HEAD, P_SM // HEAD), row(HEAD, 0), row(HEAD, 0)] + [full(a) for a in args],
        out_specs=[row(w2, 0), row(w2, 0), row(MLA_HEADS * HEAD, 0),
                   pl.BlockSpec((tb // TOK, MLA_HEADS * HEAD, TOK), lambda i: (i, 0, 0))],
        out_shape=[jax.ShapeDtypeStruct((t, w2), BF16), jax.ShapeDtypeStruct((t, w2), BF16),
                   jax.ShapeDtypeStruct((t, MLA_HEADS * HEAD), BF16),
                   jax.ShapeDtypeStruct((t // TOK, MLA_HEADS * HEAD, TOK), BF16)],
        compiler_params=_params("parallel"),
        name="mla_prep",
    )(p, p, p, cos, sin, *args)


def _lru_kernel(ux_ref, z_ref, cw_ref, cb_ref, w_ref, b_ref, lam_ref, o_ref, af_ref, ab_ref, hf_ref, hb_ref, *, lat):
    t = ux_ref.shape[0]
    ctx = t - lat
    nb = t // TOK
    nb_lat = lat // TOK
    cw, cb = cw_ref[...], cb_ref[...]
    bias = b_ref[0]
    sp = (_softplus(-lam_ref[0:1, :]), _softplus(-lam_ref[1:2, :]))
    dirs = ((af_ref, hf_ref), (ab_ref, hb_ref))
    sub = lax.broadcasted_iota(jnp.int32, (TOK, HEAD), 0) % 8

    def tile_scan(a, b, reverse):
        for k in (1, 2, 4):
            keep = (sub + k <= 7) if reverse else (sub >= k)
            shift = TOK - k if reverse else k
            a_prev = jnp.where(keep, pltpu.roll(a, shift, 0), 1.0)
            b_prev = jnp.where(keep, pltpu.roll(b, shift, 0), 0.0)
            b = b + a * b_prev
            a = a * a_prev
        return a, b

    def gates_block(blk, _):
        t0 = pl.multiple_of(blk * TOK, TOK)
        rows = pl.ds(t0, TOK)
        first = jnp.logical_or(blk == 0, blk == nb_lat)
        last = jnp.logical_or(blk == nb_lat - 1, blk == nb - 1)
        prev = ux_ref[pl.ds(pl.multiple_of(jnp.maximum(t0 - HALO, 0), HALO), HALO), :].astype(F32)
        nxt = ux_ref[pl.ds(pl.multiple_of(jnp.minimum(t0 + TOK, t - HALO), HALO), HALO), :].astype(F32)
        xe = jnp.concatenate([jnp.where(first, 0.0, prev), ux_ref[rows, :].astype(F32),
                              jnp.where(last, 0.0, nxt)], axis=0)
        xs = cb + xe[HALO - 2:HALO - 2 + TOK] * cw[0:1]
        for j in range(1, 4):
            xs = xs + xe[HALO - 2 + j:HALO - 2 + j + TOK] * cw[j:j + 1]
        g = _dg(xs.astype(BF16), w_ref[0]) + bias
        for d, (a_ref, h_ref) in enumerate(dirs):
            r = _sigmoid(g[:, 2 * d * HEAD:(2 * d + 1) * HEAD])
            gi = _sigmoid(g[:, (2 * d + 1) * HEAD:(2 * d + 2) * HEAD])
            a = jnp.exp(-LRU_C * r * sp[d])
            a, h = tile_scan(a, jnp.sqrt(1.0 - a * a) * (gi * xs), reverse=(d == 1))
            a_ref[rows, :] = a
            h_ref[rows, :] = h
        return 0

    lax.fori_loop(0, nb, gates_block, 0)

    def carry_tiles(tile0, n_tiles, state):
        def step(j, st):
            rf = pl.ds(pl.multiple_of((tile0 + j) * 8, 8), 8)
            rb = pl.ds(pl.multiple_of((tile0 + n_tiles - 1 - j) * 8, 8), 8)
            tf = hf_ref[rf, :] + af_ref[rf, :] * st[0]
            tb = hb_ref[rb, :] + ab_ref[rb, :] * st[1]
            hf_ref[rf, :] = tf
            hb_ref[rb, :] = tb
            return tf[7:8], tb[0:1]

        return lax.fori_loop(0, n_tiles, step, state, unroll=8)

    zero = jnp.zeros((1, HEAD), F32)
    carry_tiles(0, lat // 8, carry_tiles(lat // 8, ctx // 8, (zero, zero)))

    def out_block(blk, _):
        rows = pl.ds(pl.multiple_of(blk * TOK, TOK), TOK)
        o_ref[rows, :] = ((hf_ref[rows, :] + hb_ref[rows, :]) * _silu(z_ref[rows, :].astype(F32))).astype(BF16)
        return 0

    lax.fori_loop(0, nb, out_block, 0)


def _lru(p, conv_w, conv_b, w_gates, b_gates, lam, lat):
    t = p.shape[0]
    col = lambda c0: pl.BlockSpec((t, HEAD), lambda n: (0, c0 // HEAD + n))
    vec = lambda r: pl.BlockSpec((r, HEAD), lambda n: (0, n))
    return pl.pallas_call(
        functools.partial(_lru_kernel, lat=lat),
        grid=(LRU_BLOCKS,),
        in_specs=[col(P_BX), col(P_BZ), vec(4), vec(1),
                  pl.BlockSpec((1, HEAD, 4 * HEAD), lambda n: (n, 0, 0)),
                  pl.BlockSpec((1, 1, 4 * HEAD), lambda n: (n, 0, 0)), vec(2)],
        out_specs=pl.BlockSpec((t, HEAD), lambda n: (0, n)),
        out_shape=jax.ShapeDtypeStruct((t, LRU_WIDTH), BF16),
        scratch_shapes=[pltpu.VMEM((t, HEAD), F32)] * 4,
        compiler_params=_params("parallel"),
        name="lru",
    )(p, p, conv_w, conv_b.reshape(1, -1), w_gates, b_gates, lam)


def _dn_prep_kernel(q_ref, k_ref, v_ref, qp_ref, kp_ref, vp_ref, qx_ref, kx_ref, vx_ref, cw_ref, sm_ref,
                    alog_ref, dtb_ref, u_ref, w_ref, qd_ref, kd_ref, qk_ref, gl_ref, *, nb_lat):
    i = pl.program_id(0)
    tb = q_ref.shape[0]
    cc = DN_CHUNK
    first = jnp.logical_or(i == 0, i == nb_lat)
    last = jnp.logical_or(i == nb_lat - 1, i == nb_lat)
    cw = cw_ref[...]

    def conv_silu(x_ref, prev_ref, next_ref, c0):
        xe = jnp.concatenate([jnp.where(first, 0.0, prev_ref[...].astype(F32)), x_ref[...].astype(F32),
                              jnp.where(last, 0.0, next_ref[...].astype(F32))], axis=0)
        y = xe[HALO - 2:HALO - 2 + tb] * cw[0:1, c0:c0 + MIX]
        for j in range(1, 4):
            y = y + xe[HALO - 2 + j:HALO - 2 + j + tb] * cw[j:j + 1, c0:c0 + MIX]
        return _silu(y)

    q = conv_silu(q_ref, qp_ref, qx_ref, 0)
    k = conv_silu(k_ref, kp_ref, kx_ref, MIX)
    v = conv_silu(v_ref, vp_ref, vx_ref, 2 * MIX)

    sm = sm_ref[...].astype(F32)
    beta_all = _sigmoid(sm)
    g_all = -jnp.exp(alog_ref[...]) * _softplus(sm + dtb_ref[...])

    r = lax.broadcasted_iota(jnp.int32, (tb, tb), 0)
    c = lax.broadcasted_iota(jnp.int32, (tb, tb), 1)
    same = (r // cc) == (c // cc)
    tri_f = jnp.where(same, jnp.where(c <= r, 1.0, 0.0), 0.0).astype(BF16)
    tri_b = jnp.where(same, jnp.where(c >= r, 1.0, 0.0), 0.0).astype(BF16)
    gcs = (_dot_exact_lhs(tri_f, g_all), _dot_exact_lhs(tri_b, g_all))

    ii = lax.broadcasted_iota(jnp.int32, (cc, cc), 0)
    jj = lax.broadcasted_iota(jnp.int32, (cc, cc), 1)
    eye = jnp.where(ii == jj, 1.0, 0.0)
    blk16 = (ii // 16) == (jj // 16)
    blk32 = (ii // 32) == (jj // 32)
    off32 = jnp.logical_and(blk32, jnp.logical_not(blk16))
    off64 = jnp.logical_not(blk32)
    b16 = lambda m: m.astype(BF16)
    strict = (ii > jj, ii < jj)

    gl_ref[...] = jnp.zeros_like(gl_ref)
    qs, ks = [], []
    for h in range(DN_HEADS):
        hs = slice(h * HEAD, (h + 1) * HEAD)
        qh, kh = q[:, hs], k[:, hs]
        qs.append(qh * lax.rsqrt(jnp.sum(qh * qh, axis=-1, keepdims=True) + NORM_EPS) * (HEAD ** -0.5))
        ks.append(kh * lax.rsqrt(jnp.sum(kh * kh, axis=-1, keepdims=True) + NORM_EPS))

    def chunk_problems(ch):
        rows = slice(ch * cc, (ch + 1) * cc)
        gc = [gcs[d][rows] for d in range(2)]
        gct = [g.T for g in gc]
        kk, qk0 = [], []
        for h in range(DN_HEADS):
            k16 = b16(ks[h][rows])
            kk.append(_dg(k16, k16, NT))
            qk0.append(_dg(b16(qs[h][rows]), k16, NT))
        probs = []
        for d in range(2):
            end = cc - 1 if d == 0 else 0
            for h in range(DN_HEADS):
                hs = slice(h * HEAD, (h + 1) * HEAD)
                lb = SM_AB + d * 2 * DN_HEADS + h
                lg = lb + DN_HEADS
                beta = beta_all[rows, lb:lb + 1]
                gcol, grow = gc[d][:, lg:lg + 1], gct[d][lg:lg + 1, :]
                glast = gc[d][end:end + 1, lg:lg + 1]
                qh, kh, vh = qs[h][rows], ks[h][rows], v[rows, hs]
                decay = jnp.exp(jnp.where(strict[d], gcol - grow, NEG_BIG))
                eg = jnp.exp(gcol)
                kb = kh * beta
                lm = (beta * kk[h]) * decay
                qk_ref[d, h, rows, :] = (qk0[h] * (decay + eye)).astype(BF16)
                qd_ref[d, rows, hs] = (qh * eg).astype(BF16)
                kd_ref[d, rows, hs] = (kh * jnp.exp(glast - gcol)).astype(BF16)
                gl_ref[d, ch, h:h + 1, :] = jnp.broadcast_to(jnp.exp(glast), (1, HEAD))
                probs.append((d, hs, lm, jnp.concatenate([vh * beta, kb * eg], axis=1), rows))
        return probs

    for ch0 in range(0, tb // cc, DN_GROUP):
        probs = [pr for ch in range(ch0, ch0 + DN_GROUP) for pr in chunk_problems(ch)]
        lms = [pr[2] for pr in probs]
        diag = [jnp.where(blk16, lm, 0.0) for lm in lms]
        xs = [eye - m for m in diag]
        pw = [b16(m) for m in diag]
        pw = [_dg(m, m) for m in pw]
        for _ in range(2):
            pw = [b16(m) for m in pw]
            xs = [x + _dg(b16(x), m) for x, m in zip(xs, pw)]
            pw = [_dg(m, m) for m in pw]
        xs = [x + _dg(b16(x), b16(m)) for x, m in zip(xs, pw)]
        for off in (off32, off64):
            x16 = [b16(x) for x in xs]
            cx = [_dg(b16(jnp.where(off, lm, 0.0)), x) for lm, x in zip(lms, x16)]
            xs = [x - _dg(xb, b16(c)) for x, xb, c in zip(xs, x16, cx)]
        sols = [_dg(b16(x), b16(pr[3])) for x, pr in zip(xs, probs)]
        for sol, (d, hs, _, _, rows) in zip(sols, probs):
            u_ref[d, rows, hs] = sol[:, :HEAD]
            w_ref[d, rows, hs] = sol[:, HEAD:].astype(BF16)


def _dn_prep(p, conv_w, alog_row, dtb_row, lat):
    t = p.shape[0]
    tb = TOK
    nb, nb_lat = t // tb, lat // tb
    rh = tb // HALO
    blk = lambda c0: pl.BlockSpec((tb, MIX), lambda i: (i, c0 // MIX))
    prev = lambda c0: pl.BlockSpec((HALO, MIX), lambda i: (jnp.maximum(i * rh - 1, 0), c0 // MIX))
    nxt = lambda c0: pl.BlockSpec((HALO, MIX), lambda i: (jnp.minimum((i + 1) * rh, t // HALO - 1), c0 // MIX))
    cols = (P_CQ, P_CK, P_CV)
    big = pl.BlockSpec((2, tb, MIX), lambda i: (0, i, 0))
    big_shape = lambda dt: jax.ShapeDtypeStruct((2, t, MIX), dt)
    return pl.pallas_call(
        functools.partial(_dn_prep_kernel, nb_lat=nb_lat),
        grid=(nb,),
        in_specs=[blk(c0) for c0 in cols] + [prev(c0) for c0 in cols] + [nxt(c0) for c0 in cols] + [
            pl.BlockSpec((4, 3 * MIX), lambda i: (0, 0)),
            pl.BlockSpec((tb, HEAD), lambda i: (i, P_SM // HEAD)),
            pl.BlockSpec((1, HEAD), lambda i: (0, 0)),
            pl.BlockSpec((1, HEAD), lambda i: (0, 0))],
        out_specs=[big, big, big, big,
                   pl.BlockSpec((2, DN_HEADS, tb, DN_CHUNK), lambda i: (0, 0, i, 0)),
                   pl.BlockSpec((2, tb // DN_CHUNK, 8, HEAD), lambda i: (0, i, 0, 0))],
        out_shape=[big_shape(F32), big_shape(BF16), big_shape(BF16), big_shape(BF16),
                   jax.ShapeDtypeStruct((2, DN_HEADS, t, DN_CHUNK), BF16),
                   jax.ShapeDtypeStruct((2, t // DN_CHUNK, 8, HEAD), F32)],
        compiler_params=_params("parallel"),
        name="dn_prep",
    )(*([p] * 9), conv_w, p, alog_row, dtb_row)


def _dn_scan_kernel(uf, wf, qdf, kdf, qkf, glf, ub, wb, qdb, kdb, qkb, glb, of_ref, ob_ref, s_ref):
    @pl.when(pl.program_id(0) == 0)
    def _():
        s_ref[...] = jnp.zeros_like(s_ref)

    cc = DN_CHUNK
    n_chunks = uf.shape[1] // cc
    dirs = ((uf, wf, qdf, kdf, qkf, glf, of_ref), (ub, wb, qdb, kdb, qkb, glb, ob_ref))
    probs = [(d, h, slice(h * HEAD, (h + 1) * HEAD)) for d in range(2) for h in range(DN_HEADS)]
    for step in range(n_chunks):
        ch = (step, n_chunks - 1 - step)
        rows = tuple(slice(c * cc, (c + 1) * cc) for c in ch)
        ws = [_dg(jnp.concatenate([dirs[d][1][0, rows[d], hs], dirs[d][2][0, rows[d], hs]], axis=0),
                  s_ref[d, h].astype(BF16)) for d, h, hs in probs]
        v_new = [(dirs[d][0][0, rows[d], hs] - t[:cc]).astype(BF16) for t, (d, h, hs) in zip(ws, probs)]
        outs = [t[cc:] + _dg(dirs[d][4][0, h, rows[d], :], vn) for t, vn, (d, h, hs) in zip(ws, v_new, probs)]
        upd = [_dg(dirs[d][3][0, rows[d], hs], vn, TN) for vn, (d, h, hs) in zip(v_new, probs)]
        for o, ds, (d, h, hs) in zip(outs, upd, probs):
            dirs[d][6][rows[d], hs] = o
            s_ref[d, h] = s_ref[d, h] * dirs[d][5][0, ch[d], h:h + 1, :] + ds


def _dn_scan(u, w, qd, kd, qk, gl, lat):
    t = u.shape[1]
    tb = TOK
    cpb = tb // DN_CHUNK
    n, n_lat = t // tb, lat // tb
    n_ctx = n - n_lat
    bf = lambda i: jnp.where(i < n_ctx, n_lat + i, i - n_ctx)
    bb = lambda i: n - 1 - i
    specs = []
    for d, blk in ((0, bf), (1, bb)):
        big = pl.BlockSpec((1, tb, MIX), lambda i, d=d, blk=blk: (d, blk(i), 0))
        specs += [big, big, big, big,
                  pl.BlockSpec((1, DN_HEADS, tb, DN_CHUNK), lambda i, d=d, blk=blk: (d, 0, blk(i), 0)),
                  pl.BlockSpec((1, cpb, 8, HEAD), lambda i, d=d, blk=blk: (d, blk(i), 0, 0))]
    return pl.pallas_call(
        _dn_scan_kernel,
        grid=(n,),
        in_specs=specs,
        out_specs=[pl.BlockSpec((tb, MIX), lambda i: (bf(i), 0)), pl.BlockSpec((tb, MIX), lambda i: (bb(i), 0))],
        out_shape=[jax.ShapeDtypeStruct((t, MIX), F32)] * 2,
        scratch_shapes=[pltpu.VMEM((2, DN_HEADS, HEAD, HEAD), F32)],
        compiler_params=_params("arbitrary"),
        name="dn_scan",
    )(u, w, qd, kd, qk, gl, u, w, qd, kd, qk, gl)


def _dn_out_kernel(of_ref, ob_ref, z_ref, nw_ref, o_ref):
    for h in range(DN_HEADS):
        hs = slice(h * HEAD, (h + 1) * HEAD)
        o = of_ref[:, hs] + ob_ref[:, hs]
        o_ref[:, hs] = (_rms(o, nw_ref[...], HEAD) * _silu(z_ref[:, hs].astype(F32))).astype(BF16)


def _dn_out(o_f, o_b, p, norm_w):
    t = p.shape[0]
    tb = _row_tile(t)
    row = lambda c: pl.BlockSpec((tb, MIX), lambda i: (i, c))
    return pl.pallas_call(
        _dn_out_kernel,
        grid=(t // tb,),
        in_specs=[row(0), row(0), row(P_CZ // MIX), pl.BlockSpec((1, HEAD), lambda i: (0, 0))],
        out_specs=row(0),
        out_shape=jax.ShapeDtypeStruct((t, MIX), BF16),
        compiler_params=_params("parallel"),
        name="dn_out",
    )(o_f, o_b, p, norm_w.reshape(1, HEAD))


def _arrange_w_in(w_in):
    pad = jnp.zeros(w_in.shape[:2] + (P_W - w_in.shape[2],), w_in.dtype)
    tail = [w_in[..., 5328:5840],
            w_in[..., 5008:5264],
            w_in[..., 4624:5008],
            w_in[..., 5264:5328],
            w_in[..., 4608:4624],
            pad]
    def tiles(w):
        l, d, n = w.shape
        return w.reshape(l, d, n // W_TILE, W_TILE).transpose(0, 2, 1, 3).astype(BF16)

    return tiles(w_in[..., :P_MAIN]), tiles(jnp.concatenate(tail, axis=-1))


def _arrange_mla_wq(w_uq):
    l, r, _ = w_uq.shape
    w = w_uq.reshape(l, r, MLA_HEADS, MLA_QK)
    w = jnp.pad(w, ((0, 0), (0, 0), (0, 0), (0, 2 * HEAD - MLA_QK)))
    return w.reshape(l, r, MLA_HEADS * 2 * HEAD).astype(BF16)


def _pad_qk_norm(w):
    return jnp.pad(w, ((0, 0), (0, 2 * HEAD - MLA_QK)))[:, None, :]


def _small_lane_row(vals):
    l = vals.shape[0]
    row = jnp.zeros((l, HEAD), F32)
    for d in range(2):
        lo = SM_AB + d * 2 * DN_HEADS + DN_HEADS
        row = row.at[:, lo:lo + DN_HEADS].set(vals[:, d, :])
    return row[:, None, :]


def _arrange_lru_gates(w_a, b_a, w_x, b_x):
    w = jnp.concatenate([w_a[:, 0], w_x[:, 0], w_a[:, 1], w_x[:, 1]], axis=-1)
    l = b_a.shape[0]
    blk = lambda b, d: b[:, d].reshape(l, LRU_BLOCKS, 1, HEAD)
    b = jnp.concatenate([blk(b_a, 0), blk(b_x, 0), blk(b_a, 1), blk(b_x, 1)], axis=-1)
    return w.astype(BF16), b


def _rope_tables(lat, ctx):
    n_rows = lat // GRID_W
    pos = jnp.arange(max(n_rows, GRID_W), dtype=F32)
    lane = jnp.arange(HEAD)

    def table(half, width):
        inv_freq = ROPE_THETA ** (-jnp.arange(half, dtype=F32) / half)
        ang = pos[:, None] * inv_freq[lane % half][None, :]
        live = (lane < width)[None, :]
        by_row = ((lane // (2 * half)) == 0)[None, :]
        sign = jnp.where((lane % (2 * half)) < half, -1.0, 1.0)[None, :]

        def expand(f):
            per_row = jnp.repeat(f[:n_rows], GRID_W, axis=0)
            per_col = jnp.tile(f[:GRID_W], (n_rows, 1))
            return jnp.where(by_row, per_row, per_col)

        cos = jnp.where(live, expand(jnp.cos(ang)), 0.0)
        sin = jnp.where(live, expand(jnp.sin(ang)), 0.0) * sign
        cos_c = jnp.broadcast_to(jnp.where(live, 1.0, 0.0), (ctx, HEAD))
        return (jnp.concatenate([cos, cos_c], axis=0).astype(F32),
                jnp.concatenate([sin, jnp.zeros((ctx, HEAD), F32)], axis=0).astype(F32))

    return table(32, HEAD), table(16, MLA_ROPE)


def kernel(x, c, ctx, c_ctx, norm_w, w_ada, b_ada, w_in, w_out, attn_q_norm, attn_k_norm, lru_conv_w, lru_conv_b, lru_w_a, lru_b_a, lru_w_x, lru_b_x, lru_lambda, dn_conv_w, dn_a_log, dn_dt_bias, dn_norm_w, mla_q_norm, mla_kv_norm, mla_w_uq, mla_w_ukv, mla_q_qk_norm, mla_k_qk_norm):
    assert x.shape[0] == 1 and ctx.shape[1] == TOK and x.shape[1] % (8 * TOK) == 0
    lat, n_ctx = x.shape[1], ctx.shape[1]
    depth = w_in.shape[0]

    xs = jnp.concatenate([x[0], ctx[0]], axis=0)
    mod = _ada(jnp.stack([c[0], c_ctx], axis=1), w_ada, b_ada)
    (cos_a, sin_a), (cos_m, sin_m) = _rope_tables(lat, n_ctx)
    w_main, w_tail = _arrange_w_in(w_in)
    w_out_r = w_out.reshape(depth, 4, MIX, D_MODEL).astype(BF16)
    wq_r = _arrange_mla_wq(mla_w_uq)
    wkv_r = mla_w_ukv.astype(BF16)
    qn_r, kn_r = _pad_qk_norm(mla_q_qk_norm), _pad_qk_norm(mla_k_qk_norm)
    alog_r, dtb_r = _small_lane_row(dn_a_log), _small_lane_row(dn_dt_bias)
    lru_w, lru_b = _arrange_lru_gates(lru_w_a, lru_b_a, lru_w_x, lru_b_x)

    for l in range(depth):
        p = _inproj(xs, norm_w[l], mod[l, 0], mod[l, 1], w_main, w_tail, l, lat)
        qa, ka, vta = _gqa_prep(p, cos_a, sin_a, attn_q_norm[l], attn_k_norm[l])
        bound_a = jnp.max(jnp.abs(attn_q_norm[l])) * jnp.max(jnp.abs(attn_k_norm[l])) * (HEAD ** 0.5 * LOG2E)
        y_a = _flash(qa, ka, p, P_AV, vta, p, P_AZ, A_HEADS, A_KV_HEADS, HEAD, lat, bound_a, "gqa_attn")
        y_b = _lru(p, lru_conv_w[l], lru_conv_b[l], lru_w[l], lru_b[l], lru_lambda[l], lat)
        u, w, qd, kd, qk, gl = _dn_prep(p, dn_conv_w[l], alog_r[l], dtb_r[l], lat)
        o_f, o_b = _dn_scan(u, w, qd, kd, qk, gl, lat)
        y_c = _dn_out(o_f, o_b, p, dn_norm_w[l])
        qm, km, vm, vtm = _mla_prep(p, cos_m, sin_m, mla_q_norm[l], mla_kv_norm[l], wq_r[l], wkv_r[l],
                               qn_r[l], kn_r[l])
        bound_d = (jnp.max(jnp.abs(mla_q_qk_norm[l])) * jnp.max(jnp.abs(mla_k_qk_norm[l]))
                   * (MLA_QK ** 0.5 * LOG2E))
        y_d = _flash(qm, km, vm, 0, vtm, p, P_DZ, MLA_HEADS, MLA_HEADS, 2 * HEAD, lat, bound_d, "mla_attn")
        last = l == depth - 1
        xs = _outproj((y_a, y_b, y_c, y_d), w_out_r, l, xs, mod[l, 2], lat, lat if last else lat + n_ctx)
    return xs[None]
```

```python
import functools
import math

import jax
import jax.numpy as jnp
from jax import lax
from jax.experimental import pallas as pl
from jax.experimental.pallas import tpu as pltpu

F32 = jnp.float32
BF16 = jnp.bfloat16

D_MODEL = 2048
DEPTH = 4
GRID_W = 64
ROPE_THETA = 10000.0
NORM_EPS = 1e-6
HEAD = 128
A_HEADS, A_KV_HEADS = 4, 2
LRU_WIDTH, LRU_BLOCKS, LRU_C = 512, 4, 8.0
DN_HEADS, DN_CHUNK = 4, 64
DN_GROUP = 4
MLA_HEADS, MLA_Q_RANK, MLA_KV_RANK, MLA_NOPE, MLA_ROPE = 4, 384, 256, 128, 64
MLA_QK = MLA_NOPE + MLA_ROPE
MIX = 512

P_AQ, P_AK, P_AV, P_AZ = 0, 512, 768, 1024
P_BX, P_BZ = 1536, 2048
P_CQ, P_CK, P_CV, P_CZ = 2560, 3072, 3584, 4096
P_DZ, P_DCKV, P_DCQ = 4608, 5120, 5376
P_SM = 5760
P_W = 6144
P_MAIN = 4608
W_TILE = 768
IN_ROWS = 1408
OUT_TILE = 1024
ADA_TILE = 512
FLASH_TQ, FLASH_TK = 512, 4096
SM_AB = MLA_ROPE

TOK = 256
HALO = 16
VMEM_LIMIT = 56 * 1024 * 1024
LOG2E = math.log2(math.e)
NEG_BIG = -1e30
MAX_EXP2_LOGIT = 60.0

NN = ((1,), (0,))
NT = ((1,), (1,))
TN = ((0,), (0,))


def _dg(a, b, dims=NN):
    return lax.dot_general(a, b, (dims, ((), ())), preferred_element_type=F32)


def _dot_exact_lhs(m, b):
    b0 = b.astype(BF16)
    r1 = b - b0.astype(F32)
    b1 = r1.astype(BF16)
    b2 = (r1 - b1.astype(F32)).astype(BF16)
    return _dg(m, b0) + (_dg(m, b1) + _dg(m, b2))


def _sigmoid(x):
    return 1.0 / (1.0 + jnp.exp(-x))


def _silu(x):
    return x * _sigmoid(x)


def _softplus(x):
    return jnp.maximum(x, 0.0) + jnp.log(1.0 + jnp.exp(-jnp.abs(x)))


def _rms(x, w, n):
    return x * lax.rsqrt(jnp.sum(x * x, axis=-1, keepdims=True) * (1.0 / n) + NORM_EPS) * w


def _rope(x, cos, sin_signed, half):
    lane = lax.broadcasted_iota(jnp.int32, x.shape, 1)
    first = (lane % (2 * half)) < half
    rot = jnp.where(first, pltpu.roll(x, HEAD - half, 1), pltpu.roll(x, half, 1))
    return x * cos + rot * sin_signed


def _params(*sem):
    return pltpu.CompilerParams(dimension_semantics=sem, vmem_limit_bytes=VMEM_LIMIT)


def _row_tile(rows):
    for t in (768, 512, 256):
        if rows % t == 0:
            return t
    raise ValueError(f"row count {rows} is not a multiple of {TOK}")


def _ada_kernel(c_ref, w_ref, b_ref, o_ref):
    s = _silu(c_ref[...])
    w = w_ref[0]
    r0 = jnp.sum(s[:, 0:1] * w, axis=0, keepdims=True)
    r1 = jnp.sum(s[:, 1:2] * w, axis=0, keepdims=True)
    o_ref[0, 0] = jnp.concatenate([r0, r1], axis=0) + b_ref[0]


def _ada(c_cols, w_ada, b_ada):
    depth, d, _ = w_ada.shape
    tn = ADA_TILE
    per = d // tn
    return pl.pallas_call(
        _ada_kernel,
        grid=(depth, 3 * per),
        in_specs=[
            pl.BlockSpec((d, 2), lambda l, j: (0, 0)),
            pl.BlockSpec((1, d, tn), lambda l, j: (l, 0, j)),
            pl.BlockSpec((1, 1, tn), lambda l, j: (l, 0, j)),
        ],
        out_specs=pl.BlockSpec((1, 1, 2, tn), lambda l, j: (l, j // per, 0, j % per)),
        out_shape=jax.ShapeDtypeStruct((depth, 3, 2, d), F32),
        compiler_params=_params("parallel", "parallel"),
        name="ada",
    )(c_cols, w_ada, b_ada.reshape(depth, 1, 3 * d))


def _inproj_kernel(x_ref, nw_ref, shift_ref, scale_ref, wm_ref, wt_ref, o_ref, h_ref, *, lat, n_main):
    i = pl.program_id(0)
    tm = x_ref.shape[0]

    @pl.when(pl.program_id(1) == 0)
    def _():
        rc = 32
        gain = nw_ref[...] * (1.0 + scale_ref[...])
        shift = shift_ref[...]

        def chunk(ci, _):
            rows = pl.ds(pl.multiple_of(ci * rc, rc), rc)
            is_ctx = i * tm + ci * rc >= lat
            g = jnp.where(is_ctx, gain[1:2, :], gain[0:1, :])
            b = jnp.where(is_ctx, shift[1:2, :], shift[0:1, :])
            x = x_ref[rows, :]
            r = lax.rsqrt(jnp.sum(x * x, axis=-1, keepdims=True) * (1.0 / D_MODEL) + NORM_EPS)
            h_ref[rows, :] = ((x * r) * g + b).astype(BF16)
            return 0

        lax.fori_loop(0, tm // rc, chunk, 0)

    @pl.when(pl.program_id(1) < n_main)
    def _():
        o_ref[...] = _dg(h_ref[...], wm_ref[...]).astype(o_ref.dtype)

    @pl.when(pl.program_id(1) >= n_main)
    def _():
        o_ref[...] = _dg(h_ref[...], wt_ref[...]).astype(o_ref.dtype)


def _inproj(xs, norm_w, shift, scale, w_main, w_tail, layer, lat):
    t, d = xs.shape
    tm, tn = (IN_ROWS if t % IN_ROWS == 0 else _row_tile(t)), W_TILE
    n_main = P_MAIN // tn
    return pl.pallas_call(
        functools.partial(_inproj_kernel, lat=lat, n_main=n_main),
        grid=(t // tm, P_W // tn),
        in_specs=[
            pl.BlockSpec((tm, d), lambda i, j: (i, 0)),
            pl.BlockSpec((1, d), lambda i, j: (0, 0)),
            pl.BlockSpec((2, d), lambda i, j: (0, 0)),
            pl.BlockSpec((2, d), lambda i, j: (0, 0)),
            pl.BlockSpec((None, d, tn), lambda i, j: (layer, 0, jnp.minimum(j, n_main - 1))),
            pl.BlockSpec((None, d, tn), lambda i, j: (layer, 0, jnp.maximum(j - n_main, 0))),
        ],
        out_specs=pl.BlockSpec((tm, tn), lambda i, j: (i, j)),
        out_shape=jax.ShapeDtypeStruct((t, P_W), BF16),
        scratch_shapes=[pltpu.VMEM((tm, d), BF16)],
        compiler_params=_params("parallel", "arbitrary"),
        name="inproj",
    )(xs, norm_w.reshape(1, d), shift, scale, w_main, w_tail)


def _outproj_kernel(ya_ref, yb_ref, yc_ref, yd_ref, w_ref, x_ref, g_ref, o_ref, *, lat):
    i = pl.program_id(0)
    tm = x_ref.shape[0]
    acc = _dg(ya_ref[...], w_ref[0])
    acc += _dg(yb_ref[...], w_ref[1])
    acc += _dg(yc_ref[...], w_ref[2])
    acc += _dg(yd_ref[...], w_ref[3])
    row = i * tm + lax.broadcasted_iota(jnp.int32, (tm, 1), 0)
    gate = jnp.where(row >= lat, g_ref[1:2, :], g_ref[0:1, :])
    o_ref[...] = x_ref[...] + gate * acc


def _outproj(ys, w_out, layer, xs, gate, lat, out_rows):
    d = xs.shape[1]
    tm, tn = _row_tile(out_rows), OUT_TILE
    yspec = pl.BlockSpec((tm, MIX), lambda i, j: (i, 0))
    return pl.pallas_call(
        functools.partial(_outproj_kernel, lat=lat),
        grid=(out_rows // tm, d // tn),
        in_specs=[yspec, yspec, yspec, yspec,
                  pl.BlockSpec((None, 4, MIX, tn), lambda i, j: (layer, 0, 0, j)),
                  pl.BlockSpec((tm, tn), lambda i, j: (i, j)),
                  pl.BlockSpec((2, tn), lambda i, j: (0, j))],
        out_specs=pl.BlockSpec((tm, tn), lambda i, j: (i, j)),
        out_shape=jax.ShapeDtypeStruct((out_rows, d), F32),
        compiler_params=_params("parallel", "parallel"),
        name="outproj",
    )(*ys, w_out, xs, gate)


def _flash_kernel(q_ref, k_ref, v_ref, z_ref, o_ref, *, n_loop, tk, tail, dq, shared_kv, bounded, out_row0=0):
    tq = q_ref.shape[0]
    nh = q_ref.shape[1] // dq
    if out_row0:
        o_ref[0:out_row0, :] = jnp.zeros((out_row0, o_ref.shape[1]), o_ref.dtype)

    def scores(c, rows):
        kc = 0 if shared_kv else c
        s = _dg(q_ref[:, c * dq:(c + 1) * dq], k_ref[rows, kc * dq:(kc + 1) * dq], NT)
        return s, v_ref[rows, kc * HEAD:(kc + 1) * HEAD]

    def attend_bounded(c, carry, rows):
        l, acc = carry
        s, v = scores(c, rows)
        p = jnp.exp2(s)
        for j in range(s.shape[1] // HEAD):
            l = l + p[:, j * HEAD:(j + 1) * HEAD]
        return l, acc + _dg(p.astype(BF16), v)

    def attend_online(c, carry, rows):
        m, l, acc = carry
        s, v = scores(c, rows)
        m_new = jnp.maximum(m, jnp.max(s, axis=-1, keepdims=True))
        alpha = jnp.exp2(m - m_new)
        p = jnp.exp2(s - m_new)
        l = alpha * l + jnp.sum(p, axis=-1, keepdims=True)
        return m_new, l, alpha * acc + _dg(p.astype(BF16), v)

    zeros = jnp.zeros((tq, HEAD), F32)
    if bounded:
        attend, init = attend_bounded, (zeros, zeros)
    else:
        attend, init = attend_online, (jnp.full((tq, 1), -1e30, F32), jnp.zeros((tq, 1), F32), zeros)

    def step(kb, carries):
        rows = pl.ds(pl.multiple_of(kb * tk, tk), tk)
        return tuple(attend(c, carries[c], rows) for c in range(nh))

    carries = (init,) * nh
    if n_loop:
        carries = lax.fori_loop(0, n_loop, step, carries, unroll=2)
    for c in range(nh):
        carry = attend(c, carries[c], pl.ds(tail[0], tail[1]))
        l, acc = carry[-2], carry[-1]
        cs = slice(c * HEAD, (c + 1) * HEAD)
        gate = _silu(z_ref[:, cs].astype(F32))
        o_ref[out_row0:out_row0 + tq, cs] = (acc / jnp.sum(l, axis=-1, keepdims=True) * gate).astype(o_ref.dtype)


def _flash_lat_kernel(q_ref, k_ref, v_ref, z_ref, y_ref, o_ref, **kw):
    _flash_kernel(q_ref, k_ref, v_ref, z_ref, o_ref, **kw)


def _flash_t_kernel(q_ref, k_ref, vt_ref, z_ref, y_ref, o_ref, *, n_loop, tk, n_tail, dq, shared_kv):
    tq = q_ref.shape[0]
    nh = q_ref.shape[1] // dq
    sub = tk // TOK

    def attend(c, carry, row0, tile0, n_tiles):
        l, acc = carry
        kc = 0 if shared_kv else c
        keys = n_tiles * TOK
        s = _dg(k_ref[pl.ds(row0, keys), kc * dq:(kc + 1) * dq], q_ref[:, c * dq:(c + 1) * dq], NT)
        p = jnp.exp2(s)
        for r in range(keys // 8):
            l = l + p[r * 8:(r + 1) * 8, :]
        p = p.astype(BF16)
        for j in range(n_tiles):
            acc = acc + _dg(vt_ref[tile0 + j, kc * HEAD:(kc + 1) * HEAD, :], p[j * TOK:(j + 1) * TOK, :])
        return l, acc

    init = (jnp.zeros((8, tq), F32), jnp.zeros((HEAD, tq), F32))

    def step(kb, carries):
        return tuple(attend(c, carries[c], pl.multiple_of(kb * tk, tk), kb * sub, sub) for c in range(nh))

    carries = lax.fori_loop(0, n_loop, step, (init,) * nh, unroll=2)
    for c in range(nh):
        l, acc = attend(c, carries[c], n_loop * tk, n_loop * sub, n_tail)
        cs = slice(c * HEAD, (c + 1) * HEAD)
        out = (acc / jnp.sum(l, axis=0, keepdims=True)).T
        o_ref[:, cs] = (out * _silu(z_ref[:, cs].astype(F32))).astype(o_ref.dtype)


def _flash(q, k, v, v_col, vt, p, z_col, heads, kv_heads, dq, lat, logit_bound, name):
    t = q.shape[0]
    ctx = t - lat
    nh = 2
    shared_kv = heads // kv_heads == nh
    nk = 1 if shared_kv else nh
    tq = FLASH_TQ if lat % FLASH_TQ == 0 else TOK
    tk = FLASH_TK if lat % (2 * FLASH_TK) == 0 else TOK
    zb = z_col // (nh * HEAD)
    vb = v_col // (nk * HEAD)
    cb = lat // ctx
    out_shape = jax.ShapeDtypeStruct((t, heads * HEAD), BF16)

    def call(bounded):
        common = dict(tk=tk, dq=dq, shared_kv=shared_kv, bounded=bounded)
        suffix = "_bounded" if bounded else "_online"
        y = pl.pallas_call(
            functools.partial(_flash_kernel, n_loop=0, tail=(0, ctx), out_row0=lat, **common),
            grid=(heads // nh,),
            in_specs=[
                pl.BlockSpec((ctx, nh * dq), lambda g: (cb, g)),
                pl.BlockSpec((ctx, nk * dq), lambda g: (cb, g)),
                pl.BlockSpec((ctx, nk * HEAD), lambda g: (cb, vb + g)),
                pl.BlockSpec((ctx, nh * HEAD), lambda g: (cb, zb + g)),
            ],
            out_specs=pl.BlockSpec((t, nh * HEAD), lambda g: (0, g)),
            out_shape=out_shape,
            compiler_params=_params("parallel"),
            name=name + "_ctx" + suffix,
        )(q, k, v, p)
        if bounded:
            body = functools.partial(_flash_t_kernel, n_loop=lat // tk, tk=tk, n_tail=ctx // TOK, dq=dq,
                                     shared_kv=shared_kv)
            v_spec, v_arg = pl.BlockSpec((t // TOK, nk * HEAD, TOK), lambda g, i: (0, g, 0)), vt
        else:
            body = functools.partial(_flash_lat_kernel, n_loop=lat // tk, tail=(lat, ctx), **common)
            v_spec, v_arg = pl.BlockSpec((t, nk * HEAD), lambda g, i: (0, vb + g)), v
        return pl.pallas_call(
            body,
            grid=(heads // nh, lat // tq),
            in_specs=[
                pl.BlockSpec((tq, nh * dq), lambda g, i: (i, g)),
                pl.BlockSpec((t, nk * dq), lambda g, i: (0, g)),
                v_spec,
                pl.BlockSpec((tq, nh * HEAD), lambda g, i: (i, zb + g)),
                pl.BlockSpec(memory_space=pl.ANY),
            ],
            out_specs=pl.BlockSpec((tq, nh * HEAD), lambda g, i: (i, g)),
            out_shape=out_shape,
            input_output_aliases={4: 0},
            compiler_params=_params("parallel", "parallel"),
            name=name + suffix,
        )(q, k, v_arg, p, y)

    return lax.cond(logit_bound <= MAX_EXP2_LOGIT, lambda: call(True), lambda: call(False))


def _gqa_prep_kernel(q_ref, k_ref, v_ref, cos_ref, sin_ref, qn_ref, kn_ref, oq_ref, ok_ref, ovt_ref):
    cos, sin = cos_ref[...], sin_ref[...]
    scale = HEAD ** -0.5 * LOG2E
    for h in range(A_HEADS):
        c = slice(h * HEAD, (h + 1) * HEAD)
        oq_ref[:, c] = (_rope(_rms(q_ref[:, c].astype(F32), qn_ref[...], HEAD), cos, sin, 32) * scale).astype(BF16)
    for g in range(A_KV_HEADS):
        c = slice(g * HEAD, (g + 1) * HEAD)
        ok_ref[:, c] = _rope(_rms(k_ref[:, c].astype(F32), kn_ref[...], HEAD), cos, sin, 32).astype(BF16)
    for j in range(v_ref.shape[0] // TOK):
        ovt_ref[j] = v_ref[j * TOK:(j + 1) * TOK, :].astype(F32).T.astype(BF16)


def _gqa_prep(p, cos, sin, qn, kn):
    t = p.shape[0]
    tb = _row_tile(t)
    qw, kw = A_HEADS * HEAD, A_KV_HEADS * HEAD
    row = lambda w, c: pl.BlockSpec((tb, w), lambda i: (i, c))
    vec = pl.BlockSpec((1, HEAD), lambda i: (0, 0))
    return pl.pallas_call(
        _gqa_prep_kernel,
        grid=(t // tb,),
        in_specs=[row(qw, P_AQ // qw), row(kw, P_AK // kw), row(kw, P_AV // kw), row(HEAD, 0), row(HEAD, 0), vec, vec],
        out_specs=[row(qw, 0), row(kw, 0), pl.BlockSpec((tb // TOK, kw, TOK), lambda i: (i, 0, 0))],
        out_shape=[jax.ShapeDtypeStruct((t, qw), BF16), jax.ShapeDtypeStruct((t, kw), BF16),
                   jax.ShapeDtypeStruct((t // TOK, kw, TOK), BF16)],
        compiler_params=_params("parallel"),
        name="gqa_prep",
    )(p, p, p, cos, sin, qn.reshape(1, HEAD), kn.reshape(1, HEAD))


def _mla_prep_kernel(cq_ref, ckv_ref, sm_ref, cos_ref, sin_ref, qnw_ref, kvnw_ref, wq_ref, wkv_ref,
                     qn_ref, kn_ref, oq_ref, ok_ref, ov_ref, ovt_ref):
    cos, sin = cos_ref[...], sin_ref[...]
    scale = MLA_QK ** -0.5 * LOG2E
    qn, kn = qn_ref[...], kn_ref[...]
    q = _dg(_rms(cq_ref[...].astype(F32), qnw_ref[...], MLA_Q_RANK).astype(BF16), wq_ref[...])
    kv = _dg(_rms(ckv_ref[...].astype(F32), kvnw_ref[...], MLA_KV_RANK).astype(BF16), wkv_ref[...])
    lane = lax.broadcasted_iota(jnp.int32, sm_ref.shape, 1)
    kr = jnp.where(lane < MLA_ROPE, sm_ref[...].astype(F32), 0.0)
    kr_ss = jnp.sum(kr * kr, axis=-1, keepdims=True)
    for h in range(MLA_HEADS):
        lo = slice(2 * h * HEAD, (2 * h + 1) * HEAD)
        hi = slice((2 * h + 1) * HEAD, (2 * h + 2) * HEAD)
        q_nope, q_rope = q[:, lo], q[:, hi]
        r = lax.rsqrt((jnp.sum(q_nope * q_nope, axis=-1, keepdims=True)
                       + jnp.sum(q_rope * q_rope, axis=-1, keepdims=True)) * (1.0 / MLA_QK) + NORM_EPS)
        oq_ref[:, lo] = (q_nope * r * qn[:, :HEAD] * scale).astype(BF16)
        oq_ref[:, hi] = (_rope(q_rope * r * qn[:, HEAD:], cos, sin, 16) * scale).astype(BF16)
        k_nope, v = kv[:, lo], kv[:, hi]
        r = lax.rsqrt((jnp.sum(k_nope * k_nope, axis=-1, keepdims=True) + kr_ss) * (1.0 / MLA_QK) + NORM_EPS)
        ok_ref[:, lo] = (k_nope * r * kn[:, :HEAD]).astype(BF16)
        ok_ref[:, hi] = _rope(kr * r * kn[:, HEAD:], cos, sin, 16).astype(BF16)
        ov_ref[:, h * HEAD:(h + 1) * HEAD] = v.astype(BF16)
        for j in range(v.shape[0] // TOK):
            ovt_ref[j, h * HEAD:(h + 1) * HEAD, :] = v[j * TOK:(j + 1) * TOK, :].T.astype(BF16)


def _mla_prep(p, cos, sin, q_norm_w, kv_norm_w, wq, wkv, qn, kn):
    t = p.shape[0]
    tb = _row_tile(t)
    w2 = MLA_HEADS * 2 * HEAD
    row = lambda w, c: pl.BlockSpec((tb, w), lambda i: (i, c))
    full = lambda a: pl.BlockSpec(a.shape, lambda i: (0, 0))
    args = (q_norm_w.reshape(1, -1), kv_norm_w.reshape(1, -1), wq, wkv, qn, kn)
    return pl.pallas_call(
        _mla_prep_kernel,
        grid=(t // tb,),
        in_specs=[row(MLA_Q_RANK, P_DCQ // MLA_Q_RANK), row(MLA_KV_RANK, P_DCKV // MLA_KV_RANK),
                  row(HEAD, P_SM // HEAD), row(HEAD, 0), row(HEAD, 0)] + [full(a) for a in args],
        out_specs=[row(w2, 0), row(w2, 0), row(MLA_HEADS * HEAD, 0),
                   pl.BlockSpec((tb // TOK, MLA_HEADS * HEAD, TOK), lambda i: (i, 0, 0))],
        out_shape=[jax.ShapeDtypeStruct((t, w2), BF16), jax.ShapeDtypeStruct((t, w2), BF16),
                   jax.ShapeDtypeStruct((t, MLA_HEADS * HEAD), BF16),
                   jax.ShapeDtypeStruct((t // TOK, MLA_HEADS * HEAD, TOK), BF16)],
        compiler_params=_params("parallel"),
        name="mla_prep",
    )(p, p, p, cos, sin, *args)


def _lru_kernel(ux_ref, z_ref, cw_ref, cb_ref, w_ref, b_ref, lam_ref, o_ref, af_ref, ab_ref, hf_ref, hb_ref, *, lat):
    t = ux_ref.shape[0]
    ctx = t - lat
    nb = t // TOK
    nb_lat = lat // TOK
    cw, cb = cw_ref[...], cb_ref[...]
    bias = b_ref[0]
    sp = (_softplus(-lam_ref[0:1, :]), _softplus(-lam_ref[1:2, :]))
    dirs = ((af_ref, hf_ref), (ab_ref, hb_ref))
    sub = lax.broadcasted_iota(jnp.int32, (TOK, HEAD), 0) % 8

    def tile_scan(a, b, reverse):
        for k in (1, 2, 4):
            keep = (sub + k <= 7) if reverse else (sub >= k)
            shift = TOK - k if reverse else k
            a_prev = jnp.where(keep, pltpu.roll(a, shift, 0), 1.0)
            b_prev = jnp.where(keep, pltpu.roll(b, shift, 0), 0.0)
            b = b + a * b_prev
            a = a * a_prev
        return a, b

    def gates_block(blk, _):
        t0 = pl.multiple_of(blk * TOK, TOK)
        rows = pl.ds(t0, TOK)
        first = jnp.logical_or(blk == 0, blk == nb_lat)
        last = jnp.logical_or(blk == nb_lat - 1, blk == nb - 1)
        prev = ux_ref[pl.ds(pl.multiple_of(jnp.maximum(t0 - HALO, 0), HALO), HALO), :].astype(F32)
        nxt = ux_ref[pl.ds(pl.multiple_of(jnp.minimum(t0 + TOK, t - HALO), HALO), HALO), :].astype(F32)
        xe = jnp.concatenate([jnp.where(first, 0.0, prev), ux_ref[rows, :].astype(F32),
                              jnp.where(last, 0.0, nxt)], axis=0)
        xs = cb + xe[HALO - 2:HALO - 2 + TOK] * cw[0:1]
        for j in range(1, 4):
            xs = xs + xe[HALO - 2 + j:HALO - 2 + j + TOK] * cw[j:j + 1]
        g = _dg(xs.astype(BF16), w_ref[0]) + bias
        for d, (a_ref, h_ref) in enumerate(dirs):
            r = _sigmoid(g[:, 2 * d * HEAD:(2 * d + 1) * HEAD])
            gi = _sigmoid(g[:, (2 * d + 1) * HEAD:(2 * d + 2) * HEAD])
            a = jnp.exp(-LRU_C * r * sp[d])
            a, h = tile_scan(a, jnp.sqrt(1.0 - a * a) * (gi * xs), reverse=(d == 1))
            a_ref[rows, :] = a
            h_ref[rows, :] = h
        return 0

    lax.fori_loop(0, nb, gates_block, 0)

    def carry_tiles(tile0, n_tiles, state):
        def step(j, st):
            rf = pl.ds(pl.multiple_of((tile0 + j) * 8, 8), 8)
            rb = pl.ds(pl.multiple_of((tile0 + n_tiles - 1 - j) * 8, 8), 8)
            tf = hf_ref[rf, :] + af_ref[rf, :] * st[0]
            tb = hb_ref[rb, :] + ab_ref[rb, :] * st[1]
            hf_ref[rf, :] = tf
            hb_ref[rb, :] = tb
            return tf[7:8], tb[0:1]

        return lax.fori_loop(0, n_tiles, step, state, unroll=8)

    zero = jnp.zeros((1, HEAD), F32)
    carry_tiles(0, lat // 8, carry_tiles(lat // 8, ctx // 8, (zero, zero)))

    def out_block(blk, _):
        rows = pl.ds(pl.multiple_of(blk * TOK, TOK), TOK)
        o_ref[rows, :] = ((hf_ref[rows, :] + hb_ref[rows, :]) * _silu(z_ref[rows, :].astype(F32))).astype(BF16)
        return 0

    lax.fori_loop(0, nb, out_block, 0)


def _lru(p, conv_w, conv_b, w_gates, b_gates, lam, lat):
    t = p.shape[0]
    col = lambda c0: pl.BlockSpec((t, HEAD), lambda n: (0, c0 // HEAD + n))
    vec = lambda r: pl.BlockSpec((r, HEAD), lambda n: (0, n))
    return pl.pallas_call(
        functools.partial(_lru_kernel, lat=lat),
        grid=(LRU_BLOCKS,),
        in_specs=[col(P_BX), col(P_BZ), vec(4), vec(1),
                  pl.BlockSpec((1, HEAD, 4 * HEAD), lambda n: (n, 0, 0)),
                  pl.BlockSpec((1, 1, 4 * HEAD), lambda n: (n, 0, 0)), vec(2)],
        out_specs=pl.BlockSpec((t, HEAD), lambda n: (0, n)),
        out_shape=jax.ShapeDtypeStruct((t, LRU_WIDTH), BF16),
        scratch_shapes=[pltpu.VMEM((t, HEAD), F32)] * 4,
        compiler_params=_params("parallel"),
        name="lru",
    )(p, p, conv_w, conv_b.reshape(1, -1), w_gates, b_gates, lam)


def _dn_prep_kernel(q_ref, k_ref, v_ref, qp_ref, kp_ref, vp_ref, qx_ref, kx_ref, vx_ref, cw_ref, sm_ref,
                    alog_ref, dtb_ref, u_ref, w_ref, qd_ref, kd_ref, qk_ref, gl_ref, *, nb_lat):
    i = pl.program_id(0)
    tb = q_ref.shape[0]
    cc = DN_CHUNK
    first = jnp.logical_or(i == 0, i == nb_lat)
    last = jnp.logical_or(i == nb_lat - 1, i == nb_lat)
    cw = cw_ref[...]

    def conv_silu(x_ref, prev_ref, next_ref, c0):
        xe = jnp.concatenate([jnp.where(first, 0.0, prev_ref[...].astype(F32)), x_ref[...].astype(F32),
                              jnp.where(last, 0.0, next_ref[...].astype(F32))], axis=0)
        y = xe[HALO - 2:HALO - 2 + tb] * cw[0:1, c0:c0 + MIX]
        for j in range(1, 4):
            y = y + xe[HALO - 2 + j:HALO - 2 + j + tb] * cw[j:j + 1, c0:c0 + MIX]
        return _silu(y)

    q = conv_silu(q_ref, qp_ref, qx_ref, 0)
    k = conv_silu(k_ref, kp_ref, kx_ref, MIX)
    v = conv_silu(v_ref, vp_ref, vx_ref, 2 * MIX)

    sm = sm_ref[...].astype(F32)
    beta_all = _sigmoid(sm)
    g_all = -jnp.exp(alog_ref[...]) * _softplus(sm + dtb_ref[...])

    r = lax.broadcasted_iota(jnp.int32, (tb, tb), 0)
    c = lax.broadcasted_iota(jnp.int32, (tb, tb), 1)
    same = (r // cc) == (c // cc)
    tri_f = jnp.where(same, jnp.where(c <= r, 1.0, 0.0), 0.0).astype(BF16)
    tri_b = jnp.where(same, jnp.where(c >= r, 1.0, 0.0), 0.0).astype(BF16)
    gcs = (_dot_exact_lhs(tri_f, g_all), _dot_exact_lhs(tri_b, g_all))

    ii = lax.broadcasted_iota(jnp.int32, (cc, cc), 0)
    jj = lax.broadcasted_iota(jnp.int32, (cc, cc), 1)
    eye = jnp.where(ii == jj, 1.0, 0.0)
    blk16 = (ii // 16) == (jj // 16)
    blk32 = (ii // 32) == (jj // 32)
    off32 = jnp.logical_and(blk32, jnp.logical_not(blk16))
    off64 = jnp.logical_not(blk32)
    b16 = lambda m: m.astype(BF16)
    strict = (ii > jj, ii < jj)

    gl_ref[...] = jnp.zeros_like(gl_ref)
    qs, ks = [], []
    for h in range(DN_HEADS):
        hs = slice(h * HEAD, (h + 1) * HEAD)
        qh, kh = q[:, hs], k[:, hs]
        qs.append(qh * lax.rsqrt(jnp.sum(qh * qh, axis=-1, keepdims=True) + NORM_EPS) * (HEAD ** -0.5))
        ks.append(kh * lax.rsqrt(jnp.sum(kh * kh, axis=-1, keepdims=True) + NORM_EPS))

    def chunk_problems(ch):
        rows = slice(ch * cc, (ch + 1) * cc)
        gc = [gcs[d][rows] for d in range(2)]
        gct = [g.T for g in gc]
        kk, qk0 = [], []
        for h in range(DN_HEADS):
            k16 = b16(ks[h][rows])
            kk.append(_dg(k16, k16, NT))
            qk0.append(_dg(b16(qs[h][rows]), k16, NT))
        probs = []
        for d in range(2):
            end = cc - 1 if d == 0 else 0
            for h in range(DN_HEADS):
                hs = slice(h * HEAD, (h + 1) * HEAD)
                lb = SM_AB + d * 2 * DN_HEADS + h
                lg = lb + DN_HEADS
                beta = beta_all[rows, lb:lb + 1]
                gcol, grow = gc[d][:, lg:lg + 1], gct[d][lg:lg + 1, :]
                glast = gc[d][end:end + 1, lg:lg + 1]
                qh, kh, vh = qs[h][rows], ks[h][rows], v[rows, hs]
                decay = jnp.exp(jnp.where(strict[d], gcol - grow, NEG_BIG))
                eg = jnp.exp(gcol)
                kb = kh * beta
                lm = (beta * kk[h]) * decay
                qk_ref[d, h, rows, :] = (qk0[h] * (decay + eye)).astype(BF16)
                qd_ref[d, rows, hs] = (qh * eg).astype(BF16)
                kd_ref[d, rows, hs] = (kh * jnp.exp(glast - gcol)).astype(BF16)
                gl_ref[d, ch, h:h + 1, :] = jnp.broadcast_to(jnp.exp(glast), (1, HEAD))
                probs.append((d, hs, lm, jnp.concatenate([vh * beta, kb * eg], axis=1), rows))
        return probs

    for ch0 in range(0, tb // cc, DN_GROUP):
        probs = [pr for ch in range(ch0, ch0 + DN_GROUP) for pr in chunk_problems(ch)]
        lms = [pr[2] for pr in probs]
        diag = [jnp.where(blk16, lm, 0.0) for lm in lms]
        xs = [eye - m for m in diag]
        pw = [b16(m) for m in diag]
        pw = [_dg(m, m) for m in pw]
        for _ in range(2):
            pw = [b16(m) for m in pw]
            xs = [x + _dg(b16(x), m) for x, m in zip(xs, pw)]
            pw = [_dg(m, m) for m in pw]
        xs = [x + _dg(b16(x), b16(m)) for x, m in zip(xs, pw)]
        for off in (off32, off64):
            x16 = [b16(x) for x in xs]
            cx = [_dg(b16(jnp.where(off, lm, 0.0)), x) for lm, x in zip(lms, x16)]
            xs = [x - _dg(xb, b16(c)) for x, xb, c in zip(xs, x16, cx)]
        sols = [_dg(b16(x), b16(pr[3])) for x, pr in zip(xs, probs)]
        for sol, (d, hs, _, _, rows) in zip(sols, probs):
            u_ref[d, rows, hs] = sol[:, :HEAD]
            w_ref[d, rows, hs] = sol[:, HEAD:].astype(BF16)


def _dn_prep(p, conv_w, alog_row, dtb_row, lat):
    t = p.shape[0]
    tb = TOK
    nb, nb_lat = t // tb, lat // tb
    rh = tb // HALO
    blk = lambda c0: pl.BlockSpec((tb, MIX), lambda i: (i, c0 // MIX))
    prev = lambda c0: pl.BlockSpec((HALO, MIX), lambda i: (jnp.maximum(i * rh - 1, 0), c0 // MIX))
    nxt = lambda c0: pl.BlockSpec((HALO, MIX), lambda i: (jnp.minimum((i + 1) * rh, t // HALO - 1), c0 // MIX))
    cols = (P_CQ, P_CK, P_CV)
    big = pl.BlockSpec((2, tb, MIX), lambda i: (0, i, 0))
    big_shape = lambda dt: jax.ShapeDtypeStruct((2, t, MIX), dt)
    return pl.pallas_call(
        functools.partial(_dn_prep_kernel, nb_lat=nb_lat),
        grid=(nb,),
        in_specs=[blk(c0) for c0 in cols] + [prev(c0) for c0 in cols] + [nxt(c0) for c0 in cols] + [
            pl.BlockSpec((4, 3 * MIX), lambda i: (0, 0)),
            pl.BlockSpec((tb, HEAD), lambda i: (i, P_SM // HEAD)),
            pl.BlockSpec((1, HEAD), lambda i: (0, 0)),
            pl.BlockSpec((1, HEAD), lambda i: (0, 0))],
        out_specs=[big, big, big, big,
                   pl.BlockSpec((2, DN_HEADS, tb, DN_CHUNK), lambda i: (0, 0, i, 0)),
                   pl.BlockSpec((2, tb // DN_CHUNK, 8, HEAD), lambda i: (0, i, 0, 0))],
        out_shape=[big_shape(F32), big_shape(BF16), big_shape(BF16), big_shape(BF16),
                   jax.ShapeDtypeStruct((2, DN_HEADS, t, DN_CHUNK), BF16),
                   jax.ShapeDtypeStruct((2, t // DN_CHUNK, 8, HEAD), F32)],
        compiler_params=_params("parallel"),
        name="dn_prep",
    )(*([p] * 9), conv_w, p, alog_row, dtb_row)


def _dn_scan_kernel(uf, wf, qdf, kdf, qkf, glf, ub, wb, qdb, kdb, qkb, glb, of_ref, ob_ref, s_ref):
    @pl.when(pl.program_id(0) == 0)
    def _():
        s_ref[...] = jnp.zeros_like(s_ref)

    cc = DN_CHUNK
    n_chunks = uf.shape[1] // cc
    dirs = ((uf, wf, qdf, kdf, qkf, glf, of_ref), (ub, wb, qdb, kdb, qkb, glb, ob_ref))
    probs = [(d, h, slice(h * HEAD, (h + 1) * HEAD)) for d in range(2) for h in range(DN_HEADS)]
    for step in range(n_chunks):
        ch = (step, n_chunks - 1 - step)
        rows = tuple(slice(c * cc, (c + 1) * cc) for c in ch)
        ws = [_dg(jnp.concatenate([dirs[d][1][0, rows[d], hs], dirs[d][2][0, rows[d], hs]], axis=0),
                  s_ref[d, h].astype(BF16)) for d, h, hs in probs]
        v_new = [(dirs[d][0][0, rows[d], hs] - t[:cc]).astype(BF16) for t, (d, h, hs) in zip(ws, probs)]
        outs = [t[cc:] + _dg(dirs[d][4][0, h, rows[d], :], vn) for t, vn, (d, h, hs) in zip(ws, v_new, probs)]
        upd = [_dg(dirs[d][3][0, rows[d], hs], vn, TN) for vn, (d, h, hs) in zip(v_new, probs)]
        for o, ds, (d, h, hs) in zip(outs, upd, probs):
            dirs[d][6][rows[d], hs] = o
            s_ref[d, h] = s_ref[d, h] * dirs[d][5][0, ch[d], h:h + 1, :] + ds


def _dn_scan(u, w, qd, kd, qk, gl, lat):
    t = u.shape[1]
    tb = TOK
    cpb = tb // DN_CHUNK
    n, n_lat = t // tb, lat // tb
    n_ctx = n - n_lat
    bf = lambda i: jnp.where(i < n_ctx, n_lat + i, i - n_ctx)
    bb = lambda i: n - 1 - i
    specs = []
    for d, blk in ((0, bf), (1, bb)):
        big = pl.BlockSpec((1, tb, MIX), lambda i, d=d, blk=blk: (d, blk(i), 0))
        specs += [big, big, big, big,
                  pl.BlockSpec((1, DN_HEADS, tb, DN_CHUNK), lambda i, d=d, blk=blk: (d, 0, blk(i), 0)),
                  pl.BlockSpec((1, cpb, 8, HEAD), lambda i, d=d, blk=blk: (d, blk(i), 0, 0))]
    return pl.pallas_call(
        _dn_scan_kernel,
        grid=(n,),
        in_specs=specs,
        out_specs=[pl.BlockSpec((tb, MIX), lambda i: (bf(i), 0)), pl.BlockSpec((tb, MIX), lambda i: (bb(i), 0))],
        out_shape=[jax.ShapeDtypeStruct((t, MIX), F32)] * 2,
        scratch_shapes=[pltpu.VMEM((2, DN_HEADS, HEAD, HEAD), F32)],
        compiler_params=_params("arbitrary"),
        name="dn_scan",
    )(u, w, qd, kd, qk, gl, u, w, qd, kd, qk, gl)


def _dn_out_kernel(of_ref, ob_ref, z_ref, nw_ref, o_ref):
    for h in range(DN_HEADS):
        hs = slice(h * HEAD, (h + 1) * HEAD)
        o = of_ref[:, hs] + ob_ref[:, hs]
        o_ref[:, hs] = (_rms(o, nw_ref[...], HEAD) * _silu(z_ref[:, hs].astype(F32))).astype(BF16)


def _dn_out(o_f, o_b, p, norm_w):
    t = p.shape[0]
    tb = _row_tile(t)
    row = lambda c: pl.BlockSpec((tb, MIX), lambda i: (i, c))
    return pl.pallas_call(
        _dn_out_kernel,
        grid=(t // tb,),
        in_specs=[row(0), row(0), row(P_CZ // MIX), pl.BlockSpec((1, HEAD), lambda i: (0, 0))],
        out_specs=row(0),
        out_shape=jax.ShapeDtypeStruct((t, MIX), BF16),
        compiler_params=_params("parallel"),
        name="dn_out",
    )(o_f, o_b, p, norm_w.reshape(1, HEAD))


def _arrange_w_in(w_in):
    n = w_in.shape[2]
    w16 = jnp.pad(w_in, ((0, 0), (0, 0), (0, -n % HEAD))).astype(BF16)
    pad = jnp.zeros(w_in.shape[:2] + (P_W - n,), BF16)
    tail = [w16[..., 5328:5840],
            w16[..., 5008:5264],
            w16[..., 4624:5008],
            w16[..., 5264:5328],
            w16[..., 4608:4624],
            pad]
    return w16, jnp.concatenate(tail, axis=-1)


def _arrange_mla_wq(w_uq):
    l, r, _ = w_uq.shape
    w = w_uq.reshape(l, r, MLA_HEADS, MLA_QK)
    w = jnp.pad(w, ((0, 0), (0, 0), (0, 0), (0, 2 * HEAD - MLA_QK)))
    return w.reshape(l, r, MLA_HEADS * 2 * HEAD).astype(BF16)


def _pad_qk_norm(w):
    return jnp.pad(w, ((0, 0), (0, 2 * HEAD - MLA_QK)))[:, None, :]


def _small_lane_row(vals):
    l = vals.shape[0]
    row = jnp.zeros((l, HEAD), F32)
    for d in range(2):
        lo = SM_AB + d * 2 * DN_HEADS + DN_HEADS
        row = row.at[:, lo:lo + DN_HEADS].set(vals[:, d, :])
    return row[:, None, :]


def _arrange_lru_gates(w_a, b_a, w_x, b_x):
    w = jnp.concatenate([w_a[:, 0], w_x[:, 0], w_a[:, 1], w_x[:, 1]], axis=-1)
    l = b_a.shape[0]
    blk = lambda b, d: b[:, d].reshape(l, LRU_BLOCKS, 1, HEAD)
    b = jnp.concatenate([blk(b_a, 0), blk(b_x, 0), blk(b_a, 1), blk(b_x, 1)], axis=-1)
    return w.astype(BF16), b


def _rope_tables(lat, ctx):
    n_rows = lat // GRID_W
    pos = jnp.arange(max(n_rows, GRID_W), dtype=F32)
    lane = jnp.arange(HEAD)

    def table(half, width):
        inv_freq = ROPE_THETA ** (-jnp.arange(half, dtype=F32) / half)
        ang = pos[:, None] * inv_freq[lane % half][None, :]
        live = (lane < width)[None, :]
        by_row = ((lane // (2 * half)) == 0)[None, :]
        sign = jnp.where((lane % (2 * half)) < half, -1.0, 1.0)[None, :]

        def expand(f):
            per_row = jnp.repeat(f[:n_rows], GRID_W, axis=0)
            per_col = jnp.tile(f[:GRID_W], (n_rows, 1))
            return jnp.where(by_row, per_row, per_col)

        cos = jnp.where(live, expand(jnp.cos(ang)), 0.0)
        sin = jnp.where(live, expand(jnp.sin(ang)), 0.0) * sign
        cos_c = jnp.broadcast_to(jnp.where(live, 1.0, 0.0), (ctx, HEAD))
        return (jnp.concatenate([cos, cos_c], axis=0).astype(F32),
                jnp.concatenate([sin, jnp.zeros((ctx, HEAD), F32)], axis=0).astype(F32))

    return table(32, HEAD), table(16, MLA_ROPE)


def kernel(x, c, ctx, c_ctx, norm_w, w_ada, b_ada, w_in, w_out, attn_q_norm, attn_k_norm, lru_conv_w, lru_conv_b, lru_w_a, lru_b_a, lru_w_x, lru_b_x, lru_lambda, dn_conv_w, dn_a_log, dn_dt_bias, dn_norm_w, mla_q_norm, mla_kv_norm, mla_w_uq, mla_w_ukv, mla_q_qk_norm, mla_k_qk_norm):
    assert x.shape[0] == 1 and ctx.shape[1] == TOK and x.shape[1] % (8 * TOK) == 0
    lat, n_ctx = x.shape[1], ctx.shape[1]
    depth = w_in.shape[0]

    xs = jnp.concatenate([x[0], ctx[0]], axis=0)
    mod = _ada(jnp.stack([c[0], c_ctx], axis=1), w_ada, b_ada)
    (cos_a, sin_a), (cos_m, sin_m) = _rope_tables(lat, n_ctx)
    w_main, w_tail = _arrange_w_in(w_in)
    w_out_r = w_out.reshape(depth, 4, MIX, D_MODEL).astype(BF16)
    wq_r = _arrange_mla_wq(mla_w_uq)
    wkv_r = mla_w_ukv.astype(BF16)
    qn_r, kn_r = _pad_qk_norm(mla_q_qk_norm), _pad_qk_norm(mla_k_qk_norm)
    alog_r, dtb_r = _small_lane_row(dn_a_log), _small_lane_row(dn_dt_bias)
    lru_w, lru_b = _arrange_lru_gates(lru_w_a, lru_b_a, lru_w_x, lru_b_x)

    for l in range(depth):
        p = _inproj(xs, norm_w[l], mod[l, 0], mod[l, 1], w_main, w_tail, l, lat)
        qa, ka, vta = _gqa_prep(p, cos_a, sin_a, attn_q_norm[l], attn_k_norm[l])
        bound_a = jnp.max(jnp.abs(attn_q_norm[l])) * jnp.max(jnp.abs(attn_k_norm[l])) * (HEAD ** 0.5 * LOG2E)
        y_a = _flash(qa, ka, p, P_AV, vta, p, P_AZ, A_HEADS, A_KV_HEADS, HEAD, lat, bound_a, "gqa_attn")
        y_b = _lru(p, lru_conv_w[l], lru_conv_b[l], lru_w[l], lru_b[l], lru_lambda[l], lat)
        u, w, qd, kd, qk, gl = _dn_prep(p, dn_conv_w[l], alog_r[l], dtb_r[l], lat)
        o_f, o_b = _dn_scan(u, w, qd, kd, qk, gl, lat)
        y_c = _dn_out(o_f, o_b, p, dn_norm_w[l])
        qm, km, vm, vtm = _mla_prep(p, cos_m, sin_m, mla_q_norm[l], mla_kv_norm[l], wq_r[l], wkv_r[l],
                               qn_r[l], kn_r[l])
        bound_d = (jnp.max(jnp.abs(mla_q_qk_norm[l])) * jnp.max(jnp.abs(mla_k_qk_norm[l]))
                   * (MLA_QK ** 0.5 * LOG2E))
        y_d = _flash(qm, km, vm, 0, vtm, p, P_DZ, MLA_HEADS, MLA_HEADS, 2 * HEAD, lat, bound_d, "mla_attn")
        last = l == depth - 1
        xs = _outproj((y_a, y_b, y_c, y_d), w_out_r, l, xs, mod[l, 2], lat, lat if last else lat + n_ctx)
    return xs[None]
```

```python
import functools
import math

import jax
import jax.numpy as jnp
from jax import lax
from jax.experimental import pallas as pl
from jax.experimental.pallas import tpu as pltpu

F32 = jnp.float32
BF16 = jnp.bfloat16

D_MODEL = 2048
DEPTH = 4
GRID_W = 64
ROPE_THETA = 10000.0
NORM_EPS = 1e-6
HEAD = 128
A_HEADS, A_KV_HEADS = 4, 2
LRU_WIDTH, LRU_BLOCKS, LRU_C = 512, 4, 8.0
DN_HEADS, DN_CHUNK = 4, 64
DN_GROUP = 4
MLA_HEADS, MLA_Q_RANK, MLA_KV_RANK, MLA_NOPE, MLA_ROPE = 4, 384, 256, 128, 64
MLA_QK = MLA_NOPE + MLA_ROPE
MIX = 512

P_AQ, P_AK, P_AV, P_AZ = 0, 512, 768, 1024
P_BX, P_BZ = 1536, 2048
P_CQ, P_CK, P_CV, P_CZ = 2560, 3072, 3584, 4096
P_DZ, P_DCKV, P_DCQ = 4608, 5120, 5376
P_SM = 5760
P_W = 6144
P_MAIN = 4608
W_TILE = 768
IN_ROWS = 1056
OUT_TILE = 1024
ADA_TILE = 512
FLASH_TQ, FLASH_TK = 512, 4096
SM_AB = MLA_ROPE

TOK = 256
HALO = 16
VMEM_LIMIT = 56 * 1024 * 1024
LOG2E = math.log2(math.e)
NEG_BIG = -1e30
MAX_EXP2_LOGIT = 60.0

NN = ((1,), (0,))
NT = ((1,), (1,))
TN = ((0,), (0,))


def _dg(a, b, dims=NN):
    return lax.dot_general(a, b, (dims, ((), ())), preferred_element_type=F32)


def _dot_exact_lhs(m, b):
    b0 = b.astype(BF16)
    r1 = b - b0.astype(F32)
    b1 = r1.astype(BF16)
    b2 = (r1 - b1.astype(F32)).astype(BF16)
    return _dg(m, b0) + (_dg(m, b1) + _dg(m, b2))


def _sigmoid(x):
    return 1.0 / (1.0 + jnp.exp(-x))


def _silu(x):
    return x * _sigmoid(x)


def _softplus(x):
    return jnp.maximum(x, 0.0) + jnp.log(1.0 + jnp.exp(-jnp.abs(x)))


def _rms(x, w, n):
    return x * lax.rsqrt(jnp.sum(x * x, axis=-1, keepdims=True) * (1.0 / n) + NORM_EPS) * w


def _rope(x, cos, sin_signed, half):
    lane = lax.broadcasted_iota(jnp.int32, x.shape, 1)
    first = (lane % (2 * half)) < half
    rot = jnp.where(first, pltpu.roll(x, HEAD - half, 1), pltpu.roll(x, half, 1))
    return x * cos + rot * sin_signed


def _params(*sem):
    return pltpu.CompilerParams(dimension_semantics=sem, vmem_limit_bytes=VMEM_LIMIT)


def _row_tile(rows):
    for t in (768, 512, 256):
        if rows % t == 0:
            return t
    raise ValueError(f"row count {rows} is not a multiple of {TOK}")


def _ada_kernel(c_ref, w_ref, b_ref, o_ref):
    s = _silu(c_ref[...])
    w = w_ref[0]
    r0 = jnp.sum(s[:, 0:1] * w, axis=0, keepdims=True)
    r1 = jnp.sum(s[:, 1:2] * w, axis=0, keepdims=True)
    o_ref[0, 0] = jnp.concatenate([r0, r1], axis=0) + b_ref[0]


def _ada(c_cols, w_ada, b_ada):
    depth, d, _ = w_ada.shape
    tn = ADA_TILE
    per = d // tn
    return pl.pallas_call(
        _ada_kernel,
        grid=(depth, 3 * per),
        in_specs=[
            pl.BlockSpec((d, 2), lambda l, j: (0, 0)),
            pl.BlockSpec((1, d, tn), lambda l, j: (l, 0, j)),
            pl.BlockSpec((1, 1, tn), lambda l, j: (l, 0, j)),
        ],
        out_specs=pl.BlockSpec((1, 1, 2, tn), lambda l, j: (l, j // per, 0, j % per)),
        out_shape=jax.ShapeDtypeStruct((depth, 3, 2, d), F32),
        compiler_params=_params("parallel", "parallel"),
        name="ada",
    )(c_cols, w_ada, b_ada.reshape(depth, 1, 3 * d))


def _inproj_kernel(x_ref, nw_ref, shift_ref, scale_ref, wm_ref, wt_ref, o_ref, h_ref, *, lat, n_main):
    i, j = pl.program_id(0), pl.program_id(1)
    tm = x_ref.shape[0]
    n_col = pl.num_programs(1)
    slot = i % 2
    rc = 32
    gain = nw_ref[...] * (1.0 + scale_ref[...])
    shift = shift_ref[...]

    def normalise(tile, dst, inline=False):
        def chunk(ci, _):
            rows = pl.ds(pl.multiple_of(ci * rc, rc), rc)
            is_ctx = tile * tm + ci * rc >= lat
            g = jnp.where(is_ctx, gain[1:2, :], gain[0:1, :])
            b = jnp.where(is_ctx, shift[1:2, :], shift[0:1, :])
            x = x_ref[rows, :]
            r = lax.rsqrt(jnp.sum(x * x, axis=-1, keepdims=True) * (1.0 / D_MODEL) + NORM_EPS)
            h_ref[dst, rows, :] = ((x * r) * g + b).astype(BF16)
            return 0

        lax.fori_loop(0, tm // rc, chunk, 0, unroll=inline)

    @pl.when(jnp.logical_and(i == 0, j == 0))
    def _():
        normalise(0, 0)

    @pl.when(j < n_main)
    def _():
        o_ref[...] = _dg(h_ref[slot], wm_ref[...]).astype(o_ref.dtype)

    @pl.when(jnp.logical_and(j >= n_main, j < n_col - 1))
    def _():
        o_ref[...] = _dg(h_ref[slot], wt_ref[...]).astype(o_ref.dtype)

    @pl.when(j == n_col - 1)
    def _():
        o_ref[...] = _dg(h_ref[slot], wt_ref[...]).astype(o_ref.dtype)
        normalise(i + 1, 1 - slot, inline=True)


def _inproj(xs, norm_w, shift, scale, w_main, w_tail, layer, lat):
    t, d = xs.shape
    tm, tn = (IN_ROWS if t % IN_ROWS == 0 else _row_tile(t)), W_TILE
    n_main, n_col, n_row = P_MAIN // tn, P_W // tn, t // tm
    x_tile = lambda i, j: (jnp.where(j == n_col - 1, jnp.minimum(i + 1, n_row - 1), i), 0)
    return pl.pallas_call(
        functools.partial(_inproj_kernel, lat=lat, n_main=n_main),
        grid=(n_row, n_col),
        in_specs=[
            pl.BlockSpec((tm, d), x_tile),
            pl.BlockSpec((1, d), lambda i, j: (0, 0)),
            pl.BlockSpec((2, d), lambda i, j: (0, 0)),
            pl.BlockSpec((2, d), lambda i, j: (0, 0)),
            pl.BlockSpec((None, None, d, tn), lambda i, j: (layer, jnp.minimum(j, n_main - 1), 0, 0)),
            pl.BlockSpec((None, None, d, tn), lambda i, j: (layer, jnp.maximum(j - n_main, 0), 0, 0)),
        ],
        out_specs=pl.BlockSpec((tm, tn), lambda i, j: (i, j)),
        out_shape=jax.ShapeDtypeStruct((t, P_W), BF16),
        scratch_shapes=[pltpu.VMEM((2, tm, d), BF16)],
        compiler_params=_params("arbitrary", "arbitrary"),
        name="inproj",
    )(xs, norm_w.reshape(1, d), shift, scale, w_main, w_tail)


def _outproj_kernel(ya_ref, yb_ref, yc_ref, yd_ref, w_ref, x_ref, g_ref, o_ref, *, lat):
    i = pl.program_id(0)
    tm = x_ref.shape[0]
    acc = _dg(ya_ref[...], w_ref[0])
    acc += _dg(yb_ref[...], w_ref[1])
    acc += _dg(yc_ref[...], w_ref[2])
    acc += _dg(yd_ref[...], w_ref[3])
    row = i * tm + lax.broadcasted_iota(jnp.int32, (tm, 1), 0)
    gate = jnp.where(row >= lat, g_ref[1:2, :], g_ref[0:1, :])
    o_ref[...] = x_ref[...] + gate * acc


def _outproj(ys, w_out, layer, xs, gate, lat, out_rows):
    d = xs.shape[1]
    tm, tn = _row_tile(out_rows), OUT_TILE
    yspec = pl.BlockSpec((tm, MIX), lambda i, j: (i, 0))
    return pl.pallas_call(
        functools.partial(_outproj_kernel, lat=lat),
        grid=(out_rows // tm, d // tn),
        in_specs=[yspec, yspec, yspec, yspec,
                  pl.BlockSpec((None, 4, MIX, tn), lambda i, j: (layer, 0, 0, j)),
                  pl.BlockSpec((tm, tn), lambda i, j: (i, j)),
                  pl.BlockSpec((2, tn), lambda i, j: (0, j))],
        out_specs=pl.BlockSpec((tm, tn), lambda i, j: (i, j)),
        out_shape=jax.ShapeDtypeStruct((out_rows, d), F32),
        compiler_params=_params("parallel", "parallel"),
        name="outproj",
    )(*ys, w_out, xs, gate)


def _flash_kernel(q_ref, k_ref, v_ref, z_ref, o_ref, *, n_loop, tk, tail, dq, shared_kv, bounded, out_row0=0):
    tq = q_ref.shape[0]
    nh = q_ref.shape[1] // dq
    if out_row0:
        o_ref[0:out_row0, :] = jnp.zeros((out_row0, o_ref.shape[1]), o_ref.dtype)

    def scores(c, rows):
        kc = 0 if shared_kv else c
        s = _dg(q_ref[:, c * dq:(c + 1) * dq], k_ref[rows, kc * dq:(kc + 1) * dq], NT)
        return s, v_ref[rows, kc * HEAD:(kc + 1) * HEAD]

    def attend_bounded(c, carry, rows):
        l, acc = carry
        s, v = scores(c, rows)
        p = jnp.exp2(s)
        for j in range(s.shape[1] // HEAD):
            l = l + p[:, j * HEAD:(j + 1) * HEAD]
        return l, acc + _dg(p.astype(BF16), v)

    def attend_online(c, carry, rows):
        m, l, acc = carry
        s, v = scores(c, rows)
        m_new = jnp.maximum(m, jnp.max(s, axis=-1, keepdims=True))
        alpha = jnp.exp2(m - m_new)
        p = jnp.exp2(s - m_new)
        l = alpha * l + jnp.sum(p, axis=-1, keepdims=True)
        return m_new, l, alpha * acc + _dg(p.astype(BF16), v)

    zeros = jnp.zeros((tq, HEAD), F32)
    if bounded:
        attend, init = attend_bounded, (zeros, zeros)
    else:
        attend, init = attend_online, (jnp.full((tq, 1), -1e30, F32), jnp.zeros((tq, 1), F32), zeros)

    def step(kb, carries):
        rows = pl.ds(pl.multiple_of(kb * tk, tk), tk)
        return tuple(attend(c, carries[c], rows) for c in range(nh))

    carries = (init,) * nh
    if n_loop:
        carries = lax.fori_loop(0, n_loop, step, carries, unroll=2)
    for c in range(nh):
        carry = attend(c, carries[c], pl.ds(tail[0], tail[1]))
        l, acc = carry[-2], carry[-1]
        cs = slice(c * HEAD, (c + 1) * HEAD)
        gate = _silu(z_ref[:, cs].astype(F32))
        o_ref[out_row0:out_row0 + tq, cs] = (acc / jnp.sum(l, axis=-1, keepdims=True) * gate).astype(o_ref.dtype)


def _flash_lat_kernel(q_ref, k_ref, v_ref, z_ref, y_ref, o_ref, **kw):
    _flash_kernel(q_ref, k_ref, v_ref, z_ref, o_ref, **kw)


def _flash_t_kernel(q_ref, k_ref, vt_ref, z_ref, y_ref, o_ref, *, n_loop, tk, n_tail, dq, shared_kv):
    tq = q_ref.shape[0]
    nh = q_ref.shape[1] // dq
    sub = tk // TOK

    def attend(c, carry, row0, tile0, n_tiles):
        l, acc = carry
        kc = 0 if shared_kv else c
        keys = n_tiles * TOK
        s = _dg(k_ref[pl.ds(row0, keys), kc * dq:(kc + 1) * dq], q_ref[:, c * dq:(c + 1) * dq], NT)
        p = jnp.exp2(s)
        for r in range(keys // 8):
            l = l + p[r * 8:(r + 1) * 8, :]
        p = p.astype(BF16)
        for j in range(n_tiles):
            acc = acc + _dg(vt_ref[tile0 + j, kc * HEAD:(kc + 1) * HEAD, :], p[j * TOK:(j + 1) * TOK, :])
        return l, acc

    init = (jnp.zeros((8, tq), F32), jnp.zeros((HEAD, tq), F32))

    def step(kb, carries):
        return tuple(attend(c, carries[c], pl.multiple_of(kb * tk, tk), kb * sub, sub) for c in range(nh))

    carries = lax.fori_loop(0, n_loop, step, (init,) * nh, unroll=2)
    for c in range(nh):
        l, acc = attend(c, carries[c], n_loop * tk, n_loop * sub, n_tail)
        cs = slice(c * HEAD, (c + 1) * HEAD)
        out = (acc / jnp.sum(l, axis=0, keepdims=True)).T
        o_ref[:, cs] = (out * _silu(z_ref[:, cs].astype(F32))).astype(o_ref.dtype)


def _flash(q, k, v, v_col, vt, p, z_col, heads, kv_heads, dq, lat, logit_bound, name):
    t = q.shape[0]
    ctx = t - lat
    nh = 2
    shared_kv = heads // kv_heads == nh
    nk = 1 if shared_kv else nh
    tq = FLASH_TQ if lat % FLASH_TQ == 0 else TOK
    tk = FLASH_TK if lat % (2 * FLASH_TK) == 0 else TOK
    zb = z_col // (nh * HEAD)
    vb = v_col // (nk * HEAD)
    cb = lat // ctx
    out_shape = jax.ShapeDtypeStruct((t, heads * HEAD), BF16)

    def call(bounded):
        common = dict(tk=tk, dq=dq, shared_kv=shared_kv, bounded=bounded)
        suffix = "_bounded" if bounded else "_online"
        y = pl.pallas_call(
            functools.partial(_flash_kernel, n_loop=0, tail=(0, ctx), out_row0=lat, **common),
            grid=(heads // nh,),
            in_specs=[
                pl.BlockSpec((ctx, nh * dq), lambda g: (cb, g)),
                pl.BlockSpec((ctx, nk * dq), lambda g: (cb, g)),
                pl.BlockSpec((ctx, nk * HEAD), lambda g: (cb, vb + g)),
                pl.BlockSpec((ctx, nh * HEAD), lambda g: (cb, zb + g)),
            ],
            out_specs=pl.BlockSpec((t, nh * HEAD), lambda g: (0, g)),
            out_shape=out_shape,
            compiler_params=_params("parallel"),
            name=name + "_ctx" + suffix,
        )(q, k, v, p)
        if bounded:
            body = functools.partial(_flash_t_kernel, n_loop=lat // tk, tk=tk, n_tail=ctx // TOK, dq=dq,
                                     shared_kv=shared_kv)
            v_spec, v_arg = pl.BlockSpec((t // TOK, nk * HEAD, TOK), lambda g, i: (0, g, 0)), vt
        else:
            body = functools.partial(_flash_lat_kernel, n_loop=lat // tk, tail=(lat, ctx), **common)
            v_spec, v_arg = pl.BlockSpec((t, nk * HEAD), lambda g, i: (0, vb + g)), v
        return pl.pallas_call(
            body,
            grid=(heads // nh, lat // tq),
            in_specs=[
                pl.BlockSpec((tq, nh * dq), lambda g, i: (i, g)),
                pl.BlockSpec((t, nk * dq), lambda g, i: (0, g)),
                v_spec,
                pl.BlockSpec((tq, nh * HEAD), lambda g, i: (i, zb + g)),
                pl.BlockSpec(memory_space=pl.ANY),
            ],
            out_specs=pl.BlockSpec((tq, nh * HEAD), lambda g, i: (i, g)),
            out_shape=out_shape,
            input_output_aliases={4: 0},
            compiler_params=_params("parallel", "parallel"),
            name=name + suffix,
        )(q, k, v_arg, p, y)

    return lax.cond(logit_bound <= MAX_EXP2_LOGIT, lambda: call(True), lambda: call(False))


def _gqa_prep_kernel(q_ref, k_ref, v_ref, cos_ref, sin_ref, qn_ref, kn_ref, oq_ref, ok_ref, ovt_ref):
    cos, sin = cos_ref[...], sin_ref[...]
    scale = HEAD ** -0.5 * LOG2E
    for h in range(A_HEADS):
        c = slice(h * HEAD, (h + 1) * HEAD)
        oq_ref[:, c] = (_rope(_rms(q_ref[:, c].astype(F32), qn_ref[...], HEAD), cos, sin, 32) * scale).astype(BF16)
    for g in range(A_KV_HEADS):
        c = slice(g * HEAD, (g + 1) * HEAD)
        ok_ref[:, c] = _rope(_rms(k_ref[:, c].astype(F32), kn_ref[...], HEAD), cos, sin, 32).astype(BF16)
    for j in range(v_ref.shape[0] // TOK):
        ovt_ref[j] = v_ref[j * TOK:(j + 1) * TOK, :].astype(F32).T.astype(BF16)


def _gqa_prep(p, cos, sin, qn, kn):
    t = p.shape[0]
    tb = _row_tile(t)
    qw, kw = A_HEADS * HEAD, A_KV_HEADS * HEAD
    row = lambda w, c: pl.BlockSpec((tb, w), lambda i: (i, c))
    vec = pl.BlockSpec((1, HEAD), lambda i: (0, 0))
    return pl.pallas_call(
        _gqa_prep_kernel,
        grid=(t // tb,),
        in_specs=[row(qw, P_AQ // qw), row(kw, P_AK // kw), row(kw, P_AV // kw), row(HEAD, 0), row(HEAD, 0), vec, vec],
        out_specs=[row(qw, 0), row(kw, 0), pl.BlockSpec((tb // TOK, kw, TOK), lambda i: (i, 0, 0))],
        out_shape=[jax.ShapeDtypeStruct((t, qw), BF16), jax.ShapeDtypeStruct((t, kw), BF16),
                   jax.ShapeDtypeStruct((t // TOK, kw, TOK), BF16)],
        compiler_params=_params("parallel"),
        name="gqa_prep",
    )(p, p, p, cos, sin, qn.reshape(1, HEAD), kn.reshape(1, HEAD))


def _mla_prep_kernel(cq_ref, ckv_ref, sm_ref, cos_ref, sin_ref, qnw_ref, kvnw_ref, wq_ref, wkv_ref,
                     qn_ref, kn_ref, oq_ref, ok_ref, ov_ref, ovt_ref):
    cos, sin = cos_ref[...], sin_ref[...]
    scale = MLA_QK ** -0.5 * LOG2E
    qn, kn = qn_ref[...], kn_ref[...]
    q = _dg(_rms(cq_ref[...].astype(F32), qnw_ref[...], MLA_Q_RANK).astype(BF16), wq_ref[...])
    kv = _dg(_rms(ckv_ref[...].astype(F32), kvnw_ref[...], MLA_KV_RANK).astype(BF16), wkv_ref[...])
    lane = lax.broadcasted_iota(jnp.int32, sm_ref.shape, 1)
    kr = jnp.where(lane < MLA_ROPE, sm_ref[...].astype(F32), 0.0)
    kr_ss = jnp.sum(kr * kr, axis=-1, keepdims=True)
    for h in range(MLA_HEADS):
        lo = slice(2 * h * HEAD, (2 * h + 1) * HEAD)
        hi = slice((2 * h + 1) * HEAD, (2 * h + 2) * HEAD)
        q_nope, q_rope = q[:, lo], q[:, hi]
        r = lax.rsqrt((jnp.sum(q_nope * q_nope, axis=-1, keepdims=True)
                       + jnp.sum(q_rope * q_rope, axis=-1, keepdims=True)) * (1.0 / MLA_QK) + NORM_EPS)
        oq_ref[:, lo] = (q_nope * r * qn[:, :HEAD] * scale).astype(BF16)
        oq_ref[:, hi] = (_rope(q_rope * r * qn[:, HEAD:], cos, sin, 16) * scale).astype(BF16)
        k_nope, v = kv[:, lo], kv[:, hi]
        r = lax.rsqrt((jnp.sum(k_nope * k_nope, axis=-1, keepdims=True) + kr_ss) * (1.0 / MLA_QK) + NORM_EPS)
        ok_ref[:, lo] = (k_nope * r * kn[:, :HEAD]).astype(BF16)
        ok_ref[:, hi] = _rope(kr * r * kn[:, HEAD:], cos, sin, 16).astype(BF16)
        ov_ref[:, h * HEAD:(h + 1) * HEAD] = v.astype(BF16)
        for j in range(v.shape[0] // TOK):
            ovt_ref[j, h * HEAD:(h + 1) * HEAD, :] = v[j * TOK:(j + 1) * TOK, :].T.astype(BF16)


def _mla_prep(p, cos, sin, q_norm_w, kv_norm_w, wq, wkv, qn, kn):
    t = p.shape[0]
    tb = _row_tile(t)
    w2 = MLA_HEADS * 2 * HEAD
    row = lambda w, c: pl.BlockSpec((tb, w), lambda i: (i, c))
    full = lambda a: pl.BlockSpec(a.shape, lambda i: (0, 0))
    args = (q_norm_w.reshape(1, -1), kv_norm_w.reshape(1, -1), wq, wkv, qn, kn)
    return pl.pallas_call(
        _mla_prep_kernel,
        grid=(t // tb,),
        in_specs=[row(MLA_Q_RANK, P_DCQ // MLA_Q_RANK), row(MLA_KV_RANK, P_DCKV // MLA_KV_RANK),
                  row(HEAD, P_SM // HEAD), row(HEAD, 0), row(HEAD, 0)] + [full(a) for a in args],
        out_specs=[row(w2, 0), row(w2, 0), row(MLA_HEADS * HEAD, 0),
                   pl.BlockSpec((tb // TOK, MLA_HEADS * HEAD, TOK), lambda i: (i, 0, 0))],
        out_shape=[jax.ShapeDtypeStruct((t, w2), BF16), jax.ShapeDtypeStruct((t, w2), BF16),
                   jax.ShapeDtypeStruct((t, MLA_HEADS * HEAD), BF16),
                   jax.ShapeDtypeStruct((t // TOK, MLA_HEADS * HEAD, TOK), BF16)],
        compiler_params=_params("parallel"),
        name="mla_prep",
    )(p, p, p, cos, sin, *args)


def _lru_kernel(ux_ref, z_ref, cw_ref, cb_ref, w_ref, b_ref, lam_ref, o_ref, af_ref, ab_ref, hf_ref, hb_ref, *, lat):
    t = ux_ref.shape[0]
    ctx = t - lat
    nb = t // TOK
    nb_lat = lat // TOK
    cw, cb = cw_ref[...], cb_ref[...]
    bias = b_ref[0]
    sp = (_softplus(-lam_ref[0:1, :]), _softplus(-lam_ref[1:2, :]))
    dirs = ((af_ref, hf_ref), (ab_ref, hb_ref))
    sub = lax.broadcasted_iota(jnp.int32, (TOK, HEAD), 0) % 8

    def tile_scan(a, b, reverse):
        for k in (1, 2, 4):
            keep = (sub + k <= 7) if reverse else (sub >= k)
            shift = TOK - k if reverse else k
            a_prev = jnp.where(keep, pltpu.roll(a, shift, 0), 1.0)
            b_prev = jnp.where(keep, pltpu.roll(b, shift, 0), 0.0)
            b = b + a * b_prev
            a = a * a_prev
        return a, b

    def gates_block(blk, _):
        t0 = pl.multiple_of(blk * TOK, TOK)
        rows = pl.ds(t0, TOK)
        first = jnp.logical_or(blk == 0, blk == nb_lat)
        last = jnp.logical_or(blk == nb_lat - 1, blk == nb - 1)
        prev = ux_ref[pl.ds(pl.multiple_of(jnp.maximum(t0 - HALO, 0), HALO), HALO), :].astype(F32)
        nxt = ux_ref[pl.ds(pl.multiple_of(jnp.minimum(t0 + TOK, t - HALO), HALO), HALO), :].astype(F32)
        xe = jnp.concatenate([jnp.where(first, 0.0, prev), ux_ref[rows, :].astype(F32),
                              jnp.where(last, 0.0, nxt)], axis=0)
        xs = cb + xe[HALO - 2:HALO - 2 + TOK] * cw[0:1]
        for j in range(1, 4):
            xs = xs + xe[HALO - 2 + j:HALO - 2 + j + TOK] * cw[j:j + 1]
        g = _dg(xs.astype(BF16), w_ref[0]) + bias
        for d, (a_ref, h_ref) in enumerate(dirs):
            r = _sigmoid(g[:, 2 * d * HEAD:(2 * d + 1) * HEAD])
            gi = _sigmoid(g[:, (2 * d + 1) * HEAD:(2 * d + 2) * HEAD])
            a = jnp.exp(-LRU_C * r * sp[d])
            a, h = tile_scan(a, jnp.sqrt(1.0 - a * a) * (gi * xs), reverse=(d == 1))
            a_ref[rows, :] = a
            h_ref[rows, :] = h
        return 0

    lax.fori_loop(0, nb, gates_block, 0)

    def carry_tiles(tile0, n_tiles, state):
        def step(j, st):
            rf = pl.ds(pl.multiple_of((tile0 + j) * 8, 8), 8)
            rb = pl.ds(pl.multiple_of((tile0 + n_tiles - 1 - j) * 8, 8), 8)
            tf = hf_ref[rf, :] + af_ref[rf, :] * st[0]
            tb = hb_ref[rb, :] + ab_ref[rb, :] * st[1]
            hf_ref[rf, :] = tf
            hb_ref[rb, :] = tb
            return tf[7:8], tb[0:1]

        return lax.fori_loop(0, n_tiles, step, state, unroll=8)

    zero = jnp.zeros((1, HEAD), F32)
    carry_tiles(0, lat // 8, carry_tiles(lat // 8, ctx // 8, (zero, zero)))

    def out_block(blk, _):
        rows = pl.ds(pl.multiple_of(blk * TOK, TOK), TOK)
        o_ref[rows, :] = ((hf_ref[rows, :] + hb_ref[rows, :]) * _silu(z_ref[rows, :].astype(F32))).astype(BF16)
        return 0

    lax.fori_loop(0, nb, out_block, 0)


def _lru(p, conv_w, conv_b, w_gates, b_gates, lam, lat):
    t = p.shape[0]
    col = lambda c0: pl.BlockSpec((t, HEAD), lambda n: (0, c0 // HEAD + n))
    vec = lambda r: pl.BlockSpec((r, HEAD), lambda n: (0, n))
    return pl.pallas_call(
        functools.partial(_lru_kernel, lat=lat),
        grid=(LRU_BLOCKS,),
        in_specs=[col(P_BX), col(P_BZ), vec(4), vec(1),
                  pl.BlockSpec((1, HEAD, 4 * HEAD), lambda n: (n, 0, 0)),
                  pl.BlockSpec((1, 1, 4 * HEAD), lambda n: (n, 0, 0)), vec(2)],
        out_specs=pl.BlockSpec((t, HEAD), lambda n: (0, n)),
        out_shape=jax.ShapeDtypeStruct((t, LRU_WIDTH), BF16),
        scratch_shapes=[pltpu.VMEM((t, HEAD), F32)] * 4,
        compiler_params=_params("parallel"),
        name="lru",
    )(p, p, conv_w, conv_b.reshape(1, -1), w_gates, b_gates, lam)


def _dn_prep_kernel(q_ref, k_ref, v_ref, qp_ref, kp_ref, vp_ref, qx_ref, kx_ref, vx_ref, cw_ref, sm_ref,
                    alog_ref, dtb_ref, u_ref, w_ref, qd_ref, kd_ref, qk_ref, gl_ref, *, nb_lat):
    i = pl.program_id(0)
    tb = q_ref.shape[0]
    cc = DN_CHUNK
    first = jnp.logical_or(i == 0, i == nb_lat)
    last = jnp.logical_or(i == nb_lat - 1, i == nb_lat)
    cw = cw_ref[...]

    def conv_silu(x_ref, prev_ref, next_ref, c0):
        xe = jnp.concatenate([jnp.where(first, 0.0, prev_ref[...].astype(F32)), x_ref[...].astype(F32),
                              jnp.where(last, 0.0, next_ref[...].astype(F32))], axis=0)
        y = xe[HALO - 2:HALO - 2 + tb] * cw[0:1, c0:c0 + MIX]
        for j in range(1, 4):
            y = y + xe[HALO - 2 + j:HALO - 2 + j + tb] * cw[j:j + 1, c0:c0 + MIX]
        return _silu(y)

    q = conv_silu(q_ref, qp_ref, qx_ref, 0)
    k = conv_silu(k_ref, kp_ref, kx_ref, MIX)
    v = conv_silu(v_ref, vp_ref, vx_ref, 2 * MIX)

    sm = sm_ref[...].astype(F32)
    beta_all = _sigmoid(sm)
    g_all = -jnp.exp(alog_ref[...]) * _softplus(sm + dtb_ref[...])

    r = lax.broadcasted_iota(jnp.int32, (tb, tb), 0)
    c = lax.broadcasted_iota(jnp.int32, (tb, tb), 1)
    same = (r // cc) == (c // cc)
    tri_f = jnp.where(same, jnp.where(c <= r, 1.0, 0.0), 0.0).astype(BF16)
    tri_b = jnp.where(same, jnp.where(c >= r, 1.0, 0.0), 0.0).astype(BF16)
    gcs = (_dot_exact_lhs(tri_f, g_all), _dot_exact_lhs(tri_b, g_all))

    ii = lax.broadcasted_iota(jnp.int32, (cc, cc), 0)
    jj = lax.broadcasted_iota(jnp.int32, (cc, cc), 1)
    eye = jnp.where(ii == jj, 1.0, 0.0)
    blk16 = (ii // 16) == (jj // 16)
    blk32 = (ii // 32) == (jj // 32)
    off32 = jnp.logical_and(blk32, jnp.logical_not(blk16))
    off64 = jnp.logical_not(blk32)
    b16 = lambda m: m.astype(BF16)
    strict = (ii > jj, ii < jj)

    gl_ref[...] = jnp.zeros_like(gl_ref)
    qs, ks = [], []
    for h in range(DN_HEADS):
        hs = slice(h * HEAD, (h + 1) * HEAD)
        qh, kh = q[:, hs], k[:, hs]
        qs.append(qh * lax.rsqrt(jnp.sum(qh * qh, axis=-1, keepdims=True) + NORM_EPS) * (HEAD ** -0.5))
        ks.append(kh * lax.rsqrt(jnp.sum(kh * kh, axis=-1, keepdims=True) + NORM_EPS))

    def chunk_problems(ch):
        rows = slice(ch * cc, (ch + 1) * cc)
        gc = [gcs[d][rows] for d in range(2)]
        gct = [g.T for g in gc]
        kk, qk0 = [], []
        for h in range(DN_HEADS):
            k16 = b16(ks[h][rows])
            kk.append(_dg(k16, k16, NT))
            qk0.append(_dg(b16(qs[h][rows]), k16, NT))
        probs = []
        for d in range(2):
            end = cc - 1 if d == 0 else 0
            for h in range(DN_HEADS):
                hs = slice(h * HEAD, (h + 1) * HEAD)
                lb = SM_AB + d * 2 * DN_HEADS + h
                lg = lb + DN_HEADS
                beta = beta_all[rows, lb:lb + 1]
                gcol, grow = gc[d][:, lg:lg + 1], gct[d][lg:lg + 1, :]
                glast = gc[d][end:end + 1, lg:lg + 1]
                qh, kh, vh = qs[h][rows], ks[h][rows], v[rows, hs]
                decay = jnp.exp(jnp.where(strict[d], gcol - grow, NEG_BIG))
                eg = jnp.exp(gcol)
                kb = kh * beta
                lm = (beta * kk[h]) * decay
                qk_ref[d, h, rows, :] = (qk0[h] * (decay + eye)).astype(BF16)
                qd_ref[d, rows, hs] = (qh * eg).astype(BF16)
                kd_ref[d, rows, hs] = (kh * jnp.exp(glast - gcol)).astype(BF16)
                gl_ref[d, ch, h:h + 1, :] = jnp.broadcast_to(jnp.exp(glast), (1, HEAD))
                probs.append((d, hs, lm, jnp.concatenate([vh * beta, kb * eg], axis=1), rows))
        return probs

    for ch0 in range(0, tb // cc, DN_GROUP):
        probs = [pr for ch in range(ch0, ch0 + DN_GROUP) for pr in chunk_problems(ch)]
        lms = [pr[2] for pr in probs]
        diag = [jnp.where(blk16, lm, 0.0) for lm in lms]
        xs = [eye - m for m in diag]
        pw = [b16(m) for m in diag]
        pw = [_dg(m, m) for m in pw]
        for _ in range(2):
            pw = [b16(m) for m in pw]
            xs = [x + _dg(b16(x), m) for x, m in zip(xs, pw)]
            pw = [_dg(m, m) for m in pw]
        xs = [x + _dg(b16(x), b16(m)) for x, m in zip(xs, pw)]
        for off in (off32, off64):
            x16 = [b16(x) for x in xs]
            cx = [_dg(b16(jnp.where(off, lm, 0.0)), x) for lm, x in zip(lms, x16)]
            xs = [x - _dg(xb, b16(c)) for x, xb, c in zip(xs, x16, cx)]
        sols = [_dg(b16(x), b16(pr[3])) for x, pr in zip(xs, probs)]
        for sol, (d, hs, _, _, rows) in zip(sols, probs):
            u_ref[d, rows, hs] = sol[:, :HEAD]
            w_ref[d, rows, hs] = sol[:, HEAD:].astype(BF16)


def _dn_prep(p, conv_w, alog_row, dtb_row, lat):
    t = p.shape[0]
    tb = TOK
    nb, nb_lat = t // tb, lat // tb
    rh = tb // HALO
    blk = lambda c0: pl.BlockSpec((tb, MIX), lambda i: (i, c0 // MIX))
    prev = lambda c0: pl.BlockSpec((HALO, MIX), lambda i: (jnp.maximum(i * rh - 1, 0), c0 // MIX))
    nxt = lambda c0: pl.BlockSpec((HALO, MIX), lambda i: (jnp.minimum((i + 1) * rh, t // HALO - 1), c0 // MIX))
    cols = (P_CQ, P_CK, P_CV)
    big = pl.BlockSpec((2, tb, MIX), lambda i: (0, i, 0))
    big_shape = lambda dt: jax.ShapeDtypeStruct((2, t, MIX), dt)
    return pl.pallas_call(
        functools.partial(_dn_prep_kernel, nb_lat=nb_lat),
        grid=(nb,),
        in_specs=[blk(c0) for c0 in cols] + [prev(c0) for c0 in cols] + [nxt(c0) for c0 in cols] + [
            pl.BlockSpec((4, 3 * MIX), lambda i: (0, 0)),
            pl.BlockSpec((tb, HEAD), lambda i: (i, P_SM // HEAD)),
            pl.BlockSpec((1, HEAD), lambda i: (0, 0)),
            pl.BlockSpec((1, HEAD), lambda i: (0, 0))],
        out_specs=[big, big, big, big,
                   pl.BlockSpec((2, DN_HEADS, tb, DN_CHUNK), lambda i: (0, 0, i, 0)),
                   pl.BlockSpec((2, tb // DN_CHUNK, 8, HEAD), lambda i: (0, i, 0, 0))],
        out_shape=[big_shape(F32), big_shape(BF16), big_shape(BF16), big_shape(BF16),
                   jax.ShapeDtypeStruct((2, DN_HEADS, t, DN_CHUNK), BF16),
                   jax.ShapeDtypeStruct((2, t // DN_CHUNK, 8, HEAD), F32)],
        compiler_params=_params("parallel"),
        name="dn_prep",
    )(*([p] * 9), conv_w, p, alog_row, dtb_row)


def _dn_scan_kernel(uf, wf, qdf, kdf, qkf, glf, ub, wb, qdb, kdb, qkb, glb, of_ref, ob_ref, s_ref):
    @pl.when(pl.program_id(0) == 0)
    def _():
        s_ref[...] = jnp.zeros_like(s_ref)

    cc = DN_CHUNK
    n_chunks = uf.shape[1] // cc
    dirs = ((uf, wf, qdf, kdf, qkf, glf, of_ref), (ub, wb, qdb, kdb, qkb, glb, ob_ref))
    probs = [(d, h, slice(h * HEAD, (h + 1) * HEAD)) for d in range(2) for h in range(DN_HEADS)]
    for step in range(n_chunks):
        ch = (step, n_chunks - 1 - step)
        rows = tuple(slice(c * cc, (c + 1) * cc) for c in ch)
        ws = [_dg(jnp.concatenate([dirs[d][1][0, rows[d], hs], dirs[d][2][0, rows[d], hs]], axis=0),
                  s_ref[d, h].astype(BF16)) for d, h, hs in probs]
        v_new = [(dirs[d][0][0, rows[d], hs] - t[:cc]).astype(BF16) for t, (d, h, hs) in zip(ws, probs)]
        outs = [t[cc:] + _dg(dirs[d][4][0, h, rows[d], :], vn) for t, vn, (d, h, hs) in zip(ws, v_new, probs)]
        upd = [_dg(dirs[d][3][0, rows[d], hs], vn, TN) for vn, (d, h, hs) in zip(v_new, probs)]
        for o, ds, (d, h, hs) in zip(outs, upd, probs):
            dirs[d][6][rows[d], hs] = o
            s_ref[d, h] = s_ref[d, h] * dirs[d][5][0, ch[d], h:h + 1, :] + ds


def _dn_scan(u, w, qd, kd, qk, gl, lat):
    t = u.shape[1]
    tb = TOK
    cpb = tb // DN_CHUNK
    n, n_lat = t // tb, lat // tb
    n_ctx = n - n_lat
    bf = lambda i: jnp.where(i < n_ctx, n_lat + i, i - n_ctx)
    bb = lambda i: n - 1 - i
    specs = []
    for d, blk in ((0, bf), (1, bb)):
        big = pl.BlockSpec((1, tb, MIX), lambda i, d=d, blk=blk: (d, blk(i), 0))
        specs += [big, big, big, big,
                  pl.BlockSpec((1, DN_HEADS, tb, DN_CHUNK), lambda i, d=d, blk=blk: (d, 0, blk(i), 0)),
                  pl.BlockSpec((1, cpb, 8, HEAD), lambda i, d=d, blk=blk: (d, blk(i), 0, 0))]
    return pl.pallas_call(
        _dn_scan_kernel,
        grid=(n,),
        in_specs=specs,
        out_specs=[pl.BlockSpec((tb, MIX), lambda i: (bf(i), 0)), pl.BlockSpec((tb, MIX), lambda i: (bb(i), 0))],
        out_shape=[jax.ShapeDtypeStruct((t, MIX), F32)] * 2,
        scratch_shapes=[pltpu.VMEM((2, DN_HEADS, HEAD, HEAD), F32)],
        compiler_params=_params("arbitrary"),
        name="dn_scan",
    )(u, w, qd, kd, qk, gl, u, w, qd, kd, qk, gl)


def _dn_out_kernel(of_ref, ob_ref, z_ref, nw_ref, o_ref):
    for h in range(DN_HEADS):
        hs = slice(h * HEAD, (h + 1) * HEAD)
        o = of_ref[:, hs] + ob_ref[:, hs]
        o_ref[:, hs] = (_rms(o, nw_ref[...], HEAD) * _silu(z_ref[:, hs].astype(F32))).astype(BF16)


def _dn_out(o_f, o_b, p, norm_w):
    t = p.shape[0]
    tb = _row_tile(t)
    row = lambda c: pl.BlockSpec((tb, MIX), lambda i: (i, c))
    return pl.pallas_call(
        _dn_out_kernel,
        grid=(t // tb,),
        in_specs=[row(0), row(0), row(P_CZ // MIX), pl.BlockSpec((1, HEAD), lambda i: (0, 0))],
        out_specs=row(0),
        out_shape=jax.ShapeDtypeStruct((t, MIX), BF16),
        compiler_params=_params("parallel"),
        name="dn_out",
    )(o_f, o_b, p, norm_w.reshape(1, HEAD))


def _arrange_w_in(w_in):
    pad = jnp.zeros(w_in.shape[:2] + (P_W - w_in.shape[2],), w_in.dtype)
    tail = [w_in[..., 5328:5840],
            w_in[..., 5008:5264],
            w_in[..., 4624:5008],
            w_in[..., 5264:5328],
            w_in[..., 4608:4624],
            pad]
    def tiles(w):
        l, d, n = w.shape
        return w.reshape(l, d, n // W_TILE, W_TILE).transpose(0, 2, 1, 3).astype(BF16)

    return tiles(w_in[..., :P_MAIN]), tiles(jnp.concatenate(tail, axis=-1))


def _arrange_mla_wq(w_uq):
    l, r, _ = w_uq.shape
    w = w_uq.reshape(l, r, MLA_HEADS, MLA_QK)
    w = jnp.pad(w, ((0, 0), (0, 0), (0, 0), (0, 2 * HEAD - MLA_QK)))
    return w.reshape(l, r, MLA_HEADS * 2 * HEAD).astype(BF16)


def _pad_qk_norm(w):
    return jnp.pad(w, ((0, 0), (0, 2 * HEAD - MLA_QK)))[:, None, :]


def _small_lane_row(vals):
    l = vals.shape[0]
    row = jnp.zeros((l, HEAD), F32)
    for d in range(2):
        lo = SM_AB + d * 2 * DN_HEADS + DN_HEADS
        row = row.at[:, lo:lo + DN_HEADS].set(vals[:, d, :])
    return row[:, None, :]


def _arrange_lru_gates(w_a, b_a, w_x, b_x):
    w = jnp.concatenate([w_a[:, 0], w_x[:, 0], w_a[:, 1], w_x[:, 1]], axis=-1)
    l = b_a.shape[0]
    blk = lambda b, d: b[:, d].reshape(l, LRU_BLOCKS, 1, HEAD)
    b = jnp.concatenate([blk(b_a, 0), blk(b_x, 0), blk(b_a, 1), blk(b_x, 1)], axis=-1)
    return w.astype(BF16), b


def _rope_tables(lat, ctx):
    n_rows = lat // GRID_W
    pos = jnp.arange(max(n_rows, GRID_W), dtype=F32)
    lane = jnp.arange(HEAD)

    def table(half, width):
        inv_freq = ROPE_THETA ** (-jnp.arange(half, dtype=F32) / half)
        ang = pos[:, None] * inv_freq[lane % half][None, :]
        live = (lane < width)[None, :]
        by_row = ((lane // (2 * half)) == 0)[None, :]
        sign = jnp.where((lane % (2 * half)) < half, -1.0, 1.0)[None, :]

        def expand(f):
            per_row = jnp.repeat(f[:n_rows], GRID_W, axis=0)
            per_col = jnp.tile(f[:GRID_W], (n_rows, 1))
            return jnp.where(by_row, per_row, per_col)

        cos = jnp.where(live, expand(jnp.cos(ang)), 0.0)
        sin = jnp.where(live, expand(jnp.sin(ang)), 0.0) * sign
        cos_c = jnp.broadcast_to(jnp.where(live, 1.0, 0.0), (ctx, HEAD))
        return (jnp.concatenate([cos, cos_c], axis=0).astype(F32),
                jnp.concatenate([sin, jnp.zeros((ctx, HEAD), F32)], axis=0).astype(F32))

    return table(32, HEAD), table(16, MLA_ROPE)


def kernel(x, c, ctx, c_ctx, norm_w, w_ada, b_ada, w_in, w_out, attn_q_norm, attn_k_norm, lru_conv_w, lru_conv_b, lru_w_a, lru_b_a, lru_w_x, lru_b_x, lru_lambda, dn_conv_w, dn_a_log, dn_dt_bias, dn_norm_w, mla_q_norm, mla_kv_norm, mla_w_uq, mla_w_ukv, mla_q_qk_norm, mla_k_qk_norm):
    assert x.shape[0] == 1 and ctx.shape[1] == TOK and x.shape[1] % (8 * TOK) == 0
    lat, n_ctx = x.shape[1], ctx.shape[1]
    depth = w_in.shape[0]

    xs = jnp.concatenate([x[0], ctx[0]], axis=0)
    mod = _ada(jnp.stack([c[0], c_ctx], axis=1), w_ada, b_ada)
    (cos_a, sin_a), (cos_m, sin_m) = _rope_tables(lat, n_ctx)
    w_main, w_tail = _arrange_w_in(w_in)
    w_out_r = w_out.reshape(depth, 4, MIX, D_MODEL).astype(BF16)
    wq_r = _arrange_mla_wq(mla_w_uq)
    wkv_r = mla_w_ukv.astype(BF16)
    qn_r, kn_r = _pad_qk_norm(mla_q_qk_norm), _pad_qk_norm(mla_k_qk_norm)
    alog_r, dtb_r = _small_lane_row(dn_a_log), _small_lane_row(dn_dt_bias)
    lru_w, lru_b = _arrange_lru_gates(lru_w_a, lru_b_a, lru_w_x, lru_b_x)

    for l in range(depth):
        p = _inproj(xs, norm_w[l], mod[l, 0], mod[l, 1], w_main, w_tail, l, lat)
        qa, ka, vta = _gqa_prep(p, cos_a, sin_a, attn_q_norm[l], attn_k_norm[l])
        bound_a = jnp.max(jnp.abs(attn_q_norm[l])) * jnp.max(jnp.abs(attn_k_norm[l])) * (HEAD ** 0.5 * LOG2E)
        y_a = _flash(qa, ka, p, P_AV, vta, p, P_AZ, A_HEADS, A_KV_HEADS, HEAD, lat, bound_a, "gqa_attn")
        y_b = _lru(p, lru_conv_w[l], lru_conv_b[l], lru_w[l], lru_b[l], lru_lambda[l], lat)
        u, w, qd, kd, qk, gl = _dn_prep(p, dn_conv_w[l], alog_r[l], dtb_r[l], lat)
        o_f, o_b = _dn_scan(u, w, qd, kd, qk, gl, lat)
        y_c = _dn_out(o_f, o_b, p, dn_norm_w[l])
        qm, km, vm, vtm = _mla_prep(p, cos_m, sin_m, mla_q_norm[l], mla_kv_norm[l], wq_r[l], wkv_r[l],
                               qn_r[l], kn_r[l])
        bound_d = (jnp.max(jnp.abs(mla_q_qk_norm[l])) * jnp.max(jnp.abs(mla_k_qk_norm[l]))
                   * (MLA_QK ** 0.5 * LOG2E))
        y_d = _flash(qm, km, vm, 0, vtm, p, P_DZ, MLA_HEADS, MLA_HEADS, 2 * HEAD, lat, bound_d, "mla_attn")
        last = l == depth - 1
        xs = _outproj((y_a, y_b, y_c, y_d), w_out_r, l, xs, mod[l, 2], lat, lat if last else lat + n_ctx)
    return xs[None]
```
